```python
import math
import jax, jax.numpy as jnp
from jax import lax
import numpy as np

D_MODEL = 1024
BATCH = 8
SEQ = 2048
DEPTH = 2
DEC_BATCH = 128
DEC_SEQ = 8
PAST_LEN = 16384
PAGE_SIZE = 128

N_AB = (DEPTH + 1) // 2
N_CONV = DEPTH // 2
A_HEADS = 8
A_HEAD_DIM = 64
A_WIDTH = A_HEADS * A_HEAD_DIM
DECAY_LORA = 64
AAA_LORA = 64
GATE_LORA = 128
A_PROJ = 3 * A_WIDTH + DECAY_LORA + AAA_LORA + GATE_LORA
A_SPLITS = (A_WIDTH, 2 * A_WIDTH, 3 * A_WIDTH, 3 * A_WIDTH + DECAY_LORA, 3 * A_WIDTH + DECAY_LORA + AAA_LORA)
A_GN_EPS = 64e-5
B_HEADS = 8
B_HEAD_DIM = 64
B_WIDTH = B_HEADS * B_HEAD_DIM
B_PROJ = 4 * B_WIDTH
RET_CHUNK = 128
ROPE_BASE = 10000.0
B_GN_EPS = 1e-5
CONV_W = 3
D_FF = 4 * D_MODEL
LN_EPS = 1e-5
ALPHA = (2.0 * DEPTH) ** 0.25
BETA = (8.0 * DEPTH) ** -0.25

kernel_name = 'hybrid_rwkv7_retnet_shortconv_step'


def layer_norm(x, g, b):
    xf = x.astype(jnp.float32)
    mu = jnp.mean(xf, -1, keepdims=True)
    var = jnp.mean(jnp.square(xf - mu), -1, keepdims=True)
    return ((xf - mu) * lax.rsqrt(var + LN_EPS)).astype(x.dtype) * g + b


def head_norm(y, g, b, eps):
    Bn, T, H, N = y.shape
    mu = jnp.mean(y, -1, keepdims=True)
    var = jnp.mean(jnp.square(y - mu), -1, keepdims=True)
    yn = ((y - mu) * lax.rsqrt(var + eps)).reshape(Bn, T, H * N)
    return yn * g.astype(jnp.float32) + b.astype(jnp.float32)


def rope(x, pos):
    half = x.shape[-1] // 2
    inv = ROPE_BASE ** (-jnp.arange(half, dtype=jnp.float32) / half)
    ang = pos.astype(jnp.float32)[:, None] * inv[None, :]
    cos = jnp.cos(ang)[None, :, None, :]
    sin = jnp.sin(ang)[None, :, None, :]
    x1, x2 = x[..., :half], x[..., half:]
    return jnp.concatenate([x1 * cos - x2 * sin, x2 * cos + x1 * sin], axis=-1)


def rwkv7_mixer(p, p_last, S0, mu, w0, w2, a0, a2, g2, k_k, k_a, r_k, lnx_g, lnx_b):
    Bn, T, _ = p.shape
    p_prev = jnp.concatenate([p_last[:, None, :].astype(p.dtype), p[:, :-1]], axis=1)
    m = p + (p_prev - p) * mu
    r, k, v, wd, ad, gd = jnp.split(m, A_SPLITS, axis=-1)
    w = -jax.nn.softplus(-(w0 + jnp.tanh(wd) @ w2)) - 0.5
    decay = jnp.exp(-jnp.exp(w.astype(jnp.float32)))
    a = jax.nn.sigmoid(a0 + ad @ a2)
    g = jax.nn.sigmoid(gd) @ g2
    kk = k * k_k
    k = k * (1.0 + (a - 1.0) * k_a)
    heads = lambda t: t.astype(jnp.float32).reshape(Bn, T, A_HEADS, A_HEAD_DIM)
    r, k, v, kk, a, decay = map(heads, (r, k, v, kk, a, decay))
    kk = kk * lax.rsqrt(jnp.maximum(jnp.sum(kk * kk, -1, keepdims=True), 1e-24))

    def step(S, inp):
        r_t, k_t, v_t, kk_t, a_t, w_t = inp
        sa = jnp.einsum('bhvk,bhk->bhv', S, -kk_t)
        S = (S * w_t[:, :, None, :] + sa[..., None] * (kk_t * a_t)[:, :, None, :]
             + v_t[..., None] * k_t[:, :, None, :])
        return S, jnp.einsum('bhvk,bhk->bhv', S, r_t)

    xs = tuple(jnp.swapaxes(t, 0, 1) for t in (r, k, v, kk, a, decay))
    S, o = lax.scan(step, S0.astype(jnp.float32), xs)
    o = jnp.swapaxes(o, 0, 1)
    bonus = (jnp.sum(r * k * r_k.astype(jnp.float32), -1, keepdims=True) * v).reshape(Bn, T, A_WIDTH)
    y = (head_norm(o, lnx_g, lnx_b, A_GN_EPS) + bonus) * g.astype(jnp.float32)
    return y, p[:, -1], S.astype(S0.dtype)


def retention(q, k, v, S0, log_gamma):
    Bn, T, H, N = q.shape
    C = RET_CHUNK if T % RET_CHUNK == 0 else T
    nc = T // C
    chunks = lambda t: t.reshape(Bn, nc, C, H, N).transpose(1, 0, 3, 2, 4)
    idx = jnp.arange(C, dtype=jnp.float32)
    diff = idx[:, None] - idx[None, :]
    lg = log_gamma[:, None, None]
    dmask = jnp.where(diff[None] >= 0, jnp.exp(lg * jnp.maximum(diff, 0.0)[None]), 0.0)
    q_decay = jnp.exp(log_gamma[:, None] * (idx + 1.0)[None, :])[None, :, :, None]
    k_decay = jnp.exp(log_gamma[:, None] * (C - 1.0 - idx)[None, :])[None, :, :, None]
    c_decay = jnp.exp(log_gamma * C)[None, :, None, None]

    def step(S, inp):
        qi, ki, vi = inp
        scores = jnp.einsum('bhin,bhjn->bhij', qi, ki) * dmask
        inner = jnp.einsum('bhij,bhjn->bhin', scores, vi)
        cross = jnp.einsum('bhik,bhkv->bhiv', qi, S) * q_decay
        S = S * c_decay + jnp.einsum('bhjk,bhjv->bhkv', ki * k_decay, vi)
        return S, inner + cross

    S, o = lax.scan(step, S0, (chunks(q), chunks(k), chunks(v)))
    return o.transpose(1, 0, 3, 2, 4).reshape(Bn, T, H, N), S


def ab_mixer(x, shift0, wkv0, ret0, pos, w_in, mu, w0, w2, a0, a2, g2, k_k, k_a, r_k,
             lnx_g, lnx_b, gn_g, gn_b, w_out):
    Bn, T, _ = x.shape
    proj = x @ w_in
    pa, pb = proj[..., :A_PROJ], proj[..., A_PROJ:]
    ya, shift1, wkv1 = rwkv7_mixer(pa, shift0, wkv0, mu, w0, w2, a0, a2, g2, k_k, k_a, r_k, lnx_g, lnx_b)
    q, kr, vr, gate = jnp.split(pb, 4, axis=-1)
    hb = lambda t: t.astype(jnp.float32).reshape(Bn, T, B_HEADS, B_HEAD_DIM)
    q = rope(hb(q), pos)
    kr = rope(hb(kr), pos) * (B_HEAD_DIM ** -0.5)
    log_gamma = jnp.log1p(-jnp.exp2(-5.0 - jnp.arange(B_HEADS, dtype=jnp.float32)))
    ob, ret1 = retention(q, kr, hb(vr), ret0.astype(jnp.float32), log_gamma)
    yb = jax.nn.silu(gate.astype(jnp.float32)) * head_norm(ob, gn_g, gn_b, B_GN_EPS)
    y = jnp.concatenate([ya, yb], axis=-1).astype(x.dtype) @ w_out
    return y, shift1, wkv1, ret1.astype(ret0.dtype)


def conv_mixer(x, buf0, w_in, conv_w, w_out):
    T = x.shape[1]
    bg, cg, h = jnp.split(x @ w_in, 3, axis=-1)
    u = cg * h
    ext = jnp.concatenate([buf0.astype(u.dtype), u], axis=1)
    conv = sum(ext[:, j:j + T] * conv_w[j] for j in range(CONV_W))
    return (bg * conv) @ w_out, ext[:, ext.shape[1] - (CONV_W - 1):]


def sqrelu_mlp(x, w_up, w_down):
    return jnp.square(jax.nn.relu(x @ w_up)) @ w_down


def trunk(x, st_shift, st_wkv, st_ret, st_conv, pos, w_in_ab, mu_a, w0, w2, a0, a2, g2, k_k, k_a, r_k,
          lnx_g, lnx_b, gn_g, gn_b, w_out_ab, w_in_conv, conv_w, w_out_conv,
          ln1_g, ln1_b, ln2_g, ln2_b, w_up, w_down):
    shifts, wkvs, rets, convs = [], [], [], []
    for l in range(DEPTH):
        i = l // 2
        if l % 2 == 0:
            y, s1, wk1, r1 = ab_mixer(x, st_shift[i], st_wkv[i], st_ret[i], pos, w_in_ab[i], mu_a[i], w0[i],
                                      w2[i], a0[i], a2[i], g2[i], k_k[i], k_a[i], r_k[i], lnx_g[i], lnx_b[i],
                                      gn_g[i], gn_b[i], w_out_ab[i])
            shifts.append(s1)
            wkvs.append(wk1)
            rets.append(r1)
        else:
            y, c1 = conv_mixer(x, st_conv[i], w_in_conv[i], conv_w[i], w_out_conv[i])
            convs.append(c1)
        x = layer_norm(ALPHA * x + y, ln1_g[l], ln1_b[l])
        x = layer_norm(ALPHA * x + sqrelu_mlp(x, w_up[l], w_down[l]), ln2_g[l], ln2_b[l])
    return x, jnp.stack(shifts), jnp.stack(wkvs), jnp.stack(rets), jnp.stack(convs)


def setup_inputs(seed: int = 0) -> dict:
    key = jax.random.key(seed)
    ks = jax.random.split(key, 30)
    f32 = jnp.float32
    nrm = lambda k, shape, s: s * jax.random.normal(k, shape, f32)
    uni = lambda k, shape, lo, hi: jax.random.uniform(k, shape, f32, lo, hi)
    return {
        'x_prompt': nrm(ks[0], (BATCH, SEQ, D_MODEL), 1.0),
        'x_sample': nrm(ks[1], (DEC_BATCH, DEC_SEQ, D_MODEL), 1.0),
        'state_shift': nrm(ks[2], (N_AB, DEC_BATCH, A_PROJ), 1.0),
        'state_wkv': nrm(ks[3], (N_AB, DEC_BATCH, A_HEADS, A_HEAD_DIM, A_HEAD_DIM), 0.5),
        'state_ret': nrm(ks[4], (N_AB, DEC_BATCH, B_HEADS, B_HEAD_DIM, B_HEAD_DIM), 1.0),
        'state_conv': nrm(ks[5], (N_CONV, DEC_BATCH, CONV_W - 1, D_MODEL), 1.0),
        'w_in_ab': nrm(ks[6], (N_AB, D_MODEL, A_PROJ + B_PROJ), D_MODEL ** -0.5),
        'mu_a': uni(ks[7], (N_AB, A_PROJ), 0.0, 1.0),
        'w0': uni(ks[8], (N_AB, A_WIDTH), -4.0, 0.0),
        'w2': nrm(ks[9], (N_AB, DECAY_LORA, A_WIDTH), 0.5 * DECAY_LORA ** -0.5),
        'a0': nrm(ks[10], (N_AB, A_WIDTH), 0.1),
        'a2': nrm(ks[11], (N_AB, AAA_LORA, A_WIDTH), 0.5 * AAA_LORA ** -0.5),
        'g2': nrm(ks[12], (N_AB, GATE_LORA, A_WIDTH), GATE_LORA ** -0.5),
        'k_k': 0.85 + nrm(ks[13], (N_AB, A_WIDTH), 0.02),
        'k_a': 1.0 + nrm(ks[14], (N_AB, A_WIDTH), 0.02),
        'r_k': nrm(ks[15], (N_AB, A_HEADS, A_HEAD_DIM), 0.1),
        'lnx_g': 1.0 + nrm(ks[16], (N_AB, A_WIDTH), 0.01),
        'lnx_b': nrm(ks[17], (N_AB, A_WIDTH), 0.01),
        'gn_g': 1.0 + nrm(ks[18], (N_AB, B_WIDTH), 0.01),
        'gn_b': nrm(ks[19], (N_AB, B_WIDTH), 0.01),
        'w_out_ab': nrm(ks[20], (N_AB, A_WIDTH + B_WIDTH, D_MODEL), BETA * (A_WIDTH + B_WIDTH) ** -0.5),
        'w_in_conv': nrm(ks[21], (N_CONV, D_MODEL, 3 * D_MODEL), D_MODEL ** -0.5),
        'conv_w': nrm(ks[22], (N_CONV, CONV_W, D_MODEL), CONV_W ** -0.5),
        'w_out_conv': nrm(ks[23], (N_CONV, D_MODEL, D_MODEL), BETA * D_MODEL ** -0.5),
        'ln1_g': 1.0 + nrm(ks[24], (DEPTH, D_MODEL), 0.01),
        'ln1_b': nrm(ks[25], (DEPTH, D_MODEL), 0.01),
        'ln2_g': 1.0 + nrm(ks[26], (DEPTH, D_MODEL), 0.01),
        'ln2_b': nrm(ks[27], (DEPTH, D_MODEL), 0.01),
        'w_up': nrm(ks[28], (DEPTH, D_MODEL, D_FF), D_MODEL ** -0.5),
        'w_down': nrm(ks[29], (DEPTH, D_FF, D_MODEL), BETA * D_FF ** -0.5),
    }


def reference(x_prompt, x_sample, state_shift, state_wkv, state_ret, state_conv, w_in_ab, mu_a, w0, w2, a0, a2,
              g2, k_k, k_a, r_k, lnx_g, lnx_b, gn_g, gn_b, w_out_ab, w_in_conv, conv_w, w_out_conv,
              ln1_g, ln1_b, ln2_g, ln2_b, w_up, w_down):
    params = (w_in_ab, mu_a, w0, w2, a0, a2, g2, k_k, k_a, r_k, lnx_g, lnx_b, gn_g, gn_b, w_out_ab,
              w_in_conv, conv_w, w_out_conv, ln1_g, ln1_b, ln2_g, ln2_b, w_up, w_down)
    Bp, Tp, _ = x_prompt.shape
    dt = state_wkv.dtype
    z_shift = jnp.zeros((N_AB, Bp, A_PROJ), dt)
    z_wkv = jnp.zeros((N_AB, Bp, A_HEADS, A_HEAD_DIM, A_HEAD_DIM), dt)
    z_ret = jnp.zeros((N_AB, Bp, B_HEADS, B_HEAD_DIM, B_HEAD_DIM), dt)
    z_conv = jnp.zeros((N_CONV, Bp, CONV_W - 1, D_MODEL), dt)
    pos_p = jnp.arange(Tp, dtype=jnp.int32)
    y_prompt, p_shift, p_wkv, p_ret, p_conv = trunk(x_prompt, z_shift, z_wkv, z_ret, z_conv, pos_p, *params)
    pos_s = PAST_LEN + jnp.arange(x_sample.shape[1], dtype=jnp.int32)
    y_sample, s_shift, s_wkv, s_ret, s_conv = trunk(x_sample, state_shift, state_wkv, state_ret, state_conv,
                                                    pos_s, *params)
    return (y_prompt, y_sample, p_shift, p_wkv, p_ret, p_conv, s_shift, s_wkv, s_ret, s_conv)
```

```python
import functools

import jax
import jax.numpy as jnp
from jax import lax
from jax.experimental import pallas as pl
from jax.experimental.pallas import tpu as pltpu

F32 = jnp.float32
BF16 = jnp.bfloat16

D_MODEL = 1024
DEPTH = 2
PAST_LEN = 16384
HEADS = 8
HEAD_DIM = 64
WIDTH = HEADS * HEAD_DIM
DECAY_LORA = 64
AAA_LORA = 64
GATE_LORA = 128
A_PROJ = 3 * WIDTH + DECAY_LORA + AAA_LORA + GATE_LORA
B_PROJ = 4 * WIDTH
A_GN_EPS = 64e-5
B_GN_EPS = 1e-5
ROPE_BASE = 10000.0
CONV_W = 3
D_FF = 4 * D_MODEL
LN_EPS = 1e-5
ALPHA = (2.0 * DEPTH) ** 0.25

LANES = 128
SUBLANES = 8
PAIRS = WIDTH // LANES
ROWS = 64
VMEM_LIMIT = 56 * 1024 * 1024


def _bf(x):
    return x.astype(BF16)


def _dot(a, b):
    return jnp.dot(a, b, preferred_element_type=F32)


def _dot_tb(a, b):
    return lax.dot_general(a, b, (((1,), (1,)), ((), ())), preferred_element_type=F32)


def _dot_split(x, m, parts):
    acc = None
    rem = x
    for i in range(parts):
        hi = _bf(rem)
        term = _dot(hi, m)
        acc = term if acc is None else acc + term
        if i + 1 < parts:
            rem = rem - hi.astype(F32)
    return acc


def _iota2(shape, dim):
    return lax.broadcasted_iota(jnp.int32, shape, dim)


def _div2(x, n):
    assert n & (n - 1) == 0
    return lax.shift_right_logical(x, n.bit_length() - 1)


def _mod2(x, n):
    assert n & (n - 1) == 0
    return lax.bitwise_and(x, n - 1)


def _head_ones():
    ri = _iota2((LANES, LANES), 0)
    ci = _iota2((LANES, LANES), 1)
    return jnp.where(_div2(ri, HEAD_DIM) == _div2(ci, HEAD_DIM), 1.0, 0.0).astype(BF16)


def _headsum(x, ones_bd):
    outs = []
    for j in range(PAIRS):
        outs.append(_dot_split(x[:, LANES * j:LANES * (j + 1)], ones_bd, 2))
    return jnp.concatenate(outs, axis=1)


def _head_norm(o, g, b, eps, ones_bd):
    mu = _headsum(o, ones_bd) * (1.0 / HEAD_DIM)
    d = o - mu
    var = _headsum(d * d, ones_bd) * (1.0 / HEAD_DIM)
    return d * lax.rsqrt(var + eps) * g + b


def _layer_norm(z, g, b):
    mu = jnp.mean(z, axis=-1, keepdims=True)
    d = z - mu
    var = jnp.mean(d * d, axis=-1, keepdims=True)
    return d * lax.rsqrt(var + LN_EPS) * g + b


def _bdiag(x, lane_lo):
    zero = jnp.zeros_like(x)
    return jnp.concatenate([jnp.where(lane_lo, x, zero), jnp.where(lane_lo, zero, x)], axis=0)


def _stream_masks(ct):
    ri = _iota2((2 * ROWS, 2 * ROWS), 0)
    ci = _iota2((2 * ROWS, 2 * ROWS), 1)
    same = _div2(ri, ct) == _div2(ci, ct)
    return same & (ci < ri), same & (ci <= ri), ri == ci, _div2(ri, ROWS) == _div2(ci, ROWS)


def _load_state(s_scr, s0_ref, nb):
    z = jnp.zeros((HEAD_DIM, HEAD_DIM), F32)
    for s in range(nb):
        for j in range(PAIRS):
            top = jnp.concatenate([s0_ref[s, 2 * j], z], axis=1)
            bot = jnp.concatenate([z, s0_ref[s, 2 * j + 1]], axis=1)
            s_scr[s, j] = jnp.concatenate([top, bot], axis=0)


def _store_state(sout_ref, s_scr, nb):
    for s in range(nb):
        for j in range(PAIRS):
            s2 = s_scr[s, j]
            sout_ref[s, 2 * j] = s2[0:HEAD_DIM, 0:HEAD_DIM]
            sout_ref[s, 2 * j + 1] = s2[HEAD_DIM:, HEAD_DIM:]


def _seq_rows(x, q, ct):
    if ct == ROWS:
        return x
    return jnp.concatenate([x[q * ct:(q + 1) * ct], x[ROWS + q * ct:ROWS + (q + 1) * ct]], axis=0)


def _from_seq_rows(pieces, ct):
    if ct == ROWS:
        return pieces[0]
    return jnp.concatenate([p[0:ct] for p in pieces] + [p[ct:2 * ct] for p in pieces], axis=0)


def _pad_rows(x, rows):
    if x.shape[0] == rows:
        return x
    return jnp.concatenate([x, jnp.zeros((rows - x.shape[0], x.shape[1]), x.dtype)], axis=0)


def _inproj_kernel(x_ref, wa_ref, wb_ref, pa_ref, pb_ref):
    xb = _bf(x_ref[...])
    pa_ref[...] = _dot(xb, wa_ref[...])
    pb_ref[...] = _dot(xb, wb_ref[...])


def _inproj(x2d, wa, wb, tm):
    m = x2d.shape[0]
    const = lambda i: (0, 0)
    return pl.pallas_call(
        _inproj_kernel,
        grid=(m // tm,),
        in_specs=[
            pl.BlockSpec((tm, D_MODEL), lambda i: (i, 0)),
            pl.BlockSpec(wa.shape, const, pipeline_mode=pl.Buffered(1)),
            pl.BlockSpec(wb.shape, const, pipeline_mode=pl.Buffered(1)),
        ],
        out_specs=[
            pl.BlockSpec((tm, wa.shape[1]), lambda i: (i, 0)),
            pl.BlockSpec((tm, wb.shape[1]), lambda i: (i, 0)),
        ],
        out_shape=[
            jax.ShapeDtypeStruct((m, wa.shape[1]), F32),
            jax.ShapeDtypeStruct((m, wb.shape[1]), F32),
        ],
        compiler_params=pltpu.CompilerParams(
            dimension_semantics=("arbitrary",), vmem_limit_bytes=VMEM_LIMIT),
        name="inproj_ab",
    )(x2d, wa, wb)


def _rwkv_kernel(nb, tt, ct, pa_ref, shift_ref, s0_ref, mu_ref, w0_ref, w2p_ref, a0_ref, a2p_ref,
                 g2_ref, kk_ref, ka_ref, rk_ref, lng_ref, lnb_ref,
                 ya_ref, sout_ref,
                 s_scr, prev_scr, at_scr, rt_scr, bt_scr, kt_scr, v_scr, p_scr, o_scr):
    rows = nb * tt
    nchunk = rows // ROWS
    ns = ROWS // ct
    ti = pl.program_id(1)

    @pl.when(ti == 0)
    def _():
        _load_state(s_scr, s0_ref, nb)
        prev_scr[...] = shift_ref[...]

    ones_bd = _head_ones()

    p3 = pa_ref[...]
    p2 = p3.reshape(rows, A_PROJ)
    t3 = _iota2((nb, tt, A_PROJ), 1)
    rolled = pltpu.roll(p2, 1, axis=0).reshape(nb, tt, A_PROJ)
    pprev = jnp.where(t3 == 0, prev_scr[...], rolled).reshape(rows, A_PROJ)
    prev_scr[...] = p3[:, tt - 1:tt, :]
    m = p2 + (pprev - p2) * mu_ref[...]
    r = m[:, 0:WIDTH]
    k = m[:, WIDTH:2 * WIDTH]
    v = m[:, 2 * WIDTH:3 * WIDTH]
    wa = m[:, 3 * WIDTH:3 * WIDTH + LANES]
    gd = m[:, 3 * WIDTH + LANES:A_PROJ]
    z = -(w0_ref[...] + _dot(_bf(jnp.tanh(wa)), w2p_ref[...]))
    softplus = jnp.maximum(z, 0.0) + jnp.log1p(jnp.exp(-jnp.abs(z)))
    lw = -jnp.exp(-softplus - 0.5)
    a = jax.nn.sigmoid(a0_ref[...] + _dot(_bf(wa), a2p_ref[...]))
    g = _dot(_bf(jax.nn.sigmoid(gd)), g2_ref[...])
    kk = k * kk_ref[...]
    k = k * (1.0 + (a - 1.0) * ka_ref[...])
    kk = kk * lax.rsqrt(jnp.maximum(_headsum(kk * kk, ones_bd), 1e-24))
    bonus = _headsum(r * k * rk_ref[...], ones_bd) * v

    ri = _iota2((rows, rows), 0)
    ci = _iota2((rows, rows), 1)
    lcum = jnp.where((_div2(ri, ct) == _div2(ci, ct)) & (ci <= ri), 1.0, 0.0).astype(BF16)
    cs = _dot_split_lhs(lcum, lw, 3)
    pinv = jnp.exp(-cs)
    p_scr[...] = jnp.exp(cs)
    at_scr[...] = _bf(-kk * jnp.exp(cs - lw))
    rt_scr[...] = _bf(r * p_scr[...])
    bt_scr[...] = _bf(kk * a * pinv)
    kt_scr[...] = _bf(k * pinv)
    v_scr[...] = _bf(v)

    m_strict, m_incl, eye, bd64 = _stream_masks(ct)
    lane_lo = _iota2((ROWS, LANES), 1) < HEAD_DIM
    n_double = ct.bit_length() - 2

    def chunk(c, carry):
        r0 = c * ROWS if isinstance(c, int) else pl.multiple_of(c * ROWS, ROWS)
        rsl = pl.ds(r0, ROWS)
        for j in range(PAIRS):
            lsl = slice(LANES * j, LANES * (j + 1))
            aj, rj, bj, kj, vj = (s[rsl, lsl] for s in (at_scr, rt_scr, bt_scr, kt_scr, v_scr))
            a2, r2, b2, k2 = (_bdiag(x, lane_lo) for x in (aj, rj, bj, kj))
            vst = jnp.concatenate([vj, vj], axis=0)
            ar = jnp.concatenate([a2, r2], axis=0)
            sc = _dot_tb(ar, jnp.concatenate([bj, bj, kj, kj], axis=0))
            n = 2 * ROWS
            l_ab = jnp.where(m_strict, sc[0:n, 0:n], 0.0)
            l_ak = jnp.where(m_strict, sc[0:n, n:], 0.0)
            a_rb = jnp.where(m_incl, sc[n:, 0:n], 0.0)
            a_rk = jnp.where(m_incl, sc[n:, n:], 0.0)
            tinv = jnp.where(eye, 1.0, l_ab)
            pw = l_ab
            for _ in range(n_double):
                pwb = _bf(pw)
                pw = _dot(pwb, pwb)
                tinv = tinv + _dot(_bf(tinv), _bf(pw))
            lv = _dot(_bf(l_ak), vst)
            arkv = _dot(_bf(a_rk), vst)
            hx, ho = [], []
            for q in range(ns):
                seq = 0 if nb == 1 else (q * ct) // tt
                s2b = _bf(s_scr[seq, j])
                h = _dot_tb(jnp.concatenate([_seq_rows(a2, q, ct), _seq_rows(r2, q, ct)], axis=0), s2b)
                hx.append(h[0:2 * ct])
                ho.append(h[2 * ct:])
            x2 = _from_seq_rows(hx, ct) + lv
            u2 = _dot(_bf(tinv), _bf(x2))
            o2 = _from_seq_rows(ho, ct) + _dot(_bf(a_rb), _bf(u2)) + arkv
            o_scr[rsl, lsl] = jnp.where(lane_lo, o2[0:ROWS], o2[ROWS:])
            vst32 = vst.astype(F32)
            for q in range(ns):
                seq = 0 if nb == 1 else (q * ct) // tt
                uv = jnp.concatenate([_seq_rows(u2, q, ct), _seq_rows(vst32, q, ct)], axis=0)
                bk = jnp.concatenate([_seq_rows(b2, q, ct), _seq_rows(k2, q, ct)], axis=0)
                krows = max(4 * ct, LANES)
                ds = _dot(_bf(_pad_rows(uv, krows).T), _pad_rows(bk, krows))
                pc = p_scr[pl.ds(r0 + (q + 1) * ct - SUBLANES, SUBLANES), lsl][SUBLANES - 1:, :]
                s_scr[seq, j] = (s_scr[seq, j] + jnp.where(bd64, ds, 0.0)) * pc
        return carry

    if nchunk == 1:
        chunk(0, 0)
    else:
        lax.fori_loop(0, nchunk, chunk, 0)

    o = o_scr[...]
    y = (_head_norm(o, lng_ref[...], lnb_ref[...], A_GN_EPS, ones_bd) + bonus) * g
    ya_ref[...] = _bf(y)

    @pl.when(ti == pl.num_programs(1) - 1)
    def _():
        _store_state(sout_ref, s_scr, nb)


def _dot_split_lhs(m, x, parts):
    acc = None
    rem = x
    for i in range(parts):
        hi = _bf(rem)
        term = _dot(m, hi)
        acc = term if acc is None else acc + term
        if i + 1 < parts:
            rem = rem - hi.astype(F32)
    return acc


def _row_spec(shape):
    return pl.BlockSpec(shape, lambda b, t: (0,) * len(shape))


def _rwkv(pa3, shift0, s0, prm, nb, tt):
    bsz, tlen, _ = pa3.shape
    ct = min(ROWS, tt)
    rows = nb * tt
    assert rows % ROWS == 0 and ROWS % ct == 0 and bsz % nb == 0 and tlen % tt == 0
    assert nb == 1 or (tt == tlen and rows == ROWS)
    grid = (bsz // nb, tlen // tt)
    state_spec = pl.BlockSpec((nb, HEADS, HEAD_DIM, HEAD_DIM), lambda b, t: (b, 0, 0, 0))
    vec = lambda n: _row_spec((1, n))
    in_specs = [
        pl.BlockSpec((nb, tt, A_PROJ), lambda b, t: (b, t, 0)),
        pl.BlockSpec((nb, 1, A_PROJ), lambda b, t: (b, 0, 0)),
        state_spec,
        vec(A_PROJ), vec(WIDTH), _row_spec((LANES, WIDTH)), vec(WIDTH), _row_spec((LANES, WIDTH)),
        _row_spec((GATE_LORA, WIDTH)), vec(WIDTH), vec(WIDTH), vec(WIDTH), vec(WIDTH), vec(WIDTH),
    ]
    out_specs = [
        pl.BlockSpec((rows, WIDTH), lambda b, t: (b * (tlen // tt) + t, 0)),
        state_spec,
    ]
    scratch = [
        pltpu.VMEM((nb, PAIRS, LANES, LANES), F32),
        pltpu.VMEM((nb, 1, A_PROJ), F32),
        pltpu.VMEM((rows, WIDTH), BF16),
        pltpu.VMEM((rows, WIDTH), BF16),
        pltpu.VMEM((rows, WIDTH), BF16),
        pltpu.VMEM((rows, WIDTH), BF16),
        pltpu.VMEM((rows, WIDTH), BF16),
        pltpu.VMEM((rows, WIDTH), F32),
        pltpu.VMEM((rows, WIDTH), F32),
    ]
    return pl.pallas_call(
        functools.partial(_rwkv_kernel, nb, tt, ct),
        grid=grid,
        in_specs=in_specs,
        out_specs=out_specs,
        out_shape=[
            jax.ShapeDtypeStruct((bsz * tlen, WIDTH), BF16),
            jax.ShapeDtypeStruct(s0.shape, F32),
        ],
        scratch_shapes=scratch,
        compiler_params=pltpu.CompilerParams(
            dimension_semantics=("arbitrary", "arbitrary"), vmem_limit_bytes=VMEM_LIMIT),
        name="rwkv7_mixer",
    )(pa3, shift0.reshape(bsz, 1, A_PROJ), s0, *prm)


def _ret_kernel(nb, tt, ct, pos0, pb_ref, s0_ref, invf_ref, gng_ref, gnb_ref,
                yb_ref, sout_ref,
                s_scr, q_scr, k_scr, v_scr, o_scr):
    rows = nb * tt
    nchunk = rows // ROWS
    ns = ROWS // ct
    ti = pl.program_id(1)

    @pl.when(ti == 0)
    def _():
        _load_state(s_scr, s0_ref, nb)

    ones_bd = _head_ones()
    pb = pb_ref[...].reshape(rows, B_PROJ)
    q = pb[:, 0:WIDTH]
    k = pb[:, WIDTH:2 * WIDTH]
    gate = pb[:, 3 * WIDTH:]

    row = _iota2((rows, LANES), 0)
    pos = (pos0 + ti * tt + _mod2(row, tt)).astype(F32)
    ang = pos * invf_ref[...]
    cos = jnp.concatenate([jnp.cos(ang)] * PAIRS, axis=1)
    sin = jnp.concatenate([jnp.sin(ang)] * PAIRS, axis=1)
    first_half = _mod2(_iota2((rows, WIDTH), 1), HEAD_DIM) < (HEAD_DIM // 2)
    sin = jnp.where(first_half, -sin, sin)

    def rope(x):
        partner = jnp.where(first_half, pltpu.roll(x, WIDTH - HEAD_DIM // 2, axis=1),
                            pltpu.roll(x, HEAD_DIM // 2, axis=1))
        return x * cos + partner * sin

    q_scr[...] = _bf(rope(q))
    k_scr[...] = rope(k) * (HEAD_DIM ** -0.5)
    v_scr[...] = _bf(pb[:, 2 * WIDTH:3 * WIDTH])

    m_strict, m_incl, eye, bd64 = _stream_masks(ct)
    lane_lo = _iota2((ROWS, LANES), 1) < HEAD_DIM
    ri = _iota2((2 * ROWS, 2 * ROWS), 0)
    ci = _iota2((2 * ROWS, 2 * ROWS), 1)
    tpos = _mod2(ri, ct).astype(F32)
    tdiff = (_mod2(ri, ct) - _mod2(ci, ct)).astype(F32)

    def chunk(c, carry):
        r0 = c * ROWS if isinstance(c, int) else pl.multiple_of(c * ROWS, ROWS)
        rsl = pl.ds(r0, ROWS)
        for j in range(PAIRS):
            lsl = slice(LANES * j, LANES * (j + 1))
            head = (2 * j + _div2(ri, ROWS)).astype(F32)
            lg = jnp.log1p(-jnp.exp2(-5.0 - head))
            dmask = jnp.where(m_incl, jnp.exp(lg * jnp.maximum(tdiff, 0.0)), 0.0)
            qdec = jnp.exp(lg * (tpos + 1.0))
            kdec = jnp.exp(lg * (ct - 1.0 - tpos))
            cdec = jnp.exp(lg * ct)
            qj, vj = q_scr[rsl, lsl], v_scr[rsl, lsl]
            kj = k_scr[rsl, lsl]
            q2 = _bdiag(qj, lane_lo)
            vst = jnp.concatenate([vj, vj], axis=0)
            kjb = _bf(kj)
            scores = _dot_tb(q2, jnp.concatenate([kjb, kjb], axis=0)) * dmask
            o2 = _dot(_bf(scores), vst)
            k2d = _bdiag(kj, lane_lo) * kdec
            cross = []
            for s in range(ns):
                seq = 0 if nb == 1 else (s * ct) // tt
                cross.append(_dot(_seq_rows(q2, s, ct), _bf(s_scr[seq, j])))
            o2 = o2 + _from_seq_rows(cross, ct) * qdec
            o_scr[rsl, lsl] = jnp.where(lane_lo, o2[0:ROWS], o2[ROWS:])
            for s in range(ns):
                seq = 0 if nb == 1 else (s * ct) // tt
                kt = _bf(_pad_rows(_seq_rows(k2d, s, ct), LANES).T)
                ds = _dot(kt, _pad_rows(_seq_rows(vst, s, ct), LANES))
                s_scr[seq, j] = s_scr[seq, j] * cdec + jnp.where(bd64, ds, 0.0)
        return carry

    if nchunk == 1:
        chunk(0, 0)
    else:
        lax.fori_loop(0, nchunk, chunk, 0)

    o = o_scr[...]
    y = jax.nn.silu(gate) * _head_norm(o, gng_ref[...], gnb_ref[...], B_GN_EPS, ones_bd)
    yb_ref[...] = _bf(y)

    @pl.when(ti == pl.num_programs(1) - 1)
    def _():
        _store_state(sout_ref, s_scr, nb)


def _retention(pb3, s0, invf, gn_g, gn_b, nb, tt, pos0):
    bsz, tlen, _ = pb3.shape
    ct = min(ROWS, tt)
    rows = nb * tt
    assert rows % ROWS == 0 and ROWS % ct == 0 and bsz % nb == 0 and tlen % tt == 0
    assert nb == 1 or (tt == tlen and rows == ROWS)
    grid = (bsz // nb, tlen // tt)
    state_spec = pl.BlockSpec((nb, HEADS, HEAD_DIM, HEAD_DIM), lambda b, t: (b, 0, 0, 0))
    in_specs = [
        pl.BlockSpec((nb, tt, B_PROJ), lambda b, t: (b, t, 0)),
        state_spec,
        _row_spec((1, LANES)), _row_spec((1, WIDTH)), _row_spec((1, WIDTH)),
    ]
    out_specs = [
        pl.BlockSpec((rows, WIDTH), lambda b, t: (b * (tlen // tt) + t, 0)),
        state_spec,
    ]
    scratch = [
        pltpu.VMEM((nb, PAIRS, LANES, LANES), F32),
        pltpu.VMEM((rows, WIDTH), BF16),
        pltpu.VMEM((rows, WIDTH), F32),
        pltpu.VMEM((rows, WIDTH), BF16),
        pltpu.VMEM((rows, WIDTH), F32),
    ]
    return pl.pallas_call(
        functools.partial(_ret_kernel, nb, tt, ct, pos0),
        grid=grid,
        in_specs=in_specs,
        out_specs=out_specs,
        out_shape=[
            jax.ShapeDtypeStruct((bsz * tlen, WIDTH), BF16),
            jax.ShapeDtypeStruct(s0.shape, F32),
        ],
        scratch_shapes=scratch,
        compiler_params=pltpu.CompilerParams(
            dimension_semantics=("arbitrary", "arbitrary"), vmem_limit_bytes=VMEM_LIMIT),
        name="retention_mixer",
    )(pb3, s0, invf, gn_g, gn_b)


def _conv_kernel(nb, tt, x_ref, buf_ref, win_ref, cw_ref, z_ref, bout_ref, halo_scr):
    rows = nb * tt
    ti = pl.program_id(1)

    @pl.when(ti == 0)
    def _():
        halo_scr[...] = buf_ref[...]

    xb = _bf(x_ref[...].reshape(rows, D_MODEL))
    win = win_ref[...]
    bg = _dot(xb, win[:, 0:D_MODEL])
    u = _dot(xb, win[:, D_MODEL:2 * D_MODEL]) * _dot(xb, win[:, 2 * D_MODEL:])
    t3 = _iota2((nb, tt, D_MODEL), 1)
    h0 = halo_scr[:, 0:1, :]
    h1 = halo_scr[:, 1:2, :]
    u3 = u.reshape(nb, tt, D_MODEL)
    prev1 = jnp.where(t3 == 0, h1, pltpu.roll(u, 1, axis=0).reshape(nb, tt, D_MODEL))
    prev2 = jnp.where(t3 == 0, h0, jnp.where(t3 == 1, h1, pltpu.roll(u, 2, axis=0).reshape(nb, tt, D_MODEL)))
    cw = cw_ref[...]
    conv = prev2 * cw[0:1, :] + prev1 * cw[1:2, :] + u3 * cw[2:3, :]
    z_ref[...] = _bf(bg * conv.reshape(rows, D_MODEL))
    halo_scr[...] = u3[:, tt - (CONV_W - 1):, :]

    @pl.when(ti == pl.num_programs(1) - 1)
    def _():
        bout_ref[...] = halo_scr[...]


def _conv_mixer(x3, buf0, win, cw, nb, tt):
    bsz, tlen, _ = x3.shape
    rows = nb * tt
    assert bsz % nb == 0 and tlen % tt == 0 and tt >= CONV_W - 1
    grid = (bsz // nb, tlen // tt)
    buf_spec = pl.BlockSpec((nb, CONV_W - 1, D_MODEL), lambda b, t: (b, 0, 0))
    return pl.pallas_call(
        functools.partial(_conv_kernel, nb, tt),
        grid=grid,
        in_specs=[
            pl.BlockSpec((nb, tt, D_MODEL), lambda b, t: (b, t, 0)),
            buf_spec,
            pl.BlockSpec(win.shape, lambda b, t: (0, 0), pipeline_mode=pl.Buffered(1)),
            _row_spec((CONV_W, D_MODEL)),
        ],
        out_specs=[
            pl.BlockSpec((rows, D_MODEL), lambda b, t: (b * (tlen // tt) + t, 0)),
            buf_spec,
        ],
        out_shape=[
            jax.ShapeDtypeStruct((bsz * tlen, D_MODEL), BF16),
            jax.ShapeDtypeStruct(buf0.shape, F32),
        ],
        scratch_shapes=[pltpu.VMEM((nb, CONV_W - 1, D_MODEL), F32)],
        compiler_params=pltpu.CompilerParams(
            dimension_semantics=("arbitrary", "arbitrary"), vmem_limit_bytes=VMEM_LIMIT),
        name="conv_mixer",
    )(x3, buf0, win, cw)


FF_CHUNK = 1024


def _post_kernel(npieces, *refs):
    x_ref = refs[0]
    y_refs = refs[1:1 + npieces]
    wout_ref, g1_ref, b1_ref, wup_ref, wdown_ref, g2_ref, b2_ref, out_ref = refs[1 + npieces:]
    x = x_ref[...]
    y = None
    off = 0
    for y_ref in y_refs:
        n = y_ref.shape[1]
        term = _dot(y_ref[...], wout_ref[off:off + n, :])
        y = term if y is None else y + term
        off += n
    x1 = _layer_norm(ALPHA * x + y, g1_ref[...], b1_ref[...])
    x1b = _bf(x1)
    acc = None
    for f in range(D_FF // FF_CHUNK):
        h = jnp.maximum(_dot(x1b, wup_ref[:, f * FF_CHUNK:(f + 1) * FF_CHUNK]), 0.0)
        term = _dot(_bf(h * h), wdown_ref[f * FF_CHUNK:(f + 1) * FF_CHUNK, :])
        acc = term if acc is None else acc + term
    out_ref[...] = _layer_norm(ALPHA * x1 + acc, g2_ref[...], b2_ref[...])


def _post(x2d, ys, wout, g1, b1, wup, wdown, g2, b2, tm):
    m = x2d.shape[0]
    const = lambda i: (0, 0)
    resident = lambda a: pl.BlockSpec(a.shape, const, pipeline_mode=pl.Buffered(1))
    vec = pl.BlockSpec((1, D_MODEL), const)
    in_specs = [pl.BlockSpec((tm, D_MODEL), lambda i: (i, 0))]
    in_specs += [pl.BlockSpec((tm, y.shape[1]), lambda i: (i, 0)) for y in ys]
    in_specs += [resident(wout), vec, vec, resident(wup), resident(wdown), vec, vec]
    return pl.pallas_call(
        functools.partial(_post_kernel, len(ys)),
        grid=(m // tm,),
        in_specs=in_specs,
        out_specs=pl.BlockSpec((tm, D_MODEL), lambda i: (i, 0)),
        out_shape=jax.ShapeDtypeStruct((m, D_MODEL), F32),
        compiler_params=pltpu.CompilerParams(
            dimension_semantics=("arbitrary",), vmem_limit_bytes=VMEM_LIMIT),
        name="post_block",
    )(x2d, *ys, wout, g1, b1, wup, wdown, g2, b2)


def _trunk(x, st_shift, st_wkv, st_ret, st_conv, pos0, blk, w):
    bsz, tlen, _ = x.shape
    m = bsz * tlen
    x2d = x.reshape(m, D_MODEL)
    row = lambda a: a.reshape(1, -1)

    pa, pb = _inproj(x2d, w["w_a"], w["w_b"], blk["tm"])
    pa3 = pa.reshape(bsz, tlen, A_PROJ)
    rw_prm = (row(w["mu_a"]), row(w["w0"]), w["w2p"], row(w["a0"]), w["a2p"], w["g2"],
              row(w["k_k"]), row(w["k_a"]), row(w["r_k"]), row(w["lnx_g"]), row(w["lnx_b"]))
    ya, wkv1 = _rwkv(pa3, st_shift, st_wkv, rw_prm, blk["nb_mix"], blk["tt_mix"])
    yb, ret1 = _retention(pb.reshape(bsz, tlen, B_PROJ), st_ret, w["invf"], row(w["gn_g"]),
                          row(w["gn_b"]), blk["nb_mix"], blk["tt_mix"], pos0)
    shift1 = pa3[:, tlen - 1, :]
    x2d = _post(x2d, [ya, yb], w["w_out_ab"], row(w["ln1_g"][0]), row(w["ln1_b"][0]),
                w["w_up"][0], w["w_down"][0], row(w["ln2_g"][0]), row(w["ln2_b"][0]), blk["tm"])

    z, conv1 = _conv_mixer(x2d.reshape(bsz, tlen, D_MODEL), st_conv, w["w_in_conv"], w["conv_w"],
                           blk["nb_conv"], blk["tt_conv"])
    x2d = _post(x2d, [z], w["w_out_conv"], row(w["ln1_g"][1]), row(w["ln1_b"][1]),
                w["w_up"][1], w["w_down"][1], row(w["ln2_g"][1]), row(w["ln2_b"][1]), blk["tm"])
    return x2d.reshape(bsz, tlen, D_MODEL), shift1[None], wkv1[None], ret1[None], conv1[None]


def kernel(x_prompt, x_sample, state_shift, state_wkv, state_ret, state_conv, w_in_ab, mu_a, w0, w2, a0, a2,
           g2, k_k, k_a, r_k, lnx_g, lnx_b, gn_g, gn_b, w_out_ab, w_in_conv, conv_w, w_out_conv,
           ln1_g, ln1_b, ln2_g, ln2_b, w_up, w_down):
    bp, tp, _ = x_prompt.shape
    half = HEAD_DIM // 2
    inv = ROPE_BASE ** (-jnp.arange(half, dtype=F32) / half)
    zpad = jnp.zeros((LANES - DECAY_LORA, WIDTH), F32)
    w = {
        "w_a": _bf(w_in_ab[0][:, :A_PROJ]), "w_b": _bf(w_in_ab[0][:, A_PROJ:]),
        "mu_a": mu_a[0], "w0": w0[0], "a0": a0[0],
        "w2p": _bf(jnp.concatenate([w2[0], zpad], axis=0)),
        "a2p": _bf(jnp.concatenate([zpad, a2[0]], axis=0)),
        "g2": _bf(g2[0]), "k_k": k_k[0], "k_a": k_a[0], "r_k": r_k[0],
        "lnx_g": lnx_g[0], "lnx_b": lnx_b[0], "gn_g": gn_g[0], "gn_b": gn_b[0],
        "invf": jnp.tile(inv, LANES // half).reshape(1, LANES),
        "w_out_ab": _bf(w_out_ab[0]), "w_in_conv": _bf(w_in_conv[0]), "conv_w": conv_w[0],
        "w_out_conv": _bf(w_out_conv[0]),
        "ln1_g": ln1_g, "ln1_b": ln1_b, "ln2_g": ln2_g, "ln2_b": ln2_b,
        "w_up": _bf(w_up), "w_down": _bf(w_down),
    }
    dt = state_wkv.dtype
    z_shift = jnp.zeros((bp, A_PROJ), dt)
    z_state = jnp.zeros((bp, HEADS, HEAD_DIM, HEAD_DIM), dt)
    z_conv = jnp.zeros((bp, CONV_W - 1, D_MODEL), dt)
    blk_p = {"tm": 512, "nb_mix": 1, "tt_mix": 256, "nb_conv": 1, "tt_conv": 512}
    y_p, p_shift, p_wkv, p_ret, p_conv = _trunk(x_prompt, z_shift, z_state, z_state, z_conv, 0, blk_p, w)
    ts = x_sample.shape[1]
    blk_s = {"tm": 512, "nb_mix": ROWS // ts, "tt_mix": ts, "nb_conv": 256 // ts, "tt_conv": ts}
    y_s, s_shift, s_wkv, s_ret, s_conv = _trunk(x_sample, state_shift[0], state_wkv[0], state_ret[0],
                                                state_conv[0], PAST_LEN, blk_s, w)
    return (y_p, y_s, p_shift, p_wkv, p_ret, p_conv, s_shift, s_wkv, s_ret, s_conv)
```

```python
import functools

import jax
import jax.numpy as jnp
from jax import lax
from jax.experimental import pallas as pl
from jax.experimental.pallas import tpu as pltpu

F32 = jnp.float32
BF16 = jnp.bfloat16

D_MODEL = 1024
DEPTH = 2
PAST_LEN = 16384
HEADS = 8
HEAD_DIM = 64
WIDTH = HEADS * HEAD_DIM
DECAY_LORA = 64
AAA_LORA = 64
GATE_LORA = 128
A_PROJ = 3 * WIDTH + DECAY_LORA + AAA_LORA + GATE_LORA
B_PROJ = 4 * WIDTH
A_GN_EPS = 64e-5
B_GN_EPS = 1e-5
ROPE_BASE = 10000.0
CONV_W = 3
D_FF = 4 * D_MODEL
LN_EPS = 1e-5
ALPHA = (2.0 * DEPTH) ** 0.25

LANES = 128
SUBLANES = 8
PAIRS = WIDTH // LANES
ROWS = 64
N2 = 2 * ROWS
VMEM_LIMIT = 56 * 1024 * 1024


def _bf(x):
    return x.astype(BF16)


def _dot(a, b):
    return jnp.dot(a, b, preferred_element_type=F32)


def _dot_tb(a, b):
    return lax.dot_general(a, b, (((1,), (1,)), ((), ())), preferred_element_type=F32)


def _dot_split(x, m, parts):
    acc = None
    rem = x
    for i in range(parts):
        hi = _bf(rem)
        term = _dot(hi, m)
        acc = term if acc is None else acc + term
        if i + 1 < parts:
            rem = rem - hi.astype(F32)
    return acc


def _dot_split_lhs(m, x, parts):
    acc = None
    rem = x
    for i in range(parts):
        hi = _bf(rem)
        term = _dot(m, hi)
        acc = term if acc is None else acc + term
        if i + 1 < parts:
            rem = rem - hi.astype(F32)
    return acc


def _iota2(shape, dim):
    return lax.broadcasted_iota(jnp.int32, shape, dim)


def _div2(x, n):
    assert n & (n - 1) == 0
    return lax.shift_right_logical(x, n.bit_length() - 1)


def _mod2(x, n):
    assert n & (n - 1) == 0
    return lax.bitwise_and(x, n - 1)


def _head_ones():
    ri = _iota2((LANES, LANES), 0)
    ci = _iota2((LANES, LANES), 1)
    return jnp.where(_div2(ri, HEAD_DIM) == _div2(ci, HEAD_DIM), 1.0, 0.0).astype(BF16)


def _headsum(x, ones_bd):
    outs = []
    for j in range(PAIRS):
        outs.append(_dot_split(x[:, LANES * j:LANES * (j + 1)], ones_bd, 2))
    return jnp.concatenate(outs, axis=1)


def _head_norm(o, g, b, eps, ones_bd):
    mu = _headsum(o, ones_bd) * (1.0 / HEAD_DIM)
    d = o - mu
    var = _headsum(d * d, ones_bd) * (1.0 / HEAD_DIM)
    return d * lax.rsqrt(var + eps) * g + b


def _layer_norm(z, g, b):
    mu = jnp.mean(z, axis=-1, keepdims=True)
    d = z - mu
    var = jnp.mean(d * d, axis=-1, keepdims=True)
    return d * lax.rsqrt(var + LN_EPS) * g + b


def _bdiag(x, lane_lo):
    zero = jnp.zeros_like(x)
    return jnp.concatenate([jnp.where(lane_lo, x, zero), jnp.where(lane_lo, zero, x)], axis=0)


def _store_masks(mask_scr, ct):
    ri = _iota2((N2, N2), 0)
    ci = _iota2((N2, N2), 1)
    same = _div2(ri, ct) == _div2(ci, ct)
    one = lambda m: jnp.where(m, 1.0, 0.0)
    mask_scr[0] = one(same & (ci < ri))
    mask_scr[1] = one(same & (ci <= ri))
    mask_scr[2] = one(ri == ci)
    mask_scr[3] = one(_div2(ri, ROWS) == _div2(ci, ROWS))


def _load_state(s_scr, s0_ref, nb):
    z = jnp.zeros((HEAD_DIM, HEAD_DIM), F32)
    for s in range(nb):
        for j in range(PAIRS):
            top = jnp.concatenate([s0_ref[s, 2 * j], z], axis=1)
            bot = jnp.concatenate([z, s0_ref[s, 2 * j + 1]], axis=1)
            s_scr[s, j] = jnp.concatenate([top, bot], axis=0)


def _store_state(sout_ref, s_scr, nb):
    for s in range(nb):
        for j in range(PAIRS):
            s2 = s_scr[s, j]
            sout_ref[s, 2 * j] = s2[0:HEAD_DIM, 0:HEAD_DIM]
            sout_ref[s, 2 * j + 1] = s2[HEAD_DIM:, HEAD_DIM:]


def _seq_rows(x, q, ct):
    if ct == ROWS:
        return x
    return jnp.concatenate([x[q * ct:(q + 1) * ct], x[ROWS + q * ct:ROWS + (q + 1) * ct]], axis=0)


def _from_seq_rows(pieces, ct):
    if ct == ROWS:
        return pieces[0]
    return jnp.concatenate([p[0:ct] for p in pieces] + [p[ct:2 * ct] for p in pieces], axis=0)


def _pad_rows(x, rows):
    if x.shape[0] == rows:
        return x
    return jnp.concatenate([x, jnp.zeros((rows - x.shape[0], x.shape[1]), x.dtype)], axis=0)


def _emit_by_level(steps, envs):
    level = {}
    for name, deps, _ in steps:
        level[name] = 1 + max([level.get(d, 0) for d in deps], default=0)
    for lv in sorted(set(level.values())):
        for name, deps, fn in steps:
            if level[name] == lv:
                for env in envs:
                    env[name] = fn(*[env[d] for d in deps])


def _inverse_steps(nfac, eye):
    def square(q):
        f = _dot(q[1], q[1])
        return f, _bf(f)

    steps = [("q0", ("l",), lambda l: (l, _bf(l)))]
    for i in range(1, nfac):
        steps.append((f"q{i}", (f"q{i - 1}",), square))
    groups = []
    for m in range(nfac // 2):
        steps.append((f"g{m}", (f"q{2 * m}", f"q{2 * m + 1}"),
                      lambda qa, qb: eye() + qa[0] + qb[0] + _dot(qa[1], qb[1])))
        groups.append(f"g{m}")
    if nfac % 2:
        steps.append((f"g{nfac // 2}", (f"q{nfac - 1}",), lambda q: eye() + q[0]))
        groups.append(f"g{nfac // 2}")
    acc = groups[0]
    for m, gname in enumerate(groups[1:], 1):
        steps.append((f"t{m}", (acc, gname), lambda ta, g: _dot(_bf(ta), _bf(g))))
        acc = f"t{m}"
    steps.append(("tb", (acc,), _bf))
    return steps


def _inproj_kernel(x_ref, wa_ref, wb_ref, pa_ref, pb_ref):
    xb = _bf(x_ref[...])
    pa_ref[...] = _dot(xb, wa_ref[...])
    pb_ref[...] = _dot(xb, wb_ref[...])


def _inproj(x2d, wa, wb, tm):
    m = x2d.shape[0]
    const = lambda i: (0, 0)
    return pl.pallas_call(
        _inproj_kernel,
        grid=(m // tm,),
        in_specs=[
            pl.BlockSpec((tm, D_MODEL), lambda i: (i, 0)),
            pl.BlockSpec(wa.shape, const, pipeline_mode=pl.Buffered(1)),
            pl.BlockSpec(wb.shape, const, pipeline_mode=pl.Buffered(1)),
        ],
        out_specs=[
            pl.BlockSpec((tm, wa.shape[1]), lambda i: (i, 0)),
            pl.BlockSpec((tm, wb.shape[1]), lambda i: (i, 0)),
        ],
        out_shape=[
            jax.ShapeDtypeStruct((m, wa.shape[1]), F32),
            jax.ShapeDtypeStruct((m, wb.shape[1]), F32),
        ],
        compiler_params=pltpu.CompilerParams(
            dimension_semantics=("arbitrary",), vmem_limit_bytes=VMEM_LIMIT),
        name="inproj_ab",
    )(x2d, wa, wb)


def _rwkv_kernel(nb, tt, ct, pa_ref, shift_ref, s0_ref, mu_ref, w0_ref, w2p_ref, a0_ref, a2p_ref,
                 g2_ref, kk_ref, ka_ref, rk_ref, lng_ref, lnb_ref,
                 ya_ref, sout_ref,
                 s_scr, prev_scr, at_scr, rt_scr, bt_scr, kt_scr, v_scr, p_scr, o_scr, mask_scr, lcum_scr):
    rows = nb * tt
    nchunk = rows // ROWS
    ns = ROWS // ct
    ti = pl.program_id(1)

    @pl.when((pl.program_id(0) == 0) & (ti == 0))
    def _():
        _store_masks(mask_scr, ct)
        ri = _iota2((rows, rows), 0)
        ci = _iota2((rows, rows), 1)
        lcum_scr[...] = jnp.where((_div2(ri, ct) == _div2(ci, ct)) & (ci <= ri), 1.0, 0.0).astype(BF16)

    @pl.when(ti == 0)
    def _():
        _load_state(s_scr, s0_ref, nb)
        prev_scr[...] = shift_ref[...]

    ones_bd = _head_ones()

    p3 = pa_ref[...]
    p2 = p3.reshape(rows, A_PROJ)
    t3 = _iota2((nb, tt, A_PROJ), 1)
    rolled = pltpu.roll(p2, 1, axis=0).reshape(nb, tt, A_PROJ)
    pprev = jnp.where(t3 == 0, prev_scr[...], rolled).reshape(rows, A_PROJ)
    prev_scr[...] = p3[:, tt - 1:tt, :]
    m = p2 + (pprev - p2) * mu_ref[...]
    r = m[:, 0:WIDTH]
    k = m[:, WIDTH:2 * WIDTH]
    v = m[:, 2 * WIDTH:3 * WIDTH]
    wa = m[:, 3 * WIDTH:3 * WIDTH + LANES]
    gd = m[:, 3 * WIDTH + LANES:A_PROJ]
    z = -(w0_ref[...] + _dot(_bf(jnp.tanh(wa)), w2p_ref[...]))
    softplus = jnp.maximum(z, 0.0) + jnp.log1p(jnp.exp(-jnp.abs(z)))
    lw = -jnp.exp(-softplus - 0.5)
    a = jax.nn.sigmoid(a0_ref[...] + _dot(_bf(wa), a2p_ref[...]))
    g = _dot(_bf(jax.nn.sigmoid(gd)), g2_ref[...])
    kk = k * kk_ref[...]
    k = k * (1.0 + (a - 1.0) * ka_ref[...])
    kk = kk * lax.rsqrt(jnp.maximum(_headsum(kk * kk, ones_bd), 1e-24))
    bonus = _headsum(r * k * rk_ref[...], ones_bd) * v

    cs = _dot_split_lhs(lcum_scr[...], lw, 3)
    pinv = jnp.exp(-cs)
    p_scr[...] = jnp.exp(cs)
    at_scr[...] = _bf(-kk * jnp.exp(cs - lw))
    rt_scr[...] = _bf(r * p_scr[...])
    bt_scr[...] = _bf(kk * a * pinv)
    kt_scr[...] = _bf(k * pinv)
    v_scr[...] = _bf(v)

    lane_lo = _iota2((ROWS, LANES), 1) < HEAD_DIM
    m_strict = lambda: mask_scr[0]
    m_incl = lambda: mask_scr[1]
    eye = lambda: mask_scr[2]
    same_head = lambda: mask_scr[3]
    nfac = ct.bit_length() - 1

    envs = []
    for c in range(nchunk):
        for j in range(PAIRS):
            rsl = slice(c * ROWS, (c + 1) * ROWS)
            lsl = slice(LANES * j, LANES * (j + 1))
            aj, rj, bj, kj, vj = (s[rsl, lsl] for s in (at_scr, rt_scr, bt_scr, kt_scr, v_scr))
            envs.append({
                "c": c, "j": j, "rsl": rsl, "lsl": lsl,
                "a2": _bdiag(aj, lane_lo), "r2": _bdiag(rj, lane_lo),
                "b2": _bdiag(bj, lane_lo), "k2": _bdiag(kj, lane_lo),
                "bk": jnp.concatenate([bj, bj, kj, kj], axis=0),
                "vst": jnp.concatenate([vj, vj], axis=0),
            })

    steps = [
        ("sc", ("a2", "r2", "bk"), lambda a2, r2, bk: _dot_tb(jnp.concatenate([a2, r2], axis=0), bk)),
        ("l", ("sc",), lambda sc: sc[0:N2, 0:N2] * m_strict()),
        ("lkb", ("sc",), lambda sc: _bf(jnp.concatenate(
            [sc[0:N2, N2:] * m_strict(), sc[N2:, N2:] * m_incl()], axis=0))),
        ("arb", ("sc",), lambda sc: _bf(sc[N2:, 0:N2] * m_incl())),
        ("lvark", ("lkb", "vst"), _dot),
    ] + _inverse_steps(nfac, eye)

    if ns == 1:
        def f_ro(arb, tal, r2, lvark):
            x = _dot(arb, _bf(tal))
            return _bf(r2.astype(F32) + x[:, 0:N2]), x[:, N2:] + lvark[N2:]

        def f_mn(tal, b2, k2, vst):
            mm = _dot(_bf(tal.T), b2)
            n2 = _dot(_bf(vst.astype(F32).T), k2)
            return _bf(mm[0:N2]), (mm[N2:] + n2) * same_head()

        steps += [
            ("tal", ("tb", "a2", "lvark"),
             lambda tb, a2, lvark: _dot(tb, jnp.concatenate([a2, _bf(lvark[0:N2])], axis=1))),
            ("ro", ("arb", "tal", "r2", "lvark"), f_ro),
            ("mn", ("tal", "b2", "k2", "vst"), f_mn),
        ]
        _emit_by_level(steps, envs)
        state = [s_scr[0, j] for j in range(PAIRS)]
        for c in range(nchunk):
            cenv = envs[c * PAIRS:(c + 1) * PAIRS]
            sb = [_bf(s) for s in state]
            for j, env in enumerate(cenv):
                rp, oc = env["ro"]
                o2 = _dot_tb(rp, sb[j]) + oc
                o_scr[env["rsl"], env["lsl"]] = jnp.where(lane_lo, o2[0:ROWS], o2[ROWS:])
            for j, env in enumerate(cenv):
                mk, nn = env["mn"]
                pc = p_scr[(c + 1) * ROWS - 1:(c + 1) * ROWS, env["lsl"]]
                state[j] = (state[j] + _dot(sb[j], mk) + nn) * pc
        for j in range(PAIRS):
            s_scr[0, j] = state[j]
    else:
        for env in envs:
            env["sb"] = [_bf(s_scr[q, env["j"]]) for q in range(ns)]

        def f_h(a2, r2, sb):
            pieces = [_dot_tb(jnp.concatenate([_seq_rows(a2, q, ct), _seq_rows(r2, q, ct)], axis=0), sb[q])
                      for q in range(ns)]
            return (_from_seq_rows([p[0:2 * ct] for p in pieces], ct),
                    _from_seq_rows([p[2 * ct:] for p in pieces], ct))

        def f_ds(u2, vst, b2, k2):
            vst32 = vst.astype(F32)
            out = []
            for q in range(ns):
                uv = jnp.concatenate([_seq_rows(u2, q, ct), _seq_rows(vst32, q, ct)], axis=0)
                bk = jnp.concatenate([_seq_rows(b2, q, ct), _seq_rows(k2, q, ct)], axis=0)
                out.append(_dot(_bf(_pad_rows(uv, LANES).T), _pad_rows(bk, LANES)) * same_head())
            return out

        steps += [
            ("h", ("a2", "r2", "sb"), f_h),
            ("u2", ("tb", "h", "lvark"), lambda tb, h, lvark: _dot(tb, _bf(h[0] + lvark[0:N2]))),
            ("o2", ("arb", "u2", "h", "lvark"),
             lambda arb, u2, h, lvark: h[1] + _dot(arb, _bf(u2)) + lvark[N2:]),
            ("ds", ("u2", "vst", "b2", "k2"), f_ds),
        ]
        _emit_by_level(steps, envs)
        for env in envs:
            o2 = env["o2"]
            o_scr[env["rsl"], env["lsl"]] = jnp.where(lane_lo, o2[0:ROWS], o2[ROWS:])
            for q in range(ns):
                pc = p_scr[(q + 1) * ct - 1:(q + 1) * ct, env["lsl"]]
                s_scr[q, env["j"]] = (s_scr[q, env["j"]] + env["ds"][q]) * pc

    o = o_scr[...]
    y = (_head_norm(o, lng_ref[...], lnb_ref[...], A_GN_EPS, ones_bd) + bonus) * g
    ya_ref[...] = _bf(y)

    @pl.when(ti == pl.num_programs(1) - 1)
    def _():
        _store_state(sout_ref, s_scr, nb)


def _row_spec(shape):
    return pl.BlockSpec(shape, lambda b, t: (0,) * len(shape))


def _check_mixer_blocking(bsz, tlen, nb, tt):
    ct = min(ROWS, tt)
    rows = nb * tt
    assert rows % ROWS == 0 and ROWS % ct == 0 and bsz % nb == 0 and tlen % tt == 0
    assert nb == 1 or (tt == tlen and rows == ROWS)
    return ct, rows


def _rwkv(pa3, shift0, s0, prm, nb, tt):
    bsz, tlen, _ = pa3.shape
    ct, rows = _check_mixer_blocking(bsz, tlen, nb, tt)
    grid = (bsz // nb, tlen // tt)
    state_spec = pl.BlockSpec((nb, HEADS, HEAD_DIM, HEAD_DIM), lambda b, t: (b, 0, 0, 0))
    vec = lambda n: _row_spec((1, n))
    in_specs = [
        pl.BlockSpec((nb, tt, A_PROJ), lambda b, t: (b, t, 0)),
        pl.BlockSpec((nb, 1, A_PROJ), lambda b, t: (b, 0, 0)),
        state_spec,
        vec(A_PROJ), vec(WIDTH), _row_spec((LANES, WIDTH)), vec(WIDTH), _row_spec((LANES, WIDTH)),
        _row_spec((GATE_LORA, WIDTH)), vec(WIDTH), vec(WIDTH), vec(WIDTH), vec(WIDTH), vec(WIDTH),
    ]
    out_specs = [
        pl.BlockSpec((rows, WIDTH), lambda b, t: (b * (tlen // tt) + t, 0)),
        state_spec,
    ]
    scratch = [
        pltpu.VMEM((nb, PAIRS, LANES, LANES), F32),
        pltpu.VMEM((nb, 1, A_PROJ), F32),
        pltpu.VMEM((rows, WIDTH), BF16),
        pltpu.VMEM((rows, WIDTH), BF16),
        pltpu.VMEM((rows, WIDTH), BF16),
        pltpu.VMEM((rows, WIDTH), BF16),
        pltpu.VMEM((rows, WIDTH), BF16),
        pltpu.VMEM((rows, WIDTH), F32),
        pltpu.VMEM((rows, WIDTH), F32),
        pltpu.VMEM((4, N2, N2), F32),
        pltpu.VMEM((rows, rows), BF16),
    ]
    return pl.pallas_call(
        functools.partial(_rwkv_kernel, nb, tt, ct),
        grid=grid,
        in_specs=in_specs,
        out_specs=out_specs,
        out_shape=[
            jax.ShapeDtypeStruct((bsz * tlen, WIDTH), BF16),
            jax.ShapeDtypeStruct(s0.shape, F32),
        ],
        scratch_shapes=scratch,
        compiler_params=pltpu.CompilerParams(
            dimension_semantics=("arbitrary", "arbitrary"), vmem_limit_bytes=VMEM_LIMIT),
        name="rwkv7_mixer",
    )(pa3, shift0.reshape(bsz, 1, A_PROJ), s0, *prm)


def _ret_kernel(nb, tt, ct, pos0, pb_ref, s0_ref, invf_ref, gng_ref, gnb_ref,
                yb_ref, sout_ref,
                s_scr, q_scr, k_scr, v_scr, o_scr, dec_scr, mask_scr):
    rows = nb * tt
    nchunk = rows // ROWS
    ns = ROWS // ct
    ti = pl.program_id(1)

    @pl.when((pl.program_id(0) == 0) & (ti == 0))
    def _():
        _store_masks(mask_scr, ct)
        ri = _iota2((N2, N2), 0)
        ci = _iota2((N2, N2), 1)
        tpos = _mod2(ri, ct).astype(F32)
        tdiff = (_mod2(ri, ct) - _mod2(ci, ct)).astype(F32)
        for j in range(PAIRS):
            head = (2 * j + _div2(ri, ROWS)).astype(F32)
            lg = jnp.log1p(-jnp.exp2(-5.0 - head))
            dec_scr[j, 0] = mask_scr[1] * jnp.exp(lg * jnp.maximum(tdiff, 0.0))
            dec_scr[j, 1] = jnp.exp(lg * (tpos + 1.0))
            dec_scr[j, 2] = jnp.exp(lg * (ct - 1.0 - tpos))
            dec_scr[j, 3] = jnp.exp(lg * ct)

    @pl.when(ti == 0)
    def _():
        _load_state(s_scr, s0_ref, nb)

    ones_bd = _head_ones()
    pb = pb_ref[...].reshape(rows, B_PROJ)
    q = pb[:, 0:WIDTH]
    k = pb[:, WIDTH:2 * WIDTH]
    gate = pb[:, 3 * WIDTH:]

    row = _iota2((rows, LANES), 0)
    pos = (pos0 + ti * tt + _mod2(row, tt)).astype(F32)
    ang = pos * invf_ref[...]
    cos = jnp.concatenate([jnp.cos(ang)] * PAIRS, axis=1)
    sin = jnp.concatenate([jnp.sin(ang)] * PAIRS, axis=1)
    first_half = _mod2(_iota2((rows, WIDTH), 1), HEAD_DIM) < (HEAD_DIM // 2)
    sin = jnp.where(first_half, -sin, sin)

    def rope(x):
        partner = jnp.where(first_half, pltpu.roll(x, WIDTH - HEAD_DIM // 2, axis=1),
                            pltpu.roll(x, HEAD_DIM // 2, axis=1))
        return x * cos + partner * sin

    q_scr[...] = _bf(rope(q))
    k_scr[...] = rope(k) * (HEAD_DIM ** -0.5)
    v_scr[...] = _bf(pb[:, 2 * WIDTH:3 * WIDTH])

    lane_lo = _iota2((ROWS, LANES), 1) < HEAD_DIM
    same_head = lambda: mask_scr[3]

    envs = []
    for c in range(nchunk):
        for j in range(PAIRS):
            rsl = slice(c * ROWS, (c + 1) * ROWS)
            lsl = slice(LANES * j, LANES * (j + 1))
            qj, kj, vj = q_scr[rsl, lsl], k_scr[rsl, lsl], v_scr[rsl, lsl]
            kjb = _bf(kj)
            envs.append({
                "c": c, "j": j, "rsl": rsl, "lsl": lsl,
                "q2": _bdiag(qj, lane_lo),
                "kst": jnp.concatenate([kjb, kjb], axis=0),
                "k2d": _bdiag(kj, lane_lo) * dec_scr[j, 2],
                "vst": jnp.concatenate([vj, vj], axis=0),
                "dmask": lambda j=j: dec_scr[j, 0],
            })

    def f_ds(k2d, vst):
        return [_dot(_bf(_pad_rows(_seq_rows(k2d, s, ct), LANES).T), _pad_rows(_seq_rows(vst, s, ct), LANES))
                * same_head() for s in range(ns)]

    steps = [
        ("sc", ("q2", "kst", "dmask"), lambda q2, kst, dmask: _bf(_dot_tb(q2, kst) * dmask())),
        ("inner", ("sc", "vst"), _dot),
        ("ds", ("k2d", "vst"), f_ds),
    ]
    _emit_by_level(steps, envs)

    if ns == 1:
        state = [s_scr[0, j] for j in range(PAIRS)]
        for c in range(nchunk):
            for j in range(PAIRS):
                env = envs[c * PAIRS + j]
                env["sb"] = [_bf(state[j])]
                state[j] = state[j] * dec_scr[j, 3] + env["ds"][0]
        for j in range(PAIRS):
            s_scr[0, j] = state[j]
    else:
        for env in envs:
            j = env["j"]
            env["sb"] = [_bf(s_scr[s, j]) for s in range(ns)]
            for s in range(ns):
                s_scr[s, j] = s_scr[s, j] * dec_scr[j, 3] + env["ds"][s]

    def f_cross(q2, sb):
        return _from_seq_rows([_dot(_seq_rows(q2, s, ct), sb[s]) for s in range(ns)], ct)

    _emit_by_level([("cross", ("q2", "sb"), f_cross)], envs)
    for env in envs:
        o2 = env["inner"] + env["cross"] * dec_scr[env["j"], 1]
        o_scr[env["rsl"], env["lsl"]] = jnp.where(lane_lo, o2[0:ROWS], o2[ROWS:])

    o = o_scr[...]
    y = jax.nn.silu(gate) * _head_norm(o, gng_ref[...], gnb_ref[...], B_GN_EPS, ones_bd)
    yb_ref[...] = _bf(y)

    @pl.when(ti == pl.num_programs(1) - 1)
    def _():
        _store_state(sout_ref, s_scr, nb)


def _retention(pb3, s0, invf, gn_g, gn_b, nb, tt, pos0):
    bsz, tlen, _ = pb3.shape
    ct, rows = _check_mixer_blocking(bsz, tlen, nb, tt)
    grid = (bsz // nb, tlen // tt)
    state_spec = pl.BlockSpec((nb, HEADS, HEAD_DIM, HEAD_DIM), lambda b, t: (b, 0, 0, 0))
    in_specs = [
        pl.BlockSpec((nb, tt, B_PROJ), lambda b, t: (b, t, 0)),
        state_spec,
        _row_spec((1, LANES)), _row_spec((1, WIDTH)), _row_spec((1, WIDTH)),
    ]
    out_specs = [
        pl.BlockSpec((rows, WIDTH), lambda b, t: (b * (tlen // tt) + t, 0)),
        state_spec,
    ]
    scratch = [
        pltpu.VMEM((nb, PAIRS, LANES, LANES), F32),
        pltpu.VMEM((rows, WIDTH), BF16),
        pltpu.VMEM((rows, WIDTH), F32),
        pltpu.VMEM((rows, WIDTH), BF16),
        pltpu.VMEM((rows, WIDTH), F32),
        pltpu.VMEM((PAIRS, 4, N2, N2), F32),
        pltpu.VMEM((4, N2, N2), F32),
    ]
    return pl.pallas_call(
        functools.partial(_ret_kernel, nb, tt, ct, pos0),
        grid=grid,
        in_specs=in_specs,
        out_specs=out_specs,
        out_shape=[
            jax.ShapeDtypeStruct((bsz * tlen, WIDTH), BF16),
            jax.ShapeDtypeStruct(s0.shape, F32),
        ],
        scratch_shapes=scratch,
        compiler_params=pltpu.CompilerParams(
            dimension_semantics=("arbitrary", "arbitrary"), vmem_limit_bytes=VMEM_LIMIT),
        name="retention_mixer",
    )(pb3, s0, invf, gn_g, gn_b)


def _conv_kernel(nb, tt, x_ref, buf_ref, win_ref, cw_ref, z_ref, bout_ref, halo_scr):
    rows = nb * tt
    ti = pl.program_id(1)

    @pl.when(ti == 0)
    def _():
        halo_scr[...] = buf_ref[...]

    xb = _bf(x_ref[...].reshape(rows, D_MODEL))
    win = win_ref[...]
    bg = _dot(xb, win[:, 0:D_MODEL])
    u = _dot(xb, win[:, D_MODEL:2 * D_MODEL]) * _dot(xb, win[:, 2 * D_MODEL:])
    t3 = _iota2((nb, tt, D_MODEL), 1)
    h0 = halo_scr[:, 0:1, :]
    h1 = halo_scr[:, 1:2, :]
    u3 = u.reshape(nb, tt, D_MODEL)
    prev1 = jnp.where(t3 == 0, h1, pltpu.roll(u, 1, axis=0).reshape(nb, tt, D_MODEL))
    prev2 = jnp.where(t3 == 0, h0, jnp.where(t3 == 1, h1, pltpu.roll(u, 2, axis=0).reshape(nb, tt, D_MODEL)))
    cw = cw_ref[...]
    conv = prev2 * cw[0:1, :] + prev1 * cw[1:2, :] + u3 * cw[2:3, :]
    z_ref[...] = _bf(bg * conv.reshape(rows, D_MODEL))
    halo_scr[...] = u3[:, tt - (CONV_W - 1):, :]

    @pl.when(ti == pl.num_programs(1) - 1)
    def _():
        bout_ref[...] = halo_scr[...]


def _conv_mixer(x3, buf0, win, cw, nb, tt):
    bsz, tlen, _ = x3.shape
    rows = nb * tt
    assert bsz % nb == 0 and tlen % tt == 0 and tt >= CONV_W - 1
    grid = (bsz // nb, tlen // tt)
    buf_spec = pl.BlockSpec((nb, CONV_W - 1, D_MODEL), lambda b, t: (b, 0, 0))
    return pl.pallas_call(
        functools.partial(_conv_kernel, nb, tt),
        grid=grid,
        in_specs=[
            pl.BlockSpec((nb, tt, D_MODEL), lambda b, t: (b, t, 0)),
            buf_spec,
            pl.BlockSpec(win.shape, lambda b, t: (0, 0), pipeline_mode=pl.Buffered(1)),
            _row_spec((CONV_W, D_MODEL)),
        ],
        out_specs=[
            pl.BlockSpec((rows, D_MODEL), lambda b, t: (b * (tlen // tt) + t, 0)),
            buf_spec,
        ],
        out_shape=[
            jax.ShapeDtypeStruct((bsz * tlen, D_MODEL), BF16),
            jax.ShapeDtypeStruct(buf0.shape, F32),
        ],
        scratch_shapes=[pltpu.VMEM((nb, CONV_W - 1, D_MODEL), F32)],
        compiler_params=pltpu.CompilerParams(
            dimension_semantics=("arbitrary", "arbitrary"), vmem_limit_bytes=VMEM_LIMIT),
        name="conv_mixer",
    )(x3, buf0, win, cw)


FF_CHUNK = 1024


def _post_kernel(npieces, *refs):
    x_ref = refs[0]
    y_refs = refs[1:1 + npieces]
    wout_ref, g1_ref, b1_ref, wup_ref, wdown_ref, g2_ref, b2_ref, out_ref = refs[1 + npieces:]
    x = x_ref[...]
    y = None
    off = 0
    for y_ref in y_refs:
        n = y_ref.shape[1]
        term = _dot(y_ref[...], wout_ref[off:off + n, :])
        y = term if y is None else y + term
        off += n
    x1 = _layer_norm(ALPHA * x + y, g1_ref[...], b1_ref[...])
    x1b = _bf(x1)
    acc = None
    for f in range(D_FF // FF_CHUNK):
        h = jnp.maximum(_dot(x1b, wup_ref[:, f * FF_CHUNK:(f + 1) * FF_CHUNK]), 0.0)
        term = _dot(_bf(h * h), wdown_ref[f * FF_CHUNK:(f + 1) * FF_CHUNK, :])
        acc = term if acc is None else acc + term
    out_ref[...] = _layer_norm(ALPHA * x1 + acc, g2_ref[...], b2_ref[...])


def _post(x2d, ys, wout, g1, b1, wup, wdown, g2, b2, tm):
    m = x2d.shape[0]
    const = lambda i: (0, 0)
    resident = lambda a: pl.BlockSpec(a.shape, const, pipeline_mode=pl.Buffered(1))
    vec = pl.BlockSpec((1, D_MODEL), const)
    in_specs = [pl.BlockSpec((tm, D_MODEL), lambda i: (i, 0))]
    in_specs += [pl.BlockSpec((tm, y.shape[1]), lambda i: (i, 0)) for y in ys]
    in_specs += [resident(wout), vec, vec, resident(wup), resident(wdown), vec, vec]
    return pl.pallas_call(
        functools.partial(_post_kernel, len(ys)),
        grid=(m // tm,),
        in_specs=in_specs,
        out_specs=pl.BlockSpec((tm, D_MODEL), lambda i: (i, 0)),
        out_shape=jax.ShapeDtypeStruct((m, D_MODEL), F32),
        compiler_params=pltpu.CompilerParams(
            dimension_semantics=("arbitrary",), vmem_limit_bytes=VMEM_LIMIT),
        name="post_block",
    )(x2d, *ys, wout, g1, b1, wup, wdown, g2, b2)


def _trunk(x, st_shift, st_wkv, st_ret, st_conv, pos0, blk, w):
    bsz, tlen, _ = x.shape
    m = bsz * tlen
    x2d = x.reshape(m, D_MODEL)
    row = lambda a: a.reshape(1, -1)

    pa, pb = _inproj(x2d, w["w_a"], w["w_b"], blk["tm"])
    pa3 = pa.reshape(bsz, tlen, A_PROJ)
    rw_prm = (row(w["mu_a"]), row(w["w0"]), w["w2p"], row(w["a0"]), w["a2p"], w["g2"],
              row(w["k_k"]), row(w["k_a"]), row(w["r_k"]), row(w["lnx_g"]), row(w["lnx_b"]))
    ya, wkv1 = _rwkv(pa3, st_shift, st_wkv, rw_prm, blk["nb_mix"], blk["tt_mix"])
    yb, ret1 = _retention(pb.reshape(bsz, tlen, B_PROJ), st_ret, w["invf"], row(w["gn_g"]),
                          row(w["gn_b"]), blk["nb_mix"], blk["tt_mix"], pos0)
    shift1 = pa3[:, tlen - 1, :]
    x2d = _post(x2d, [ya, yb], w["w_out_ab"], row(w["ln1_g"][0]), row(w["ln1_b"][0]),
                w["w_up"][0], w["w_down"][0], row(w["ln2_g"][0]), row(w["ln2_b"][0]), blk["tm"])

    z, conv1 = _conv_mixer(x2d.reshape(bsz, tlen, D_MODEL), st_conv, w["w_in_conv"], w["conv_w"],
                           blk["nb_conv"], blk["tt_conv"])
    x2d = _post(x2d, [z], w["w_out_conv"], row(w["ln1_g"][1]), row(w["ln1_b"][1]),
                w["w_up"][1], w["w_down"][1], row(w["ln2_g"][1]), row(w["ln2_b"][1]), blk["tm"])
    return x2d.reshape(bsz, tlen, D_MODEL), shift1[None], wkv1[None], ret1[None], conv1[None]


def kernel(x_prompt, x_sample, state_shift, state_wkv, state_ret, state_conv, w_in_ab, mu_a, w0, w2, a0, a2,
           g2, k_k, k_a, r_k, lnx_g, lnx_b, gn_g, gn_b, w_out_ab, w_in_conv, conv_w, w_out_conv,
           ln1_g, ln1_b, ln2_g, ln2_b, w_up, w_down):
    bp, tp, _ = x_prompt.shape
    half = HEAD_DIM // 2
    inv = ROPE_BASE ** (-jnp.arange(half, dtype=F32) / half)
    zpad = jnp.zeros((LANES - DECAY_LORA, WIDTH), F32)
    w = {
        "w_a": _bf(w_in_ab[0][:, :A_PROJ]), "w_b": _bf(w_in_ab[0][:, A_PROJ:]),
        "mu_a": mu_a[0], "w0": w0[0], "a0": a0[0],
        "w2p": _bf(jnp.concatenate([w2[0], zpad], axis=0)),
        "a2p": _bf(jnp.concatenate([zpad, a2[0]], axis=0)),
        "g2": _bf(g2[0]), "k_k": k_k[0], "k_a": k_a[0], "r_k": r_k[0],
        "lnx_g": lnx_g[0], "lnx_b": lnx_b[0], "gn_g": gn_g[0], "gn_b": gn_b[0],
        "invf": jnp.tile(inv, LANES // half).reshape(1, LANES),
        "w_out_ab": _bf(w_out_ab[0]), "w_in_conv": _bf(w_in_conv[0]), "conv_w": conv_w[0],
        "w_out_conv": _bf(w_out_conv[0]),
        "ln1_g": ln1_g, "ln1_b": ln1_b, "ln2_g": ln2_g, "ln2_b": ln2_b,
        "w_up": _bf(w_up), "w_down": _bf(w_down),
    }
    dt = state_wkv.dtype
    z_shift = jnp.zeros((bp, A_PROJ), dt)
    z_state = jnp.zeros((bp, HEADS, HEAD_DIM, HEAD_DIM), dt)
    z_conv = jnp.zeros((bp, CONV_W - 1, D_MODEL), dt)
    blk_p = {"tm": 512, "nb_mix": 1, "tt_mix": 256, "nb_conv": 1, "tt_conv": 512}
    y_p, p_shift, p_wkv, p_ret, p_conv = _trunk(x_prompt, z_shift, z_state, z_state, z_conv, 0, blk_p, w)
    ts = x_sample.shape[1]
    blk_s = {"tm": 512, "nb_mix": ROWS // ts, "tt_mix": ts, "nb_conv": 256 // ts, "tt_conv": ts}
    y_s, s_shift, s_wkv, s_ret, s_conv = _trunk(x_sample, state_shift[0], state_wkv[0], state_ret[0],
                                                state_conv[0], PAST_LEN, blk_s, w)
    return (y_p, y_s, p_shift, p_wkv, p_ret, p_conv, s_shift, s_wkv, s_ret, s_conv)
```

```python
import functools

import jax
import jax.numpy as jnp
from jax import lax
from jax.experimental import pallas as pl
from jax.experimental.pallas import tpu as pltpu

F32 = jnp.float32
BF16 = jnp.bfloat16

D_MODEL = 1024
DEPTH = 2
PAST_LEN = 16384
HEADS = 8
HEAD_DIM = 64
WIDTH = HEADS * HEAD_DIM
DECAY_LORA = 64
AAA_LORA = 64
GATE_LORA = 128
A_PROJ = 3 * WIDTH + DECAY_LORA + AAA_LORA + GATE_LORA
B_PROJ = 4 * WIDTH
A_GN_EPS = 64e-5
B_GN_EPS = 1e-5
ROPE_BASE = 10000.0
CONV_W = 3
D_FF = 4 * D_MODEL
LN_EPS = 1e-5
ALPHA = (2.0 * DEPTH) ** 0.25

LANES = 128
SUBLANES = 8
PAIRS = WIDTH // LANES
ROWS = 64
N2 = 2 * ROWS
VMEM_LIMIT = 56 * 1024 * 1024


def _bf(x):
    return x.astype(BF16)


def _dot(a, b):
    return jnp.dot(a, b, preferred_element_type=F32)


def _dot_tb(a, b):
    return lax.dot_general(a, b, (((1,), (1,)), ((), ())), preferred_element_type=F32)


def _dot_split_lhs(m, x, parts):
    acc = None
    rem = x
    for i in range(parts):
        hi = _bf(rem)
        term = _dot(m, hi)
        acc = term if acc is None else acc + term
        if i + 1 < parts:
            rem = rem - hi.astype(F32)
    return acc


def _iota2(shape, dim):
    return lax.broadcasted_iota(jnp.int32, shape, dim)


def _div2(x, n):
    assert n & (n - 1) == 0
    return lax.shift_right_logical(x, n.bit_length() - 1)


def _mod2(x, n):
    assert n & (n - 1) == 0
    return lax.bitwise_and(x, n - 1)


def _head_ones():
    ri = _iota2((LANES, LANES), 0)
    ci = _iota2((LANES, LANES), 1)
    return jnp.where(_div2(ri, HEAD_DIM) == _div2(ci, HEAD_DIM), 1.0, 0.0).astype(BF16)


def _headsum(x, ones_bd):
    xb = _bf(x)
    return jnp.concatenate([_dot(xb[:, LANES * j:LANES * (j + 1)], ones_bd) for j in range(PAIRS)], axis=1)


def _head_norm(o, g, b, eps, ones_bd):
    mu = _headsum(o, ones_bd) * (1.0 / HEAD_DIM)
    d = o - mu
    var = _headsum(d * d, ones_bd) * (1.0 / HEAD_DIM)
    return d * lax.rsqrt(var + eps) * g + b


def _layer_norm(z, g, b):
    mu = jnp.mean(z, axis=-1, keepdims=True)
    d = z - mu
    var = jnp.mean(d * d, axis=-1, keepdims=True)
    return d * lax.rsqrt(var + LN_EPS) * g + b


def _bdiag(x, lane_lo):
    zero = jnp.zeros_like(x)
    return jnp.concatenate([jnp.where(lane_lo, x, zero), jnp.where(lane_lo, zero, x)], axis=0)


def _unstack_heads(x, lane_lo):
    return jnp.where(lane_lo, x[0:ROWS], x[ROWS:])


def _stream_masks(ct, width):
    ri = _iota2((N2, width), 0)
    ci = _mod2(_iota2((N2, width), 1), N2)
    same = _div2(ri, ct) == _div2(ci, ct)
    return same & (ci < ri), same & (ci <= ri), ri == ci, _div2(ri, ROWS) == _div2(ci, ROWS)


def _load_state(s_scr, s0_ref, nb):
    z = jnp.zeros((HEAD_DIM, HEAD_DIM), F32)
    for s in range(nb):
        for j in range(PAIRS):
            top = jnp.concatenate([s0_ref[s, 2 * j], z], axis=1)
            bot = jnp.concatenate([z, s0_ref[s, 2 * j + 1]], axis=1)
            s_scr[s, j] = jnp.concatenate([top, bot], axis=0)


def _store_state(sout_ref, s_scr, nb):
    for s in range(nb):
        for j in range(PAIRS):
            s2 = s_scr[s, j]
            sout_ref[s, 2 * j] = s2[0:HEAD_DIM, 0:HEAD_DIM]
            sout_ref[s, 2 * j + 1] = s2[HEAD_DIM:, HEAD_DIM:]


def _seq_rows(x, q, ct):
    if ct == ROWS:
        return x
    return jnp.concatenate([x[q * ct:(q + 1) * ct], x[ROWS + q * ct:ROWS + (q + 1) * ct]], axis=0)


def _from_seq_rows(pieces, ct):
    if ct == ROWS:
        return pieces[0]
    return jnp.concatenate([p[0:ct] for p in pieces] + [p[ct:2 * ct] for p in pieces], axis=0)


def _pad_rows(x, rows):
    if x.shape[0] == rows:
        return x
    return jnp.concatenate([x, jnp.zeros((rows - x.shape[0], x.shape[1]), x.dtype)], axis=0)


def _emit_by_level(steps, envs):
    level = {}
    for name, deps, _ in steps:
        level[name] = 1 + max([level.get(d, 0) for d in deps], default=0)
    for lv in sorted(set(level.values())):
        for name, deps, fn in steps:
            if level[name] == lv:
                for env in envs:
                    env[name] = fn(*[env[d] for d in deps])


def _inverse_steps(nfac, eye):
    def advance(last):
        def fn(q, p):
            pb = _bf(p)
            if last:
                return p + _dot(q, pb), None
            x = _dot(q, jnp.concatenate([pb, q], axis=1))
            return p + x[:, 0:N2], _bf(x[:, N2:])
        return fn

    steps = [
        ("q0", ("l",), _bf),
        ("p1", ("l",), lambda l: l + eye()),
        ("q1", ("q0",), lambda q: _bf(_dot(q, q))),
    ]
    for i in range(1, nfac):
        steps.append((f"s{i}", (f"q{i}", f"p{i}"), advance(i == nfac - 1)))
        steps.append((f"p{i + 1}", (f"s{i}",), lambda s: s[0]))
        steps.append((f"q{i + 1}", (f"s{i}",), lambda s: s[1]))
    steps.append(("tb", (f"p{nfac}",), _bf))
    return steps


def _inproj_kernel(x_ref, wa_ref, wb_ref, pa_ref, pb_ref):
    xb = _bf(x_ref[...])
    pa_ref[...] = _dot(xb, wa_ref[...])
    pb_ref[...] = _dot(xb, wb_ref[...])


def _inproj(x2d, wa, wb, tm):
    m = x2d.shape[0]
    const = lambda i: (0, 0)
    return pl.pallas_call(
        _inproj_kernel,
        grid=(m // tm,),
        in_specs=[
            pl.BlockSpec((tm, D_MODEL), lambda i: (i, 0)),
            pl.BlockSpec(wa.shape, const, pipeline_mode=pl.Buffered(1)),
            pl.BlockSpec(wb.shape, const, pipeline_mode=pl.Buffered(1)),
        ],
        out_specs=[
            pl.BlockSpec((tm, wa.shape[1]), lambda i: (i, 0)),
            pl.BlockSpec((tm, wb.shape[1]), lambda i: (i, 0)),
        ],
        out_shape=[
            jax.ShapeDtypeStruct((m, wa.shape[1]), F32),
            jax.ShapeDtypeStruct((m, wb.shape[1]), F32),
        ],
        compiler_params=pltpu.CompilerParams(
            dimension_semantics=("arbitrary",), vmem_limit_bytes=VMEM_LIMIT),
        name="inproj_ab",
    )(x2d, wa, wb)


def _rwkv_kernel(nb, tt, ct, pa_ref, shift_ref, s0_ref, mu_ref, w0_ref, w2p_ref, a0_ref, a2p_ref,
                 g2_ref, kk_ref, ka_ref, rk_ref, lng_ref, lnb_ref,
                 ya_ref, sout_ref,
                 s_scr, prev_scr, at_scr, rt_scr, bt_scr, kt_scr, v_scr, p_scr, o_scr,
                 scmask_scr, eye_scr, head_scr, lcum_scr):
    rows = nb * tt
    nchunk = rows // ROWS
    ns = ROWS // ct
    ti = pl.program_id(1)

    @pl.when((pl.program_id(0) == 0) & (ti == 0))
    def _():
        strict, incl, eye, same_head = _stream_masks(ct, 2 * N2)
        one = lambda m: jnp.where(m, 1.0, 0.0)
        scmask_scr[...] = jnp.concatenate([one(strict), one(incl)], axis=0)
        eye_scr[...] = one(eye)[:, 0:N2]
        head_scr[...] = one(same_head)[:, 0:N2]
        ri = _iota2((rows, rows), 0)
        ci = _iota2((rows, rows), 1)
        lcum_scr[...] = jnp.where((_div2(ri, ct) == _div2(ci, ct)) & (ci <= ri), 1.0, 0.0).astype(BF16)

    @pl.when(ti == 0)
    def _():
        _load_state(s_scr, s0_ref, nb)
        prev_scr[...] = shift_ref[...]

    ones_bd = _head_ones()

    p3 = pa_ref[...]
    p2 = p3.reshape(rows, A_PROJ)
    t3 = _iota2((nb, tt, A_PROJ), 1)
    rolled = pltpu.roll(p2, 1, axis=0).reshape(nb, tt, A_PROJ)
    pprev = jnp.where(t3 == 0, prev_scr[...], rolled).reshape(rows, A_PROJ)
    prev_scr[...] = p3[:, tt - 1:tt, :]
    m = p2 + (pprev - p2) * mu_ref[...]
    r = m[:, 0:WIDTH]
    k = m[:, WIDTH:2 * WIDTH]
    v = m[:, 2 * WIDTH:3 * WIDTH]
    wa = m[:, 3 * WIDTH:3 * WIDTH + LANES]
    gd = m[:, 3 * WIDTH + LANES:A_PROJ]
    z = -(w0_ref[...] + _dot(_bf(jnp.tanh(wa)), w2p_ref[...]))
    softplus = jnp.maximum(z, 0.0) + jnp.log(1.0 + jnp.exp(-jnp.abs(z)))
    lw = -jnp.exp(-softplus - 0.5)
    a = jax.nn.sigmoid(a0_ref[...] + _dot(_bf(wa), a2p_ref[...]))
    g = _dot(_bf(jax.nn.sigmoid(gd)), g2_ref[...])
    kk = k * kk_ref[...]
    k = k * (1.0 + (a - 1.0) * ka_ref[...])
    kk = kk * lax.rsqrt(jnp.maximum(_headsum(kk * kk, ones_bd), 1e-24))
    bonus = _headsum(r * k * rk_ref[...], ones_bd) * v

    cs = _dot_split_lhs(lcum_scr[...], lw, 2)
    pinv = jnp.exp(-cs)
    p_scr[...] = jnp.exp(cs)
    at_scr[...] = _bf(-kk * jnp.exp(cs - lw))
    rt_scr[...] = _bf(r * p_scr[...])
    bt_scr[...] = _bf(kk * a * pinv)
    kt_scr[...] = _bf(k * pinv)
    v_scr[...] = _bf(v)

    lane_lo = _iota2((ROWS, LANES), 1) < HEAD_DIM
    sc_mask = lambda: scmask_scr[...]
    eye = lambda: eye_scr[...]
    same_head = lambda: head_scr[...]
    nfac = ct.bit_length() - 1

    envs = []
    for c in range(nchunk):
        for j in range(PAIRS):
            rsl = slice(c * ROWS, (c + 1) * ROWS)
            lsl = slice(LANES * j, LANES * (j + 1))
            aj, rj, bj, kj, vj = (s[rsl, lsl] for s in (at_scr, rt_scr, bt_scr, kt_scr, v_scr))
            envs.append({
                "c": c, "j": j, "rsl": rsl, "lsl": lsl,
                "a2": _bdiag(aj, lane_lo), "r2": _bdiag(rj, lane_lo),
                "b2": _bdiag(bj, lane_lo), "k2": _bdiag(kj, lane_lo),
                "bk": jnp.concatenate([bj, bj, kj, kj], axis=0),
                "vst": jnp.concatenate([vj, vj], axis=0),
            })

    steps = [
        ("scm", ("a2", "r2", "bk"),
         lambda a2, r2, bk: _dot_tb(jnp.concatenate([a2, r2], axis=0), bk) * sc_mask()),
        ("l", ("scm",), lambda scm: scm[0:N2, 0:N2]),
        ("lkb", ("scm",), lambda scm: _bf(scm[:, N2:])),
        ("arb", ("scm",), lambda scm: _bf(scm[N2:, 0:N2])),
        ("lvark", ("lkb", "vst"), _dot),
    ] + _inverse_steps(nfac, eye)

    if ns == 1:
        def f_ro(arb, tal, r2, lvark):
            x = _dot(arb, _bf(tal))
            return _bf(r2.astype(F32) + x[:, 0:N2]), x[:, N2:] + lvark[N2:]

        def f_mn(tal, b2, k2, vst):
            mm = _dot(_bf(tal.T), b2)
            n2 = _dot(_bf(vst.astype(F32).T), k2)
            return _bf(mm[0:N2]), (mm[N2:] + n2) * same_head()

        steps += [
            ("tal", ("tb", "a2", "lvark"),
             lambda tb, a2, lvark: _dot(tb, jnp.concatenate([a2, _bf(lvark[0:N2])], axis=1))),
            ("ro", ("arb", "tal", "r2", "lvark"), f_ro),
            ("mn", ("tal", "b2", "k2", "vst"), f_mn),
        ]
        _emit_by_level(steps, envs)
        state = [s_scr[0, j] for j in range(PAIRS)]
        for c in range(nchunk):
            cenv = envs[c * PAIRS:(c + 1) * PAIRS]
            sb = [_bf(s) for s in state]
            for j, env in enumerate(cenv):
                rp, oc = env["ro"]
                o_scr[env["rsl"], env["lsl"]] = _unstack_heads(_dot_tb(rp, sb[j]) + oc, lane_lo)
            for j, env in enumerate(cenv):
                mk, nn = env["mn"]
                pc = p_scr[(c + 1) * ROWS - 1:(c + 1) * ROWS, env["lsl"]]
                state[j] = (state[j] + _dot(sb[j], mk) + nn) * pc
        for j in range(PAIRS):
            s_scr[0, j] = state[j]
    else:
        for env in envs:
            env["sb"] = [_bf(s_scr[q, env["j"]]) for q in range(ns)]

        def f_h(a2, r2, sb):
            pieces = [_dot_tb(jnp.concatenate([_seq_rows(a2, q, ct), _seq_rows(r2, q, ct)], axis=0), sb[q])
                      for q in range(ns)]
            return (_from_seq_rows([x[0:2 * ct] for x in pieces], ct),
                    _from_seq_rows([x[2 * ct:] for x in pieces], ct))

        def f_ds(u2, vst, b2, k2):
            vst32 = vst.astype(F32)
            out = []
            for q in range(ns):
                uv = jnp.concatenate([_seq_rows(u2, q, ct), _seq_rows(vst32, q, ct)], axis=0)
                bk = jnp.concatenate([_seq_rows(b2, q, ct), _seq_rows(k2, q, ct)], axis=0)
                out.append(_dot(_bf(_pad_rows(uv, LANES).T), _pad_rows(bk, LANES)) * same_head())
            return out

        steps += [
            ("h", ("a2", "r2", "sb"), f_h),
            ("u2", ("tb", "h", "lvark"), lambda tb, h, lvark: _dot(tb, _bf(h[0] + lvark[0:N2]))),
            ("o2", ("arb", "u2", "h", "lvark"),
             lambda arb, u2, h, lvark: h[1] + _dot(arb, _bf(u2)) + lvark[N2:]),
            ("ds", ("u2", "vst", "b2", "k2"), f_ds),
        ]
        _emit_by_level(steps, envs)
        for env in envs:
            o_scr[env["rsl"], env["lsl"]] = _unstack_heads(env["o2"], lane_lo)
            for q in range(ns):
                pc = p_scr[(q + 1) * ct - 1:(q + 1) * ct, env["lsl"]]
                s_scr[q, env["j"]] = (s_scr[q, env["j"]] + env["ds"][q]) * pc

    o = o_scr[...]
    y = (_head_norm(o, lng_ref[...], lnb_ref[...], A_GN_EPS, ones_bd) + bonus) * g
    ya_ref[...] = _bf(y)

    @pl.when(ti == pl.num_programs(1) - 1)
    def _():
        _store_state(sout_ref, s_scr, nb)


def _row_spec(shape):
    return pl.BlockSpec(shape, lambda b, t: (0,) * len(shape))


def _check_mixer_blocking(bsz, tlen, nb, tt):
    ct = min(ROWS, tt)
    rows = nb * tt
    assert rows % ROWS == 0 and ROWS % ct == 0 and bsz % nb == 0 and tlen % tt == 0
    assert nb == 1 or (tt == tlen and rows == ROWS)
    return ct, rows


def _rwkv(pa3, shift0, s0, prm, nb, tt):
    bsz, tlen, _ = pa3.shape
    ct, rows = _check_mixer_blocking(bsz, tlen, nb, tt)
    grid = (bsz // nb, tlen // tt)
    state_spec = pl.BlockSpec((nb, HEADS, HEAD_DIM, HEAD_DIM), lambda b, t: (b, 0, 0, 0))
    vec = lambda n: _row_spec((1, n))
    in_specs = [
        pl.BlockSpec((nb, tt, A_PROJ), lambda b, t: (b, t, 0)),
        pl.BlockSpec((nb, 1, A_PROJ), lambda b, t: (b, 0, 0)),
        state_spec,
        vec(A_PROJ), vec(WIDTH), _row_spec((LANES, WIDTH)), vec(WIDTH), _row_spec((LANES, WIDTH)),
        _row_spec((GATE_LORA, WIDTH)), vec(WIDTH), vec(WIDTH), vec(WIDTH), vec(WIDTH), vec(WIDTH),
    ]
    out_specs = [
        pl.BlockSpec((rows, WIDTH), lambda b, t: (b * (tlen // tt) + t, 0)),
        state_spec,
    ]
    scratch = [
        pltpu.VMEM((nb, PAIRS, LANES, LANES), F32),
        pltpu.VMEM((nb, 1, A_PROJ), F32),
        pltpu.VMEM((rows, WIDTH), BF16),
        pltpu.VMEM((rows, WIDTH), BF16),
        pltpu.VMEM((rows, WIDTH), BF16),
        pltpu.VMEM((rows, WIDTH), BF16),
        pltpu.VMEM((rows, WIDTH), BF16),
        pltpu.VMEM((rows, WIDTH), F32),
        pltpu.VMEM((rows, WIDTH), F32),
        pltpu.VMEM((2 * N2, 2 * N2), F32),
        pltpu.VMEM((N2, N2), F32),
        pltpu.VMEM((N2, N2), F32),
        pltpu.VMEM((rows, rows), BF16),
    ]
    return pl.pallas_call(
        functools.partial(_rwkv_kernel, nb, tt, ct),
        grid=grid,
        in_specs=in_specs,
        out_specs=out_specs,
        out_shape=[
            jax.ShapeDtypeStruct((bsz * tlen, WIDTH), BF16),
            jax.ShapeDtypeStruct(s0.shape, F32),
        ],
        scratch_shapes=scratch,
        compiler_params=pltpu.CompilerParams(
            dimension_semantics=("arbitrary", "arbitrary"), vmem_limit_bytes=VMEM_LIMIT),
        name="rwkv7_mixer",
    )(pa3, shift0.reshape(bsz, 1, A_PROJ), s0, *prm)


def _ret_kernel(nb, tt, ct, pos0, pb_ref, s0_ref, invf_ref, gng_ref, gnb_ref,
                yb_ref, sout_ref,
                s_scr, q_scr, k_scr, v_scr, o_scr, dec_scr, head_scr, trig_scr):
    rows = nb * tt
    nchunk = rows // ROWS
    ns = ROWS // ct
    ti = pl.program_id(1)

    @pl.when((pl.program_id(0) == 0) & (ti == 0))
    def _():
        _, incl, _, same_head = _stream_masks(ct, N2)
        head_scr[...] = jnp.where(same_head, 1.0, 0.0)
        ri = _iota2((N2, N2), 0)
        ci = _iota2((N2, N2), 1)
        tpos = _mod2(ri, ct).astype(F32)
        tdiff = (_mod2(ri, ct) - _mod2(ci, ct)).astype(F32)
        for j in range(PAIRS):
            head = (2 * j + _div2(ri, ROWS)).astype(F32)
            lg = jnp.log1p(-jnp.exp2(-5.0 - head))
            dec_scr[j, 0] = jnp.where(incl, jnp.exp(lg * jnp.maximum(tdiff, 0.0)), 0.0)
            dec_scr[j, 1] = jnp.exp(lg * (tpos + 1.0))
            dec_scr[j, 2] = jnp.exp(lg * (ct - 1.0 - tpos))
            dec_scr[j, 3] = jnp.exp(lg * ct)
        off = _mod2(_iota2((rows, LANES), 0), tt).astype(F32) * invf_ref[...]
        trig_scr[0] = jnp.cos(off)
        trig_scr[1] = jnp.sin(off)

    @pl.when(ti == 0)
    def _():
        _load_state(s_scr, s0_ref, nb)

    ones_bd = _head_ones()
    pb = pb_ref[...].reshape(rows, B_PROJ)
    q = pb[:, 0:WIDTH]
    k = pb[:, WIDTH:2 * WIDTH]
    gate = pb[:, 3 * WIDTH:]

    base = jnp.zeros((SUBLANES, LANES), F32) + (pos0 + ti * tt).astype(F32)
    ang = base * invf_ref[...]
    cos_a, sin_a = jnp.cos(ang)[0:1], jnp.sin(ang)[0:1]
    cos_b, sin_b = trig_scr[0], trig_scr[1]
    cos = jnp.concatenate([cos_a * cos_b - sin_a * sin_b] * PAIRS, axis=1)
    sin = jnp.concatenate([sin_a * cos_b + cos_a * sin_b] * PAIRS, axis=1)
    first_half = _mod2(_iota2((rows, WIDTH), 1), HEAD_DIM) < (HEAD_DIM // 2)
    sin = jnp.where(first_half, -sin, sin)

    def rope(x):
        partner = jnp.where(first_half, pltpu.roll(x, WIDTH - HEAD_DIM // 2, axis=1),
                            pltpu.roll(x, HEAD_DIM // 2, axis=1))
        return x * cos + partner * sin

    q_scr[...] = _bf(rope(q))
    k_scr[...] = rope(k) * (HEAD_DIM ** -0.5)
    v_scr[...] = _bf(pb[:, 2 * WIDTH:3 * WIDTH])

    lane_lo = _iota2((ROWS, LANES), 1) < HEAD_DIM
    same_head = lambda: head_scr[...]

    envs = []
    for c in range(nchunk):
        for j in range(PAIRS):
            rsl = slice(c * ROWS, (c + 1) * ROWS)
            lsl = slice(LANES * j, LANES * (j + 1))
            qj, kj, vj = q_scr[rsl, lsl], k_scr[rsl, lsl], v_scr[rsl, lsl]
            kjb = _bf(kj)
            envs.append({
                "c": c, "j": j, "rsl": rsl, "lsl": lsl,
                "q2": _bdiag(qj, lane_lo),
                "kst": jnp.concatenate([kjb, kjb], axis=0),
                "k2d": _bdiag(kj, lane_lo) * dec_scr[j, 2],
                "vst": jnp.concatenate([vj, vj], axis=0),
                "dmask": lambda j=j: dec_scr[j, 0],
            })

    def f_ds(k2d, vst):
        return [_dot(_bf(_pad_rows(_seq_rows(k2d, s, ct), LANES).T), _pad_rows(_seq_rows(vst, s, ct), LANES))
                * same_head() for s in range(ns)]

    steps = [
        ("sc", ("q2", "kst", "dmask"), lambda q2, kst, dmask: _bf(_dot_tb(q2, kst) * dmask())),
        ("inner", ("sc", "vst"), _dot),
        ("ds", ("k2d", "vst"), f_ds),
    ]
    _emit_by_level(steps, envs)

    if ns == 1:
        state = [s_scr[0, j] for j in range(PAIRS)]
        for c in range(nchunk):
            for j in range(PAIRS):
                env = envs[c * PAIRS + j]
                env["sb"] = [_bf(state[j])]
                state[j] = state[j] * dec_scr[j, 3] + env["ds"][0]
        for j in range(PAIRS):
            s_scr[0, j] = state[j]
    else:
        for env in envs:
            j = env["j"]
            env["sb"] = [_bf(s_scr[s, j]) for s in range(ns)]
            for s in range(ns):
                s_scr[s, j] = s_scr[s, j] * dec_scr[j, 3] + env["ds"][s]

    def f_cross(q2, sb):
        return _from_seq_rows([_dot(_seq_rows(q2, s, ct), sb[s]) for s in range(ns)], ct)

    _emit_by_level([("cross", ("q2", "sb"), f_cross)], envs)
    for env in envs:
        o2 = env["inner"] + env["cross"] * dec_scr[env["j"], 1]
        o_scr[env["rsl"], env["lsl"]] = _unstack_heads(o2, lane_lo)

    o = o_scr[...]
    y = jax.nn.silu(gate) * _head_norm(o, gng_ref[...], gnb_ref[...], B_GN_EPS, ones_bd)
    yb_ref[...] = _bf(y)

    @pl.when(ti == pl.num_programs(1) - 1)
    def _():
        _store_state(sout_ref, s_scr, nb)


def _retention(pb3, s0, invf, gn_g, gn_b, nb, tt, pos0):
    bsz, tlen, _ = pb3.shape
    ct, rows = _check_mixer_blocking(bsz, tlen, nb, tt)
    grid = (bsz // nb, tlen // tt)
    state_spec = pl.BlockSpec((nb, HEADS, HEAD_DIM, HEAD_DIM), lambda b, t: (b, 0, 0, 0))
    in_specs = [
        pl.BlockSpec((nb, tt, B_PROJ), lambda b, t: (b, t, 0)),
        state_spec,
        _row_spec((1, LANES)), _row_spec((1, WIDTH)), _row_spec((1, WIDTH)),
    ]
    out_specs = [
        pl.BlockSpec((rows, WIDTH), lambda b, t: (b * (tlen // tt) + t, 0)),
        state_spec,
    ]
    scratch = [
        pltpu.VMEM((nb, PAIRS, LANES, LANES), F32),
        pltpu.VMEM((rows, WIDTH), BF16),
        pltpu.VMEM((rows, WIDTH), F32),
        pltpu.VMEM((rows, WIDTH), BF16),
        pltpu.VMEM((rows, WIDTH), F32),
        pltpu.VMEM((PAIRS, 4, N2, N2), F32),
        pltpu.VMEM((N2, N2), F32),
        pltpu.VMEM((2, rows, LANES), F32),
    ]
    return pl.pallas_call(
        functools.partial(_ret_kernel, nb, tt, ct, pos0),
        grid=grid,
        in_specs=in_specs,
        out_specs=out_specs,
        out_shape=[
            jax.ShapeDtypeStruct((bsz * tlen, WIDTH), BF16),
            jax.ShapeDtypeStruct(s0.shape, F32),
        ],
        scratch_shapes=scratch,
        compiler_params=pltpu.CompilerParams(
            dimension_semantics=("arbitrary", "arbitrary"), vmem_limit_bytes=VMEM_LIMIT),
        name="retention_mixer",
    )(pb3, s0, invf, gn_g, gn_b)


def _conv_kernel(nb, tt, nsplit, x_ref, buf_ref, win_ref, cw_ref, z_ref, bout_ref, halo_scr):
    rows = nb * tt
    sub = rows // nsplit
    ti = pl.program_id(1)

    @pl.when(ti == 0)
    def _():
        halo_scr[...] = buf_ref[...]

    proj = []
    for i in range(nsplit):
        xi = x_ref[0, i * sub:(i + 1) * sub, :] if nb == 1 else x_ref[...].reshape(rows, D_MODEL)
        xb = _bf(xi)
        bg = _dot(xb, win_ref[:, 0:D_MODEL])
        u = _dot(xb, win_ref[:, D_MODEL:2 * D_MODEL]) * _dot(xb, win_ref[:, 2 * D_MODEL:])
        proj.append((bg, u))

    cw = cw_ref[...]
    st = tt // nsplit if nb == 1 else tt
    t3 = _iota2((nb, st, D_MODEL), 1)
    h0 = halo_scr[:, 0:1, :]
    h1 = halo_scr[:, 1:2, :]
    for i, (bg, u) in enumerate(proj):
        u3 = u.reshape(nb, st, D_MODEL)
        prev1 = jnp.where(t3 == 0, h1, pltpu.roll(u, 1, axis=0).reshape(nb, st, D_MODEL))
        prev2 = jnp.where(t3 == 0, h0,
                          jnp.where(t3 == 1, h1, pltpu.roll(u, 2, axis=0).reshape(nb, st, D_MODEL)))
        conv = prev2 * cw[0:1, :] + prev1 * cw[1:2, :] + u3 * cw[2:3, :]
        z_ref[i * sub:(i + 1) * sub, :] = _bf(bg * conv.reshape(sub, D_MODEL))
        h0 = u3[:, st - 2:st - 1, :]
        h1 = u3[:, st - 1:st, :]
    halo_scr[:, 0:1, :] = h0
    halo_scr[:, 1:2, :] = h1

    @pl.when(ti == pl.num_programs(1) - 1)
    def _():
        bout_ref[...] = halo_scr[...]


def _conv_mixer(x3, buf0, win, cw, nb, tt, nsplit):
    bsz, tlen, _ = x3.shape
    rows = nb * tt
    assert bsz % nb == 0 and tlen % tt == 0
    assert nsplit == 1 or nb == 1
    assert tt % nsplit == 0 and tt // nsplit >= CONV_W - 1
    grid = (bsz // nb, tlen // tt)
    buf_spec = pl.BlockSpec((nb, CONV_W - 1, D_MODEL), lambda b, t: (b, 0, 0))
    return pl.pallas_call(
        functools.partial(_conv_kernel, nb, tt, nsplit),
        grid=grid,
        in_specs=[
            pl.BlockSpec((nb, tt, D_MODEL), lambda b, t: (b, t, 0)),
            buf_spec,
            pl.BlockSpec(win.shape, lambda b, t: (0, 0), pipeline_mode=pl.Buffered(1)),
            _row_spec((CONV_W, D_MODEL)),
        ],
        out_specs=[
            pl.BlockSpec((rows, D_MODEL), lambda b, t: (b * (tlen // tt) + t, 0)),
            buf_spec,
        ],
        out_shape=[
            jax.ShapeDtypeStruct((bsz * tlen, D_MODEL), BF16),
            jax.ShapeDtypeStruct(buf0.shape, F32),
        ],
        scratch_shapes=[pltpu.VMEM((nb, CONV_W - 1, D_MODEL), F32)],
        compiler_params=pltpu.CompilerParams(
            dimension_semantics=("arbitrary", "arbitrary"), vmem_limit_bytes=VMEM_LIMIT),
        name="conv_mixer",
    )(x3, buf0, win, cw)


FF_CHUNK = 1024


def _post_kernel(npieces, nsplit, *refs):
    x_ref = refs[0]
    y_refs = refs[1:1 + npieces]
    wout_ref, g1_ref, b1_ref, wup_ref, wdown_ref, g2_ref, b2_ref, out_ref = refs[1 + npieces:]
    sub = x_ref.shape[0] // nsplit

    def f_y(r):
        y = None
        off = 0
        for y_ref in y_refs:
            n = y_ref.shape[1]
            term = _dot(y_ref[r, :], wout_ref[off:off + n, :])
            y = term if y is None else y + term
            off += n
        return y

    def f_mlp(f):
        def step(x1b, acc):
            h = jnp.maximum(_dot(x1b, wup_ref[:, f * FF_CHUNK:(f + 1) * FF_CHUNK]), 0.0)
            term = _dot(_bf(h * h), wdown_ref[f * FF_CHUNK:(f + 1) * FF_CHUNK, :])
            return term if acc is None else acc + term
        return step

    def f_out(r, x1, acc):
        out_ref[r, :] = _layer_norm(ALPHA * x1 + acc, g2_ref[...], b2_ref[...])
        return None

    steps = [
        ("y", ("r",), f_y),
        ("x1", ("r", "y"), lambda r, y: _layer_norm(ALPHA * x_ref[r, :] + y, g1_ref[...], b1_ref[...])),
        ("x1b", ("x1",), _bf),
        ("acc-1", ("x1b",), lambda x1b: None),
    ]
    nff = D_FF // FF_CHUNK
    for f in range(nff):
        steps.append((f"acc{f}", ("x1b", f"acc{f - 1}"), f_mlp(f)))
    steps.append(("out", ("r", "x1", f"acc{nff - 1}"), f_out))
    envs = [{"r": slice(i * sub, (i + 1) * sub)} for i in range(nsplit)]
    _emit_by_level(steps, envs)


def _post(x2d, ys, wout, g1, b1, wup, wdown, g2, b2, tm, nsplit):
    m = x2d.shape[0]
    assert m % tm == 0 and tm % nsplit == 0
    const = lambda i: (0, 0)
    resident = lambda a: pl.BlockSpec(a.shape, const, pipeline_mode=pl.Buffered(1))
    vec = pl.BlockSpec((1, D_MODEL), const)
    in_specs = [pl.BlockSpec((tm, D_MODEL), lambda i: (i, 0))]
    in_specs += [pl.BlockSpec((tm, y.shape[1]), lambda i: (i, 0)) for y in ys]
    in_specs += [resident(wout), vec, vec, resident(wup), resident(wdown), vec, vec]
    return pl.pallas_call(
        functools.partial(_post_kernel, len(ys), nsplit),
        grid=(m // tm,),
        in_specs=in_specs,
        out_specs=pl.BlockSpec((tm, D_MODEL), lambda i: (i, 0)),
        out_shape=jax.ShapeDtypeStruct((m, D_MODEL), F32),
        compiler_params=pltpu.CompilerParams(
            dimension_semantics=("arbitrary",), vmem_limit_bytes=VMEM_LIMIT),
        name="post_block",
    )(x2d, *ys, wout, g1, b1, wup, wdown, g2, b2)


def _trunk(x, st_shift, st_wkv, st_ret, st_conv, pos0, blk, w):
    bsz, tlen, _ = x.shape
    m = bsz * tlen
    x2d = x.reshape(m, D_MODEL)
    row = lambda a: a.reshape(1, -1)

    pa, pb = _inproj(x2d, w["w_a"], w["w_b"], blk["tm_in"])
    pa3 = pa.reshape(bsz, tlen, A_PROJ)
    rw_prm = (row(w["mu_a"]), row(w["w0"]), w["w2p"], row(w["a0"]), w["a2p"], w["g2"],
              row(w["k_k"]), row(w["k_a"]), row(w["r_k"]), row(w["lnx_g"]), row(w["lnx_b"]))
    ya, wkv1 = _rwkv(pa3, st_shift, st_wkv, rw_prm, blk["nb_mix"], blk["tt_mix"])
    yb, ret1 = _retention(pb.reshape(bsz, tlen, B_PROJ), st_ret, w["invf"], row(w["gn_g"]),
                          row(w["gn_b"]), blk["nb_mix"], blk["tt_mix"], pos0)
    shift1 = pa3[:, tlen - 1, :]
    x2d = _post(x2d, [ya, yb], w["w_out_ab"], row(w["ln1_g"][0]), row(w["ln1_b"][0]),
                w["w_up"][0], w["w_down"][0], row(w["ln2_g"][0]), row(w["ln2_b"][0]),
                blk["tm_post"], blk["split"])

    z, conv1 = _conv_mixer(x2d.reshape(bsz, tlen, D_MODEL), st_conv, w["w_in_conv"], w["conv_w"],
                           blk["nb_conv"], blk["tt_conv"], blk["split_conv"])
    x2d = _post(x2d, [z], w["w_out_conv"], row(w["ln1_g"][1]), row(w["ln1_b"][1]),
                w["w_up"][1], w["w_down"][1], row(w["ln2_g"][1]), row(w["ln2_b"][1]),
                blk["tm_post"], blk["split"])
    return x2d.reshape(bsz, tlen, D_MODEL), shift1[None], wkv1[None], ret1[None], conv1[None]


def kernel(x_prompt, x_sample, state_shift, state_wkv, state_ret, state_conv, w_in_ab, mu_a, w0, w2, a0, a2,
           g2, k_k, k_a, r_k, lnx_g, lnx_b, gn_g, gn_b, w_out_ab, w_in_conv, conv_w, w_out_conv,
           ln1_g, ln1_b, ln2_g, ln2_b, w_up, w_down):
    bp, tp, _ = x_prompt.shape
    half = HEAD_DIM // 2
    inv = ROPE_BASE ** (-jnp.arange(half, dtype=F32) / half)
    zpad = jnp.zeros((LANES - DECAY_LORA, WIDTH), F32)
    w = {
        "w_a": _bf(w_in_ab[0][:, :A_PROJ]), "w_b": _bf(w_in_ab[0][:, A_PROJ:]),
        "mu_a": mu_a[0], "w0": w0[0], "a0": a0[0],
        "w2p": _bf(jnp.concatenate([w2[0], zpad], axis=0)),
        "a2p": _bf(jnp.concatenate([zpad, a2[0]], axis=0)),
        "g2": _bf(g2[0]), "k_k": k_k[0], "k_a": k_a[0], "r_k": r_k[0],
        "lnx_g": lnx_g[0], "lnx_b": lnx_b[0], "gn_g": gn_g[0], "gn_b": gn_b[0],
        "invf": jnp.tile(inv, LANES // half).reshape(1, LANES),
        "w_out_ab": _bf(w_out_ab[0]), "w_in_conv": _bf(w_in_conv[0]), "conv_w": conv_w[0],
        "w_out_conv": _bf(w_out_conv[0]),
        "ln1_g": ln1_g, "ln1_b": ln1_b, "ln2_g": ln2_g, "ln2_b": ln2_b,
        "w_up": _bf(w_up), "w_down": _bf(w_down),
    }
    dt = state_wkv.dtype
    z_shift = jnp.zeros((bp, A_PROJ), dt)
    z_state = jnp.zeros((bp, HEADS, HEAD_DIM, HEAD_DIM), dt)
    z_conv = jnp.zeros((bp, CONV_W - 1, D_MODEL), dt)
    blk_p = {"tm_in": 512, "tm_post": 1024, "split": 2, "nb_mix": 1, "tt_mix": 256,
             "nb_conv": 1, "tt_conv": 1024, "split_conv": 2}
    y_p, p_shift, p_wkv, p_ret, p_conv = _trunk(x_prompt, z_shift, z_state, z_state, z_conv, 0, blk_p, w)
    ts = x_sample.shape[1]
    blk_s = {"tm_in": 512, "tm_post": 1024, "split": 2, "nb_mix": ROWS // ts, "tt_mix": ts,
             "nb_conv": 256 // ts, "tt_conv": ts, "split_conv": 1}
    y_s, s_shift, s_wkv, s_ret, s_conv = _trunk(x_sample, state_shift[0], state_wkv[0], state_ret[0],
                                                state_conv[0], PAST_LEN, blk_s, w)
    return (y_p, y_s, p_shift, p_wkv, p_ret, p_conv, s_shift, s_wkv, s_ret, s_conv)
```

```python
import functools

import jax
import jax.numpy as jnp
from jax import lax
from jax.experimental import pallas as pl
from jax.experimental.pallas import tpu as pltpu

F32 = jnp.float32
BF16 = jnp.bfloat16

D_MODEL = 1024
DEPTH = 2
PAST_LEN = 16384
HEADS = 8
HEAD_DIM = 64
WIDTH = HEADS * HEAD_DIM
DECAY_LORA = 64
AAA_LORA = 64
GATE_LORA = 128
A_PROJ = 3 * WIDTH + DECAY_LORA + AAA_LORA + GATE_LORA
B_PROJ = 4 * WIDTH
A_GN_EPS = 64e-5
B_GN_EPS = 1e-5
ROPE_BASE = 10000.0
CONV_W = 3
D_FF = 4 * D_MODEL
LN_EPS = 1e-5
ALPHA = (2.0 * DEPTH) ** 0.25

LANES = 128
SUBLANES = 8
PAIRS = WIDTH // LANES
ROWS = 64
N2 = 2 * ROWS
VMEM_LIMIT = 56 * 1024 * 1024


def _bf(x):
    return x.astype(BF16)


def _dot(a, b):
    return jnp.dot(a, b, preferred_element_type=F32)


def _dot_tb(a, b):
    return lax.dot_general(a, b, (((1,), (1,)), ((), ())), preferred_element_type=F32)


def _dot_ta(a, b):
    return lax.dot_general(a, b, (((0,), (0,)), ((), ())), preferred_element_type=F32)


def _dot_split_lhs(m, x, parts):
    acc = None
    rem = x
    for i in range(parts):
        hi = _bf(rem)
        term = _dot(m, hi)
        acc = term if acc is None else acc + term
        if i + 1 < parts:
            rem = rem - hi.astype(F32)
    return acc


def _iota2(shape, dim):
    return lax.broadcasted_iota(jnp.int32, shape, dim)


def _div2(x, n):
    assert n & (n - 1) == 0
    return lax.shift_right_logical(x, n.bit_length() - 1)


def _mod2(x, n):
    assert n & (n - 1) == 0
    return lax.bitwise_and(x, n - 1)


def _head_ones():
    ri = _iota2((LANES, LANES), 0)
    ci = _iota2((LANES, LANES), 1)
    return jnp.where(_div2(ri, HEAD_DIM) == _div2(ci, HEAD_DIM), 1.0, 0.0).astype(BF16)


def _headsum(x, ones_bd):
    xb = _bf(x)
    return jnp.concatenate([_dot(xb[:, LANES * j:LANES * (j + 1)], ones_bd) for j in range(PAIRS)], axis=1)


def _head_norm(o, g, b, eps, ones_bd):
    mu = _headsum(o, ones_bd) * (1.0 / HEAD_DIM)
    d = o - mu
    var = _headsum(d * d, ones_bd) * (1.0 / HEAD_DIM)
    return d * lax.rsqrt(var + eps) * g + b


def _layer_norm(z, g, b):
    mu = jnp.mean(z, axis=-1, keepdims=True)
    d = z - mu
    var = jnp.mean(d * d, axis=-1, keepdims=True)
    return d * lax.rsqrt(var + LN_EPS) * g + b


def _bdiag(x, lane_lo):
    zero = jnp.zeros_like(x)
    return jnp.concatenate([jnp.where(lane_lo, x, zero), jnp.where(lane_lo, zero, x)], axis=0)


def _unstack_heads(x, lane_lo):
    return jnp.where(lane_lo, x[0:ROWS], x[ROWS:])


def _stream_masks(ct, width):
    ri = _iota2((N2, width), 0)
    ci = _mod2(_iota2((N2, width), 1), N2)
    same = _div2(ri, ct) == _div2(ci, ct)
    return same & (ci < ri), same & (ci <= ri), ri == ci, _div2(ri, ROWS) == _div2(ci, ROWS)


def _load_state(s_scr, s0_ref, nb):
    z = jnp.zeros((HEAD_DIM, HEAD_DIM), F32)
    for s in range(nb):
        for j in range(PAIRS):
            top = jnp.concatenate([s0_ref[s, 2 * j], z], axis=1)
            bot = jnp.concatenate([z, s0_ref[s, 2 * j + 1]], axis=1)
            s_scr[s, j] = jnp.concatenate([top, bot], axis=0)


def _store_state(sout_ref, s_scr, nb):
    for s in range(nb):
        for j in range(PAIRS):
            s2 = s_scr[s, j]
            sout_ref[s, 2 * j] = s2[0:HEAD_DIM, 0:HEAD_DIM]
            sout_ref[s, 2 * j + 1] = s2[HEAD_DIM:, HEAD_DIM:]


def _seq_rows(x, q, ct):
    if ct == ROWS:
        return x
    return jnp.concatenate([x[q * ct:(q + 1) * ct], x[ROWS + q * ct:ROWS + (q + 1) * ct]], axis=0)


def _from_seq_rows(pieces, ct):
    if ct == ROWS:
        return pieces[0]
    return jnp.concatenate([p[0:ct] for p in pieces] + [p[ct:2 * ct] for p in pieces], axis=0)


def _pad_rows(x, rows):
    if x.shape[0] == rows:
        return x
    return jnp.concatenate([x, jnp.zeros((rows - x.shape[0], x.shape[1]), x.dtype)], axis=0)


def _emit_by_level(steps, envs):
    level = {}
    for name, deps, _ in steps:
        level[name] = 1 + max([level.get(d, 0) for d in deps], default=0)
    for lv in sorted(set(level.values())):
        for name, deps, fn in steps:
            if level[name] == lv:
                for env in envs:
                    env[name] = fn(*[env[d] for d in deps])


def _inverse_steps(nfac, eye):
    def advance(last):
        def fn(q, p):
            pb = _bf(p)
            if last:
                return p + _dot(q, pb), None
            x = _dot(q, jnp.concatenate([pb, q], axis=1))
            return p + x[:, 0:N2], _bf(x[:, N2:])
        return fn

    steps = [
        ("q0", ("l",), _bf),
        ("p1", ("l",), lambda l: l + eye()),
        ("q1", ("q0",), lambda q: _bf(_dot(q, q))),
    ]
    for i in range(1, nfac):
        steps.append((f"s{i}", (f"q{i}", f"p{i}"), advance(i == nfac - 1)))
        steps.append((f"p{i + 1}", (f"s{i}",), lambda s: s[0]))
        steps.append((f"q{i + 1}", (f"s{i}",), lambda s: s[1]))
    steps.append(("tb", (f"p{nfac}",), _bf))
    return steps


def _rwkv_score_steps(sc_mask):
    return [
        ("scm", ("a2", "r2", "bk"),
         lambda a2, r2, bk: _dot_tb(jnp.concatenate([a2, r2], axis=0), bk) * sc_mask()),
        ("l", ("scm",), lambda scm: scm[0:N2, 0:N2]),
        ("lkb", ("scm",), lambda scm: _bf(scm[:, N2:])),
        ("arb", ("scm",), lambda scm: _bf(scm[N2:, 0:N2])),
        ("lvark", ("lkb", "vst"), _dot),
    ]


def _rwkv_carried_steps(same_head):
    def f_ro(arb, tal, r2, lvark):
        x = _dot(arb, _bf(tal))
        return _bf(r2.astype(F32) + x[:, 0:N2]), x[:, N2:] + lvark[N2:]

    def f_mn(tal, b2, k2, vst):
        mm = _dot_ta(_bf(tal), b2)
        n2 = _dot_ta(vst, k2)
        return _bf(mm[0:N2]), (mm[N2:] + n2) * same_head()

    return [
        ("tal", ("tb", "a2", "lvark"),
         lambda tb, a2, lvark: _dot(tb, jnp.concatenate([a2, _bf(lvark[0:N2])], axis=1))),
        ("ro", ("arb", "tal", "r2", "lvark"), f_ro),
        ("mn", ("tal", "b2", "k2", "vst"), f_mn),
    ]


def _inproj_kernel(x_ref, w_ref, pa_ref, pb_ref):
    xb = _bf(x_ref[...])
    pa_ref[...] = _dot(xb, w_ref[:, 0:A_PROJ])
    pb_ref[...] = _dot(xb, w_ref[:, A_PROJ:])


def _inproj(x2d, w, tm):
    m = x2d.shape[0]
    return pl.pallas_call(
        _inproj_kernel,
        grid=(m // tm,),
        in_specs=[
            pl.BlockSpec((tm, D_MODEL), lambda i: (i, 0)),
            pl.BlockSpec(w.shape, lambda i: (0, 0), pipeline_mode=pl.Buffered(1)),
        ],
        out_specs=[
            pl.BlockSpec((tm, A_PROJ), lambda i: (i, 0)),
            pl.BlockSpec((tm, B_PROJ), lambda i: (i, 0)),
        ],
        out_shape=[
            jax.ShapeDtypeStruct((m, A_PROJ), F32),
            jax.ShapeDtypeStruct((m, B_PROJ), F32),
        ],
        compiler_params=pltpu.CompilerParams(
            dimension_semantics=("arbitrary",), vmem_limit_bytes=VMEM_LIMIT),
        name="inproj_ab",
    )(x2d, w)


def _rwkv_kernel(nb, tt, ct, pa_ref, shift_ref, s0_ref, mu_ref, w0_ref, w2p_ref, a0_ref, a2p_ref,
                 g2_ref, kk_ref, ka_ref, rk_ref, lng_ref, lnb_ref,
                 ya_ref, sout_ref,
                 s_scr, prev_scr, at_scr, rt_scr, bt_scr, kt_scr, v_scr, p_scr, o_scr,
                 scmask_scr, eye_scr, head_scr, lcum_scr):
    rows = nb * tt
    nchunk = rows // ROWS
    ns = ROWS // ct
    ti = pl.program_id(1)

    @pl.when((pl.program_id(0) == 0) & (ti == 0))
    def _():
        strict, incl, eye, same_head = _stream_masks(ct, 2 * N2)
        one = lambda m: jnp.where(m, 1.0, 0.0)
        scmask_scr[...] = jnp.concatenate([one(strict), one(incl)], axis=0)
        eye_scr[...] = one(eye)[:, 0:N2]
        head_scr[...] = one(same_head)[:, 0:N2]
        ri = _iota2((rows, rows), 0)
        ci = _iota2((rows, rows), 1)
        lcum_scr[...] = jnp.where((_div2(ri, ct) == _div2(ci, ct)) & (ci <= ri), 1.0, 0.0).astype(BF16)

    @pl.when(ti == 0)
    def _():
        _load_state(s_scr, s0_ref, nb)
        prev_scr[...] = shift_ref[...]

    ones_bd = _head_ones()

    p3 = pa_ref[...]
    p2 = p3.reshape(rows, A_PROJ)
    t3 = _iota2((nb, tt, A_PROJ), 1)
    rolled = pltpu.roll(p2, 1, axis=0).reshape(nb, tt, A_PROJ)
    pprev = jnp.where(t3 == 0, prev_scr[...], rolled).reshape(rows, A_PROJ)
    prev_scr[...] = p3[:, tt - 1:tt, :]
    m = p2 + (pprev - p2) * mu_ref[...]
    r = m[:, 0:WIDTH]
    k = m[:, WIDTH:2 * WIDTH]
    v = m[:, 2 * WIDTH:3 * WIDTH]
    wa = m[:, 3 * WIDTH:3 * WIDTH + LANES]
    gd = m[:, 3 * WIDTH + LANES:A_PROJ]
    z = -(w0_ref[...] + _dot(_bf(jnp.tanh(wa)), w2p_ref[...]))
    softplus = jnp.maximum(z, 0.0) + jnp.log(1.0 + jnp.exp(-jnp.abs(z)))
    lw = -jnp.exp(-softplus - 0.5)
    a = jax.nn.sigmoid(a0_ref[...] + _dot(_bf(wa), a2p_ref[...]))
    g = _dot(_bf(jax.nn.sigmoid(gd)), g2_ref[...])
    kk = k * kk_ref[...]
    k = k * (1.0 + (a - 1.0) * ka_ref[...])
    kk = kk * lax.rsqrt(jnp.maximum(_headsum(kk * kk, ones_bd), 1e-24))
    bonus = _headsum(r * k * rk_ref[...], ones_bd) * v

    cs = _dot_split_lhs(lcum_scr[...], lw, 2)
    pinv = jnp.exp(-cs)
    p_scr[...] = jnp.exp(cs)
    at_scr[...] = _bf(-kk * jnp.exp(cs - lw))
    rt_scr[...] = _bf(r * p_scr[...])
    bt_scr[...] = _bf(kk * a * pinv)
    kt_scr[...] = _bf(k * pinv)
    v_scr[...] = _bf(v)

    lane_lo = _iota2((ROWS, LANES), 1) < HEAD_DIM
    sc_mask = lambda: scmask_scr[...]
    eye = lambda: eye_scr[...]
    same_head = lambda: head_scr[...]
    nfac = ct.bit_length() - 1

    envs = []
    for c in range(nchunk):
        for j in range(PAIRS):
            rsl = slice(c * ROWS, (c + 1) * ROWS)
            lsl = slice(LANES * j, LANES * (j + 1))
            aj, rj, bj, kj, vj = (s[rsl, lsl] for s in (at_scr, rt_scr, bt_scr, kt_scr, v_scr))
            envs.append({
                "c": c, "j": j, "rsl": rsl, "lsl": lsl,
                "a2": _bdiag(aj, lane_lo), "r2": _bdiag(rj, lane_lo),
                "b2": _bdiag(bj, lane_lo), "k2": _bdiag(kj, lane_lo),
                "bk": jnp.concatenate([bj, bj, kj, kj], axis=0),
                "vst": jnp.concatenate([vj, vj], axis=0),
            })

    steps = _rwkv_score_steps(sc_mask) + _inverse_steps(nfac, eye)
    if ns == 1:
        _emit_by_level(steps + _rwkv_carried_steps(same_head), envs)
        state = [s_scr[0, j] for j in range(PAIRS)]
        for c in range(nchunk):
            cenv = envs[c * PAIRS:(c + 1) * PAIRS]
            sb = [_bf(s) for s in state]
            for j, env in enumerate(cenv):
                rp, oc = env["ro"]
                o_scr[env["rsl"], env["lsl"]] = _unstack_heads(_dot_tb(rp, sb[j]) + oc, lane_lo)
            for j, env in enumerate(cenv):
                mk, nn = env["mn"]
                pc = p_scr[(c + 1) * ROWS - 1:(c + 1) * ROWS, env["lsl"]]
                state[j] = (state[j] + _dot(sb[j], mk) + nn) * pc
        for j in range(PAIRS):
            s_scr[0, j] = state[j]
    else:
        for env in envs:
            env["sb"] = [_bf(s_scr[q, env["j"]]) for q in range(ns)]

        def f_h(a2, r2, sb):
            pieces = [_dot_tb(jnp.concatenate([_seq_rows(a2, q, ct), _seq_rows(r2, q, ct)], axis=0), sb[q])
                      for q in range(ns)]
            return (_from_seq_rows([x[0:2 * ct] for x in pieces], ct),
                    _from_seq_rows([x[2 * ct:] for x in pieces], ct))

        def f_ds(u2, vst, b2, k2):
            vst32 = vst.astype(F32)
            out = []
            for q in range(ns):
                uv = jnp.concatenate([_seq_rows(u2, q, ct), _seq_rows(vst32, q, ct)], axis=0)
                bk = jnp.concatenate([_seq_rows(b2, q, ct), _seq_rows(k2, q, ct)], axis=0)
                out.append(_dot_ta(_bf(_pad_rows(uv, LANES)), _pad_rows(bk, LANES)) * same_head())
            return out

        steps += [
            ("h", ("a2", "r2", "sb"), f_h),
            ("u2", ("tb", "h", "lvark"), lambda tb, h, lvark: _dot(tb, _bf(h[0] + lvark[0:N2]))),
            ("o2", ("arb", "u2", "h", "lvark"),
             lambda arb, u2, h, lvark: h[1] + _dot(arb, _bf(u2)) + lvark[N2:]),
            ("ds", ("u2", "vst", "b2", "k2"), f_ds),
        ]
        _emit_by_level(steps, envs)
        for env in envs:
            o_scr[env["rsl"], env["lsl"]] = _unstack_heads(env["o2"], lane_lo)
            for q in range(ns):
                pc = p_scr[(q + 1) * ct - 1:(q + 1) * ct, env["lsl"]]
                s_scr[q, env["j"]] = (s_scr[q, env["j"]] + env["ds"][q]) * pc

    o = o_scr[...]
    y = (_head_norm(o, lng_ref[...], lnb_ref[...], A_GN_EPS, ones_bd) + bonus) * g
    ya_ref[...] = _bf(y)

    @pl.when(ti == pl.num_programs(1) - 1)
    def _():
        _store_state(sout_ref, s_scr, nb)


def _row_spec(shape):
    return pl.BlockSpec(shape, lambda b, t: (0,) * len(shape))


def _check_mixer_blocking(bsz, tlen, nb, tt):
    ct = min(ROWS, tt)
    rows = nb * tt
    assert rows % ROWS == 0 and ROWS % ct == 0 and bsz % nb == 0 and tlen % tt == 0
    assert nb == 1 or (tt == tlen and rows == ROWS)
    return ct, rows


def _rwkv(pa3, shift0, s0, prm, nb, tt):
    bsz, tlen, _ = pa3.shape
    ct, rows = _check_mixer_blocking(bsz, tlen, nb, tt)
    grid = (bsz // nb, tlen // tt)
    state_spec = pl.BlockSpec((nb, HEADS, HEAD_DIM, HEAD_DIM), lambda b, t: (b, 0, 0, 0))
    vec = lambda n: _row_spec((1, n))
    in_specs = [
        pl.BlockSpec((nb, tt, A_PROJ), lambda b, t: (b, t, 0)),
        pl.BlockSpec((nb, 1, A_PROJ), lambda b, t: (b, 0, 0)),
        state_spec,
        vec(A_PROJ), vec(WIDTH), _row_spec((LANES, WIDTH)), vec(WIDTH), _row_spec((LANES, WIDTH)),
        _row_spec((GATE_LORA, WIDTH)), vec(WIDTH), vec(WIDTH), vec(WIDTH), vec(WIDTH), vec(WIDTH),
    ]
    out_specs = [
        pl.BlockSpec((rows, WIDTH), lambda b, t: (b * (tlen // tt) + t, 0)),
        state_spec,
    ]
    scratch = [
        pltpu.VMEM((nb, PAIRS, LANES, LANES), F32),
        pltpu.VMEM((nb, 1, A_PROJ), F32),
        pltpu.VMEM((rows, WIDTH), BF16),
        pltpu.VMEM((rows, WIDTH), BF16),
        pltpu.VMEM((rows, WIDTH), BF16),
        pltpu.VMEM((rows, WIDTH), BF16),
        pltpu.VMEM((rows, WIDTH), BF16),
        pltpu.VMEM((rows, WIDTH), F32),
        pltpu.VMEM((rows, WIDTH), F32),
        pltpu.VMEM((2 * N2, 2 * N2), F32),
        pltpu.VMEM((N2, N2), F32),
        pltpu.VMEM((N2, N2), F32),
        pltpu.VMEM((rows, rows), BF16),
    ]
    return pl.pallas_call(
        functools.partial(_rwkv_kernel, nb, tt, ct),
        grid=grid,
        in_specs=in_specs,
        out_specs=out_specs,
        out_shape=[
            jax.ShapeDtypeStruct((bsz * tlen, WIDTH), BF16),
            jax.ShapeDtypeStruct(s0.shape, F32),
        ],
        scratch_shapes=scratch,
        compiler_params=pltpu.CompilerParams(
            dimension_semantics=("arbitrary", "arbitrary"), vmem_limit_bytes=VMEM_LIMIT),
        name="rwkv7_mixer",
    )(pa3, shift0.reshape(bsz, 1, A_PROJ), s0, *prm)


def _ret_kernel(nb, tt, ct, pos0, pb_ref, s0_ref, invf_ref, gng_ref, gnb_ref,
                yb_ref, sout_ref,
                s_scr, q_scr, k_scr, v_scr, o_scr, dec_scr, head_scr, trig_scr):
    rows = nb * tt
    nchunk = rows // ROWS
    ns = ROWS // ct
    ti = pl.program_id(1)

    @pl.when((pl.program_id(0) == 0) & (ti == 0))
    def _():
        _, incl, _, same_head = _stream_masks(ct, N2)
        head_scr[...] = jnp.where(same_head, 1.0, 0.0)
        ri = _iota2((N2, N2), 0)
        ci = _iota2((N2, N2), 1)
        tpos = _mod2(ri, ct).astype(F32)
        tdiff = (_mod2(ri, ct) - _mod2(ci, ct)).astype(F32)
        for j in range(PAIRS):
            head = (2 * j + _div2(ri, ROWS)).astype(F32)
            lg = jnp.log1p(-jnp.exp2(-5.0 - head))
            dec_scr[j, 0] = jnp.where(incl, jnp.exp(lg * jnp.maximum(tdiff, 0.0)), 0.0)
            dec_scr[j, 1] = jnp.exp(lg * (tpos + 1.0))
            dec_scr[j, 2] = jnp.exp(lg * (ct - 1.0 - tpos))
            dec_scr[j, 3] = jnp.exp(lg * ct)
        off = _mod2(_iota2((rows, LANES), 0), tt).astype(F32) * invf_ref[...]
        trig_scr[0] = jnp.cos(off)
        trig_scr[1] = jnp.sin(off)

    @pl.when(ti == 0)
    def _():
        _load_state(s_scr, s0_ref, nb)

    ones_bd = _head_ones()
    pb = pb_ref[...].reshape(rows, B_PROJ)
    q = pb[:, 0:WIDTH]
    k = pb[:, WIDTH:2 * WIDTH]
    gate = pb[:, 3 * WIDTH:]

    base = jnp.zeros((SUBLANES, LANES), F32) + (pos0 + ti * tt).astype(F32)
    ang = base * invf_ref[...]
    cos_a, sin_a = jnp.cos(ang)[0:1], jnp.sin(ang)[0:1]
    cos_b, sin_b = trig_scr[0], trig_scr[1]
    cos = jnp.concatenate([cos_a * cos_b - sin_a * sin_b] * PAIRS, axis=1)
    sin = jnp.concatenate([sin_a * cos_b + cos_a * sin_b] * PAIRS, axis=1)
    first_half = _mod2(_iota2((rows, WIDTH), 1), HEAD_DIM) < (HEAD_DIM // 2)
    sin = jnp.where(first_half, -sin, sin)

    pr = _iota2((LANES, LANES), 0)
    pc = _iota2((LANES, LANES), 1)
    swap = jnp.where(pr == lax.bitwise_xor(pc, HEAD_DIM // 2), 1.0, 0.0).astype(BF16)

    def rope(x):
        xb = _bf(x)
        partner = jnp.concatenate([_dot(xb[:, LANES * j:LANES * (j + 1)], swap) for j in range(PAIRS)], axis=1)
        return x * cos + partner * sin

    q_scr[...] = _bf(rope(q))
    k_scr[...] = rope(k) * (HEAD_DIM ** -0.5)
    v_scr[...] = _bf(pb[:, 2 * WIDTH:3 * WIDTH])

    lane_lo = _iota2((ROWS, LANES), 1) < HEAD_DIM
    same_head = lambda: head_scr[...]

    envs = []
    for c in range(nchunk):
        for j in range(PAIRS):
            rsl = slice(c * ROWS, (c + 1) * ROWS)
            lsl = slice(LANES * j, LANES * (j + 1))
            qj, kj, vj = q_scr[rsl, lsl], k_scr[rsl, lsl], v_scr[rsl, lsl]
            kjb = _bf(kj)
            envs.append({
                "c": c, "j": j, "rsl": rsl, "lsl": lsl,
                "q2": _bdiag(qj, lane_lo),
                "kst": jnp.concatenate([kjb, kjb], axis=0),
                "k2d": _bdiag(kj, lane_lo) * dec_scr[j, 2],
                "vst": jnp.concatenate([vj, vj], axis=0),
                "dmask": lambda j=j: dec_scr[j, 0],
            })

    def f_ds(k2d, vst):
        return [_dot_ta(_bf(_pad_rows(_seq_rows(k2d, s, ct), LANES)), _pad_rows(_seq_rows(vst, s, ct), LANES))
                * same_head() for s in range(ns)]

    steps = [
        ("sc", ("q2", "kst", "dmask"), lambda q2, kst, dmask: _bf(_dot_tb(q2, kst) * dmask())),
        ("inner", ("sc", "vst"), _dot),
        ("ds", ("k2d", "vst"), f_ds),
    ]
    _emit_by_level(steps, envs)

    if ns == 1:
        state = [s_scr[0, j] for j in range(PAIRS)]
        for c in range(nchunk):
            for j in range(PAIRS):
                env = envs[c * PAIRS + j]
                env["sb"] = [_bf(state[j])]
                state[j] = state[j] * dec_scr[j, 3] + env["ds"][0]
        for j in range(PAIRS):
            s_scr[0, j] = state[j]
    else:
        for env in envs:
            j = env["j"]
            env["sb"] = [_bf(s_scr[s, j]) for s in range(ns)]
            for s in range(ns):
                s_scr[s, j] = s_scr[s, j] * dec_scr[j, 3] + env["ds"][s]

    def f_cross(q2, sb):
        return _from_seq_rows([_dot(_seq_rows(q2, s, ct), sb[s]) for s in range(ns)], ct)

    _emit_by_level([("cross", ("q2", "sb"), f_cross)], envs)
    for env in envs:
        o2 = env["inner"] + env["cross"] * dec_scr[env["j"], 1]
        o_scr[env["rsl"], env["lsl"]] = _unstack_heads(o2, lane_lo)

    o = o_scr[...]
    y = jax.nn.silu(gate) * _head_norm(o, gng_ref[...], gnb_ref[...], B_GN_EPS, ones_bd)
    yb_ref[...] = _bf(y)

    @pl.when(ti == pl.num_programs(1) - 1)
    def _():
        _store_state(sout_ref, s_scr, nb)


def _retention(pb3, s0, invf, gn_g, gn_b, nb, tt, pos0):
    bsz, tlen, _ = pb3.shape
    ct, rows = _check_mixer_blocking(bsz, tlen, nb, tt)
    grid = (bsz // nb, tlen // tt)
    state_spec = pl.BlockSpec((nb, HEADS, HEAD_DIM, HEAD_DIM), lambda b, t: (b, 0, 0, 0))
    in_specs = [
        pl.BlockSpec((nb, tt, B_PROJ), lambda b, t: (b, t, 0)),
        state_spec,
        _row_spec((1, LANES)), _row_spec((1, WIDTH)), _row_spec((1, WIDTH)),
    ]
    out_specs = [
        pl.BlockSpec((rows, WIDTH), lambda b, t: (b * (tlen // tt) + t, 0)),
        state_spec,
    ]
    scratch = [
        pltpu.VMEM((nb, PAIRS, LANES, LANES), F32),
        pltpu.VMEM((rows, WIDTH), BF16),
        pltpu.VMEM((rows, WIDTH), F32),
        pltpu.VMEM((rows, WIDTH), BF16),
        pltpu.VMEM((rows, WIDTH), F32),
        pltpu.VMEM((PAIRS, 4, N2, N2), F32),
        pltpu.VMEM((N2, N2), F32),
        pltpu.VMEM((2, rows, LANES), F32),
    ]
    return pl.pallas_call(
        functools.partial(_ret_kernel, nb, tt, ct, pos0),
        grid=grid,
        in_specs=in_specs,
        out_specs=out_specs,
        out_shape=[
            jax.ShapeDtypeStruct((bsz * tlen, WIDTH), BF16),
            jax.ShapeDtypeStruct(s0.shape, F32),
        ],
        scratch_shapes=scratch,
        compiler_params=pltpu.CompilerParams(
            dimension_semantics=("arbitrary", "arbitrary"), vmem_limit_bytes=VMEM_LIMIT),
        name="retention_mixer",
    )(pb3, s0, invf, gn_g, gn_b)


def _conv_kernel(nb, tt, nsplit, x_ref, buf_ref, win_ref, cw_ref, z_ref, bout_ref, halo_scr):
    rows = nb * tt
    sub = rows // nsplit
    ti = pl.program_id(1)

    @pl.when(ti == 0)
    def _():
        halo_scr[...] = buf_ref[...]

    proj = []
    for i in range(nsplit):
        xi = x_ref[0, i * sub:(i + 1) * sub, :] if nb == 1 else x_ref[...].reshape(rows, D_MODEL)
        xb = _bf(xi)
        bg = _dot(xb, win_ref[:, 0:D_MODEL])
        u = _dot(xb, win_ref[:, D_MODEL:2 * D_MODEL]) * _dot(xb, win_ref[:, 2 * D_MODEL:])
        proj.append((bg, u))

    cw = cw_ref[...]
    st = tt // nsplit if nb == 1 else tt
    t3 = _iota2((nb, st, D_MODEL), 1)
    h0 = halo_scr[:, 0:1, :]
    h1 = halo_scr[:, 1:2, :]
    for i, (bg, u) in enumerate(proj):
        u3 = u.reshape(nb, st, D_MODEL)
        prev1 = jnp.where(t3 == 0, h1, pltpu.roll(u, 1, axis=0).reshape(nb, st, D_MODEL))
        prev2 = jnp.where(t3 == 0, h0,
                          jnp.where(t3 == 1, h1, pltpu.roll(u, 2, axis=0).reshape(nb, st, D_MODEL)))
        conv = prev2 * cw[0:1, :] + prev1 * cw[1:2, :] + u3 * cw[2:3, :]
        z_ref[i * sub:(i + 1) * sub, :] = _bf(bg * conv.reshape(sub, D_MODEL))
        h0 = u3[:, st - 2:st - 1, :]
        h1 = u3[:, st - 1:st, :]
    halo_scr[:, 0:1, :] = h0
    halo_scr[:, 1:2, :] = h1

    @pl.when(ti == pl.num_programs(1) - 1)
    def _():
        bout_ref[...] = halo_scr[...]


def _conv_mixer(x3, buf0, win, cw, nb, tt, nsplit):
    bsz, tlen, _ = x3.shape
    rows = nb * tt
    assert bsz % nb == 0 and tlen % tt == 0
    assert nsplit == 1 or nb == 1
    assert tt % nsplit == 0 and tt // nsplit >= CONV_W - 1
    grid = (bsz // nb, tlen // tt)
    buf_spec = pl.BlockSpec((nb, CONV_W - 1, D_MODEL), lambda b, t: (b, 0, 0))
    return pl.pallas_call(
        functools.partial(_conv_kernel, nb, tt, nsplit),
        grid=grid,
        in_specs=[
            pl.BlockSpec((nb, tt, D_MODEL), lambda b, t: (b, t, 0)),
            buf_spec,
            pl.BlockSpec(win.shape, lambda b, t: (0, 0), pipeline_mode=pl.Buffered(1)),
            _row_spec((CONV_W, D_MODEL)),
        ],
        out_specs=[
            pl.BlockSpec((rows, D_MODEL), lambda b, t: (b * (tlen // tt) + t, 0)),
            buf_spec,
        ],
        out_shape=[
            jax.ShapeDtypeStruct((bsz * tlen, D_MODEL), BF16),
            jax.ShapeDtypeStruct(buf0.shape, F32),
        ],
        scratch_shapes=[pltpu.VMEM((nb, CONV_W - 1, D_MODEL), F32)],
        compiler_params=pltpu.CompilerParams(
            dimension_semantics=("arbitrary", "arbitrary"), vmem_limit_bytes=VMEM_LIMIT),
        name="conv_mixer",
    )(x3, buf0, win, cw)


FF_CHUNK = 1024


def _post_kernel(npieces, nsplit, *refs):
    x_ref = refs[0]
    y_refs = refs[1:1 + npieces]
    wout_ref, g1_ref, b1_ref, wup_ref, wdown_ref, g2_ref, b2_ref, out_ref = refs[1 + npieces:]
    sub = x_ref.shape[0] // nsplit

    def f_y(r):
        y = None
        off = 0
        for y_ref in y_refs:
            n = y_ref.shape[1]
            term = _dot(y_ref[r, :], wout_ref[off:off + n, :])
            y = term if y is None else y + term
            off += n
        return y

    def f_mlp(f):
        def step(x1b, acc):
            h = jnp.maximum(_dot(x1b, wup_ref[:, f * FF_CHUNK:(f + 1) * FF_CHUNK]), 0.0)
            term = _dot(_bf(h * h), wdown_ref[f * FF_CHUNK:(f + 1) * FF_CHUNK, :])
            return term if acc is None else acc + term
        return step

    def f_out(r, x1, acc):
        out_ref[r, :] = _layer_norm(ALPHA * x1 + acc, g2_ref[...], b2_ref[...])
        return None

    steps = [
        ("y", ("r",), f_y),
        ("x1", ("r", "y"), lambda r, y: _layer_norm(ALPHA * x_ref[r, :] + y, g1_ref[...], b1_ref[...])),
        ("x1b", ("x1",), _bf),
        ("acc-1", ("x1b",), lambda x1b: None),
    ]
    nff = D_FF // FF_CHUNK
    for f in range(nff):
        steps.append((f"acc{f}", ("x1b", f"acc{f - 1}"), f_mlp(f)))
    steps.append(("out", ("r", "x1", f"acc{nff - 1}"), f_out))
    envs = [{"r": slice(i * sub, (i + 1) * sub)} for i in range(nsplit)]
    _emit_by_level(steps, envs)


def _post(x2d, ys, wout, g1, b1, wup, wdown, g2, b2, tm, nsplit):
    m = x2d.shape[0]
    assert m % tm == 0 and tm % nsplit == 0
    const = lambda i: (0, 0)
    resident = lambda a: pl.BlockSpec(a.shape, const, pipeline_mode=pl.Buffered(1))
    vec = pl.BlockSpec((1, D_MODEL), const)
    in_specs = [pl.BlockSpec((tm, D_MODEL), lambda i: (i, 0))]
    in_specs += [pl.BlockSpec((tm, y.shape[1]), lambda i: (i, 0)) for y in ys]
    in_specs += [resident(wout), vec, vec, resident(wup), resident(wdown), vec, vec]
    return pl.pallas_call(
        functools.partial(_post_kernel, len(ys), nsplit),
        grid=(m // tm,),
        in_specs=in_specs,
        out_specs=pl.BlockSpec((tm, D_MODEL), lambda i: (i, 0)),
        out_shape=jax.ShapeDtypeStruct((m, D_MODEL), F32),
        compiler_params=pltpu.CompilerParams(
            dimension_semantics=("arbitrary",), vmem_limit_bytes=VMEM_LIMIT),
        name="post_block",
    )(x2d, *ys, wout, g1, b1, wup, wdown, g2, b2)


def _trunk(x, st_shift, st_wkv, st_ret, st_conv, pos0, blk, w):
    bsz, tlen, _ = x.shape
    m = bsz * tlen
    x2d = x.reshape(m, D_MODEL)
    row = lambda a: a.reshape(1, -1)

    pa, pb = _inproj(x2d, w["w_in_ab"], blk["tm_in"])
    pa3 = pa.reshape(bsz, tlen, A_PROJ)
    rw_prm = (row(w["mu_a"]), row(w["w0"]), w["w2p"], row(w["a0"]), w["a2p"], w["g2"],
              row(w["k_k"]), row(w["k_a"]), row(w["r_k"]), row(w["lnx_g"]), row(w["lnx_b"]))
    ya, wkv1 = _rwkv(pa3, st_shift, st_wkv, rw_prm, blk["nb_mix"], blk["tt_mix"])
    yb, ret1 = _retention(pb.reshape(bsz, tlen, B_PROJ), st_ret, w["invf"], row(w["gn_g"]),
                          row(w["gn_b"]), blk["nb_mix"], blk["tt_mix"], pos0)
    shift1 = pa3[:, tlen - 1, :]
    x2d = _post(x2d, [ya, yb], w["w_out_ab"], row(w["ln1_g"][0]), row(w["ln1_b"][0]),
                w["w_up"][0], w["w_down"][0], row(w["ln2_g"][0]), row(w["ln2_b"][0]),
                blk["tm_post"], blk["split"])

    z, conv1 = _conv_mixer(x2d.reshape(bsz, tlen, D_MODEL), st_conv, w["w_in_conv"], w["conv_w"],
                           blk["nb_conv"], blk["tt_conv"], blk["split_conv"])
    x2d = _post(x2d, [z], w["w_out_conv"], row(w["ln1_g"][1]), row(w["ln1_b"][1]),
                w["w_up"][1], w["w_down"][1], row(w["ln2_g"][1]), row(w["ln2_b"][1]),
                blk["tm_post"], blk["split"])
    return x2d.reshape(bsz, tlen, D_MODEL), shift1[None], wkv1[None], ret1[None], conv1[None]


def kernel(x_prompt, x_sample, state_shift, state_wkv, state_ret, state_conv, w_in_ab, mu_a, w0, w2, a0, a2,
           g2, k_k, k_a, r_k, lnx_g, lnx_b, gn_g, gn_b, w_out_ab, w_in_conv, conv_w, w_out_conv,
           ln1_g, ln1_b, ln2_g, ln2_b, w_up, w_down):
    bp, tp, _ = x_prompt.shape
    half = HEAD_DIM // 2
    inv = ROPE_BASE ** (-jnp.arange(half, dtype=F32) / half)
    zpad = jnp.zeros((LANES - DECAY_LORA, WIDTH), F32)
    w = {
        "w_in_ab": _bf(w_in_ab[0]),
        "mu_a": mu_a[0], "w0": w0[0], "a0": a0[0],
        "w2p": _bf(jnp.concatenate([w2[0], zpad], axis=0)),
        "a2p": _bf(jnp.concatenate([zpad, a2[0]], axis=0)),
        "g2": _bf(g2[0]), "k_k": k_k[0], "k_a": k_a[0], "r_k": r_k[0],
        "lnx_g": lnx_g[0], "lnx_b": lnx_b[0], "gn_g": gn_g[0], "gn_b": gn_b[0],
        "invf": jnp.tile(inv, LANES // half).reshape(1, LANES),
        "w_out_ab": _bf(w_out_ab[0]), "w_in_conv": _bf(w_in_conv[0]), "conv_w": conv_w[0],
        "w_out_conv": _bf(w_out_conv[0]),
        "ln1_g": ln1_g, "ln1_b": ln1_b, "ln2_g": ln2_g, "ln2_b": ln2_b,
        "w_up": [_bf(w_up[l]) for l in range(DEPTH)], "w_down": [_bf(w_down[l]) for l in range(DEPTH)],
    }
    dt = state_wkv.dtype
    z_shift = jnp.zeros((bp, A_PROJ), dt)
    z_state = jnp.zeros((bp, HEADS, HEAD_DIM, HEAD_DIM), dt)
    z_conv = jnp.zeros((bp, CONV_W - 1, D_MODEL), dt)
    blk_p = {"tm_in": 512, "tm_post": 1024, "split": 2, "nb_mix": 1, "tt_mix": 256,
             "nb_conv": 1, "tt_conv": 1024, "split_conv": 2}
    y_p, p_shift, p_wkv, p_ret, p_conv = _trunk(x_prompt, z_shift, z_state, z_state, z_conv, 0, blk_p, w)
    ts = x_sample.shape[1]
    blk_s = {"tm_in": 512, "tm_post": 1024, "split": 2, "nb_mix": ROWS // ts, "tt_mix": ts,
             "nb_conv": 256 // ts, "tt_conv": ts, "split_conv": 1}
    y_s, s_shift, s_wkv, s_ret, s_conv = _trunk(x_sample, state_shift[0], state_wkv[0], state_ret[0],
                                                state_conv[0], PAST_LEN, blk_s, w)
    return (y_p, y_s, p_shift, p_wkv, p_ret, p_conv, s_shift, s_wkv, s_ret, s_conv)
```

```python
import functools

import jax
import jax.numpy as jnp
from jax import lax
from jax.experimental import pallas as pl
from jax.experimental.pallas import tpu as pltpu

F32 = jnp.float32
BF16 = jnp.bfloat16

D_MODEL = 1024
DEPTH = 2
PAST_LEN = 16384
HEADS = 8
HEAD_DIM = 64
WIDTH = HEADS * HEAD_DIM
DECAY_LORA = 64
AAA_LORA = 64
GATE_LORA = 128
A_PROJ = 3 * WIDTH + DECAY_LORA + AAA_LORA + GATE_LORA
B_PROJ = 4 * WIDTH
A_GN_EPS = 64e-5
B_GN_EPS = 1e-5
ROPE_BASE = 10000.0
CONV_W = 3
D_FF = 4 * D_MODEL
LN_EPS = 1e-5
ALPHA = (2.0 * DEPTH) ** 0.25

LANES = 128
SUBLANES = 8
PAIRS = WIDTH // LANES
ROWS = 64
N2 = 2 * ROWS
VMEM_LIMIT = 56 * 1024 * 1024


def _bf(x):
    return x.astype(BF16)


def _dot(a, b):
    return jnp.dot(a, b, preferred_element_type=F32)


def _dot_tb(a, b):
    return lax.dot_general(a, b, (((1,), (1,)), ((), ())), preferred_element_type=F32)


def _dot_ta(a, b):
    return lax.dot_general(a, b, (((0,), (0,)), ((), ())), preferred_element_type=F32)


def _dot_split_lhs(m, x, parts):
    acc = None
    rem = x
    for i in range(parts):
        hi = _bf(rem)
        term = _dot(m, hi)
        acc = term if acc is None else acc + term
        if i + 1 < parts:
            rem = rem - hi.astype(F32)
    return acc


def _iota2(shape, dim):
    return lax.broadcasted_iota(jnp.int32, shape, dim)


def _div2(x, n):
    assert n & (n - 1) == 0
    return lax.shift_right_logical(x, n.bit_length() - 1)


def _mod2(x, n):
    assert n & (n - 1) == 0
    return lax.bitwise_and(x, n - 1)


def _head_ones():
    ri = _iota2((LANES, LANES), 0)
    ci = _iota2((LANES, LANES), 1)
    return jnp.where(_div2(ri, HEAD_DIM) == _div2(ci, HEAD_DIM), 1.0, 0.0).astype(BF16)


def _headsum(x, ones_bd):
    xb = _bf(x)
    return jnp.concatenate([_dot(xb[:, LANES * j:LANES * (j + 1)], ones_bd) for j in range(PAIRS)], axis=1)


def _head_norm(o, g, b, eps, ones_bd):
    mu = _headsum(o, ones_bd) * (1.0 / HEAD_DIM)
    d = o - mu
    var = _headsum(d * d, ones_bd) * (1.0 / HEAD_DIM)
    return d * lax.rsqrt(var + eps) * g + b


def _layer_norm(z, g, b):
    mu = jnp.mean(z, axis=-1, keepdims=True)
    d = z - mu
    var = jnp.mean(d * d, axis=-1, keepdims=True)
    return d * lax.rsqrt(var + LN_EPS) * g + b


def _bdiag(x, lane_lo):
    zero = jnp.zeros_like(x)
    return jnp.concatenate([jnp.where(lane_lo, x, zero), jnp.where(lane_lo, zero, x)], axis=0)


def _unstack_heads(x, lane_lo):
    return jnp.where(lane_lo, x[0:ROWS], x[ROWS:])


def _stream_masks(ct, width):
    ri = _iota2((N2, width), 0)
    ci = _mod2(_iota2((N2, width), 1), N2)
    same = _div2(ri, ct) == _div2(ci, ct)
    return same & (ci < ri), same & (ci <= ri), ri == ci, _div2(ri, ROWS) == _div2(ci, ROWS)


def _load_state(s_scr, s0_ref, nb):
    z = jnp.zeros((HEAD_DIM, HEAD_DIM), F32)
    for s in range(nb):
        for j in range(PAIRS):
            top = jnp.concatenate([s0_ref[s, 2 * j], z], axis=1)
            bot = jnp.concatenate([z, s0_ref[s, 2 * j + 1]], axis=1)
            s_scr[s, j] = jnp.concatenate([top, bot], axis=0)


def _store_state(sout_ref, s_scr, nb):
    for s in range(nb):
        for j in range(PAIRS):
            s2 = s_scr[s, j]
            sout_ref[s, 2 * j] = s2[0:HEAD_DIM, 0:HEAD_DIM]
            sout_ref[s, 2 * j + 1] = s2[HEAD_DIM:, HEAD_DIM:]


def _seq_rows(x, q, ct):
    if ct == ROWS:
        return x
    return jnp.concatenate([x[q * ct:(q + 1) * ct], x[ROWS + q * ct:ROWS + (q + 1) * ct]], axis=0)


def _from_seq_rows(pieces, ct):
    if ct == ROWS:
        return pieces[0]
    return jnp.concatenate([p[0:ct] for p in pieces] + [p[ct:2 * ct] for p in pieces], axis=0)


def _pad_rows(x, rows):
    if x.shape[0] == rows:
        return x
    return jnp.concatenate([x, jnp.zeros((rows - x.shape[0], x.shape[1]), x.dtype)], axis=0)


def _emit_by_level(steps, envs):
    level = {}
    for name, deps, _ in steps:
        level[name] = 1 + max([level.get(d, 0) for d in deps], default=0)
    for lv in sorted(set(level.values())):
        for name, deps, fn in steps:
            if level[name] == lv:
                for env in envs:
                    env[name] = fn(*[env[d] for d in deps])


def _inverse_steps(nfac, eye):
    def advance(last):
        def fn(q, p):
            pb = _bf(p)
            if last:
                return p + _dot(q, pb), None
            x = _dot(q, jnp.concatenate([pb, q], axis=1))
            return p + x[:, 0:N2], _bf(x[:, N2:])
        return fn

    steps = [
        ("q0", ("l",), _bf),
        ("p1", ("l",), lambda l: l + eye()),
        ("q1", ("q0",), lambda q: _bf(_dot(q, q))),
    ]
    for i in range(1, nfac):
        steps.append((f"s{i}", (f"q{i}", f"p{i}"), advance(i == nfac - 1)))
        steps.append((f"p{i + 1}", (f"s{i}",), lambda s: s[0]))
        steps.append((f"q{i + 1}", (f"s{i}",), lambda s: s[1]))
    steps.append(("tb", (f"p{nfac}",), _bf))
    return steps


def _rwkv_score_steps(sc_mask):
    return [
        ("scm", ("a2", "r2", "bk"),
         lambda a2, r2, bk: _dot_tb(jnp.concatenate([a2, r2], axis=0), bk) * sc_mask()),
        ("l", ("scm",), lambda scm: scm[0:N2, 0:N2]),
        ("lkb", ("scm",), lambda scm: _bf(scm[:, N2:])),
        ("arb", ("scm",), lambda scm: _bf(scm[N2:, 0:N2])),
        ("lvark", ("lkb", "vst"), _dot),
    ]


def _rwkv_carried_steps(same_head):
    def f_ro(arb, tal, r2, lvark):
        x = _dot(arb, _bf(tal))
        return _bf(r2.astype(F32) + x[:, 0:N2]), x[:, N2:] + lvark[N2:]

    def f_mn(tal, b2, k2, vst):
        mm = _dot_ta(_bf(tal), b2)
        n2 = _dot_ta(vst, k2)
        return _bf(mm[0:N2]), (mm[N2:] + n2) * same_head()

    return [
        ("tal", ("tb", "a2", "lvark"),
         lambda tb, a2, lvark: _dot(tb, jnp.concatenate([a2, _bf(lvark[0:N2])], axis=1))),
        ("ro", ("arb", "tal", "r2", "lvark"), f_ro),
        ("mn", ("tal", "b2", "k2", "vst"), f_mn),
    ]


def _inproj_kernel(x_ref, w_ref, pa_ref, pb_ref):
    xb = _bf(x_ref[...])
    pa_ref[...] = _dot(xb, w_ref[:, 0:A_PROJ])
    pb_ref[...] = _dot(xb, w_ref[:, A_PROJ:])


def _inproj(x2d, w, tm):
    m = x2d.shape[0]
    return pl.pallas_call(
        _inproj_kernel,
        grid=(m // tm,),
        in_specs=[
            pl.BlockSpec((tm, D_MODEL), lambda i: (i, 0)),
            pl.BlockSpec(w.shape, lambda i: (0, 0), pipeline_mode=pl.Buffered(1)),
        ],
        out_specs=[
            pl.BlockSpec((tm, A_PROJ), lambda i: (i, 0)),
            pl.BlockSpec((tm, B_PROJ), lambda i: (i, 0)),
        ],
        out_shape=[
            jax.ShapeDtypeStruct((m, A_PROJ), F32),
            jax.ShapeDtypeStruct((m, B_PROJ), F32),
        ],
        compiler_params=pltpu.CompilerParams(
            dimension_semantics=("arbitrary",), vmem_limit_bytes=VMEM_LIMIT),
        name="inproj_ab",
    )(x2d, w)


def _rwkv_kernel(nb, tt, ct, pa_ref, shift_ref, s0_ref, mu_ref, w0_ref, w2p_ref, a0_ref, a2p_ref,
                 g2_ref, kk_ref, ka_ref, rk_ref, lng_ref, lnb_ref,
                 ya_ref, sout_ref,
                 s_scr, prev_scr, at_scr, rt_scr, bt_scr, kt_scr, v_scr, p_scr, o_scr,
                 scmask_scr, eye_scr, head_scr, lcum_scr):
    rows = nb * tt
    nchunk = rows // ROWS
    ns = ROWS // ct
    ti = pl.program_id(1)

    @pl.when((pl.program_id(0) == 0) & (ti == 0))
    def _():
        strict, incl, eye, same_head = _stream_masks(ct, 2 * N2)
        one = lambda m: jnp.where(m, 1.0, 0.0)
        scmask_scr[...] = jnp.concatenate([one(strict), one(incl)], axis=0)
        eye_scr[...] = one(eye)[:, 0:N2]
        head_scr[...] = one(same_head)[:, 0:N2]
        ri = _iota2(lcum_scr.shape, 0)
        ci = _iota2(lcum_scr.shape, 1)
        lcum_scr[...] = jnp.where((_div2(ri, ct) == _div2(ci, ct)) & (ci <= ri), 1.0, 0.0).astype(BF16)

    @pl.when(ti == 0)
    def _():
        _load_state(s_scr, s0_ref, nb)
        prev_scr[...] = shift_ref[...]

    ones_bd = _head_ones()

    p3 = pa_ref[...]
    p2 = p3.reshape(rows, A_PROJ)
    t3 = _iota2((nb, tt, A_PROJ), 1)
    rolled = pltpu.roll(p2, 1, axis=0).reshape(nb, tt, A_PROJ)
    pprev = jnp.where(t3 == 0, prev_scr[...], rolled).reshape(rows, A_PROJ)
    prev_scr[...] = p3[:, tt - 1:tt, :]
    m = p2 + (pprev - p2) * mu_ref[...]
    r = m[:, 0:WIDTH]
    k = m[:, WIDTH:2 * WIDTH]
    v = m[:, 2 * WIDTH:3 * WIDTH]
    wa = m[:, 3 * WIDTH:3 * WIDTH + LANES]
    gd = m[:, 3 * WIDTH + LANES:A_PROJ]
    z = -(w0_ref[...] + _dot(_bf(jnp.tanh(wa)), w2p_ref[...]))
    softplus = jnp.maximum(z, 0.0) + jnp.log(1.0 + jnp.exp(-jnp.abs(z)))
    lw = -jnp.exp(-softplus - 0.5)
    a = jax.nn.sigmoid(a0_ref[...] + _dot(_bf(wa), a2p_ref[...]))
    g = _dot(_bf(jax.nn.sigmoid(gd)), g2_ref[...])
    kk = k * kk_ref[...]
    k = k * (1.0 + (a - 1.0) * ka_ref[...])
    kk = kk * lax.rsqrt(jnp.maximum(_headsum(kk * kk, ones_bd), 1e-24))
    bonus = _headsum(r * k * rk_ref[...], ones_bd) * v

    span = lcum_scr.shape[0]
    cs = jnp.concatenate([_dot_split_lhs(lcum_scr[...], lw[i * span:(i + 1) * span], 2)
                          for i in range(rows // span)], axis=0)
    pinv = jnp.exp(-cs)
    p_scr[...] = jnp.exp(cs)
    at_scr[...] = _bf(-kk * jnp.exp(cs - lw))
    rt_scr[...] = _bf(r * p_scr[...])
    bt_scr[...] = _bf(kk * a * pinv)
    kt_scr[...] = _bf(k * pinv)
    v_scr[...] = _bf(v)

    lane_lo = _iota2((ROWS, LANES), 1) < HEAD_DIM
    sc_mask = lambda: scmask_scr[...]
    eye = lambda: eye_scr[...]
    same_head = lambda: head_scr[...]
    nfac = ct.bit_length() - 1

    envs = []
    for c in range(nchunk):
        for j in range(PAIRS):
            rsl = slice(c * ROWS, (c + 1) * ROWS)
            lsl = slice(LANES * j, LANES * (j + 1))
            aj, rj, bj, kj, vj = (s[rsl, lsl] for s in (at_scr, rt_scr, bt_scr, kt_scr, v_scr))
            envs.append({
                "c": c, "j": j, "rsl": rsl, "lsl": lsl,
                "a2": _bdiag(aj, lane_lo), "r2": _bdiag(rj, lane_lo),
                "b2": _bdiag(bj, lane_lo), "k2": _bdiag(kj, lane_lo),
                "bk": jnp.concatenate([bj, bj, kj, kj], axis=0),
                "vst": jnp.concatenate([vj, vj], axis=0),
            })

    steps = _rwkv_score_steps(sc_mask) + _inverse_steps(nfac, eye)
    if ns == 1:
        _emit_by_level(steps + _rwkv_carried_steps(same_head), envs)
        per_seq = tt // ROWS
        state = [[s_scr[s, j] for j in range(PAIRS)] for s in range(nb)]
        for cpos in range(per_seq):
            for s in range(nb):
                c = s * per_seq + cpos
                cenv = envs[c * PAIRS:(c + 1) * PAIRS]
                sb = [_bf(x) for x in state[s]]
                for j, env in enumerate(cenv):
                    rp, oc = env["ro"]
                    o_scr[env["rsl"], env["lsl"]] = _unstack_heads(_dot_tb(rp, sb[j]) + oc, lane_lo)
                for j, env in enumerate(cenv):
                    mk, nn = env["mn"]
                    pc = p_scr[(c + 1) * ROWS - 1:(c + 1) * ROWS, env["lsl"]]
                    state[s][j] = (state[s][j] + _dot(sb[j], mk) + nn) * pc
        for s in range(nb):
            for j in range(PAIRS):
                s_scr[s, j] = state[s][j]
    else:
        for env in envs:
            env["sb"] = [_bf(s_scr[q, env["j"]]) for q in range(ns)]

        def f_h(a2, r2, sb):
            pieces = [_dot_tb(jnp.concatenate([_seq_rows(a2, q, ct), _seq_rows(r2, q, ct)], axis=0), sb[q])
                      for q in range(ns)]
            return (_from_seq_rows([x[0:2 * ct] for x in pieces], ct),
                    _from_seq_rows([x[2 * ct:] for x in pieces], ct))

        def f_ds(u2, vst, b2, k2):
            vst32 = vst.astype(F32)
            out = []
            for q in range(ns):
                uv = jnp.concatenate([_seq_rows(u2, q, ct), _seq_rows(vst32, q, ct)], axis=0)
                bk = jnp.concatenate([_seq_rows(b2, q, ct), _seq_rows(k2, q, ct)], axis=0)
                out.append(_dot_ta(_bf(_pad_rows(uv, LANES)), _pad_rows(bk, LANES)) * same_head())
            return out

        steps += [
            ("h", ("a2", "r2", "sb"), f_h),
            ("u2", ("tb", "h", "lvark"), lambda tb, h, lvark: _dot(tb, _bf(h[0] + lvark[0:N2]))),
            ("o2", ("arb", "u2", "h", "lvark"),
             lambda arb, u2, h, lvark: h[1] + _dot(arb, _bf(u2)) + lvark[N2:]),
            ("ds", ("u2", "vst", "b2", "k2"), f_ds),
        ]
        _emit_by_level(steps, envs)
        for env in envs:
            o_scr[env["rsl"], env["lsl"]] = _unstack_heads(env["o2"], lane_lo)
            for q in range(ns):
                pc = p_scr[(q + 1) * ct - 1:(q + 1) * ct, env["lsl"]]
                s_scr[q, env["j"]] = (s_scr[q, env["j"]] + env["ds"][q]) * pc

    o = o_scr[...]
    y = (_head_norm(o, lng_ref[...], lnb_ref[...], A_GN_EPS, ones_bd) + bonus) * g
    ya_ref[...] = _bf(y).reshape(nb, tt, WIDTH)

    @pl.when(ti == pl.num_programs(1) - 1)
    def _():
        _store_state(sout_ref, s_scr, nb)


def _row_spec(shape):
    return pl.BlockSpec(shape, lambda b, t: (0,) * len(shape))


def _check_mixer_blocking(bsz, tlen, nb, tt):
    ct = min(ROWS, tt)
    rows = nb * tt
    assert rows % ROWS == 0 and ROWS % ct == 0 and bsz % nb == 0 and tlen % tt == 0
    assert tt % ROWS == 0 or (tt == tlen and rows == ROWS)
    return ct, rows


def _rwkv(pa3, shift0, s0, prm, nb, tt):
    bsz, tlen, _ = pa3.shape
    ct, rows = _check_mixer_blocking(bsz, tlen, nb, tt)
    span = tt if tt % ROWS == 0 else rows
    grid = (bsz // nb, tlen // tt)
    state_spec = pl.BlockSpec((nb, HEADS, HEAD_DIM, HEAD_DIM), lambda b, t: (b, 0, 0, 0))
    vec = lambda n: _row_spec((1, n))
    in_specs = [
        pl.BlockSpec((nb, tt, A_PROJ), lambda b, t: (b, t, 0)),
        pl.BlockSpec((nb, 1, A_PROJ), lambda b, t: (b, 0, 0)),
        state_spec,
        vec(A_PROJ), vec(WIDTH), _row_spec((LANES, WIDTH)), vec(WIDTH), _row_spec((LANES, WIDTH)),
        _row_spec((GATE_LORA, WIDTH)), vec(WIDTH), vec(WIDTH), vec(WIDTH), vec(WIDTH), vec(WIDTH),
    ]
    out_specs = [
        pl.BlockSpec((nb, tt, WIDTH), lambda b, t: (b, t, 0)),
        state_spec,
    ]
    scratch = [
        pltpu.VMEM((nb, PAIRS, LANES, LANES), F32),
        pltpu.VMEM((nb, 1, A_PROJ), F32),
        pltpu.VMEM((rows, WIDTH), BF16),
        pltpu.VMEM((rows, WIDTH), BF16),
        pltpu.VMEM((rows, WIDTH), BF16),
        pltpu.VMEM((rows, WIDTH), BF16),
        pltpu.VMEM((rows, WIDTH), BF16),
        pltpu.VMEM((rows, WIDTH), F32),
        pltpu.VMEM((rows, WIDTH), F32),
        pltpu.VMEM((2 * N2, 2 * N2), F32),
        pltpu.VMEM((N2, N2), F32),
        pltpu.VMEM((N2, N2), F32),
        pltpu.VMEM((span, span), BF16),
    ]
    return pl.pallas_call(
        functools.partial(_rwkv_kernel, nb, tt, ct),
        grid=grid,
        in_specs=in_specs,
        out_specs=out_specs,
        out_shape=[
            jax.ShapeDtypeStruct((bsz, tlen, WIDTH), BF16),
            jax.ShapeDtypeStruct(s0.shape, F32),
        ],
        scratch_shapes=scratch,
        compiler_params=pltpu.CompilerParams(
            dimension_semantics=("arbitrary", "arbitrary"), vmem_limit_bytes=VMEM_LIMIT),
        name="rwkv7_mixer",
    )(pa3, shift0.reshape(bsz, 1, A_PROJ), s0, *prm)


def _ret_kernel(nb, tt, ct, pos0, pb_ref, s0_ref, invf_ref, gng_ref, gnb_ref,
                yb_ref, sout_ref,
                s_scr, q_scr, k_scr, v_scr, o_scr, dec_scr, head_scr, trig_scr):
    rows = nb * tt
    nchunk = rows // ROWS
    ns = ROWS // ct
    ti = pl.program_id(1)

    @pl.when((pl.program_id(0) == 0) & (ti == 0))
    def _():
        _, incl, _, same_head = _stream_masks(ct, N2)
        head_scr[...] = jnp.where(same_head, 1.0, 0.0)
        ri = _iota2((N2, N2), 0)
        ci = _iota2((N2, N2), 1)
        tpos = _mod2(ri, ct).astype(F32)
        tdiff = (_mod2(ri, ct) - _mod2(ci, ct)).astype(F32)
        for j in range(PAIRS):
            head = (2 * j + _div2(ri, ROWS)).astype(F32)
            lg = jnp.log1p(-jnp.exp2(-5.0 - head))
            dec_scr[j, 0] = jnp.where(incl, jnp.exp(lg * jnp.maximum(tdiff, 0.0)), 0.0)
            dec_scr[j, 1] = jnp.exp(lg * (tpos + 1.0))
            dec_scr[j, 2] = jnp.exp(lg * (ct - 1.0 - tpos))
            dec_scr[j, 3] = jnp.exp(lg * ct)
        off = _mod2(_iota2((rows, LANES), 0), tt).astype(F32) * invf_ref[...]
        trig_scr[0] = jnp.cos(off)
        trig_scr[1] = jnp.sin(off)

    @pl.when(ti == 0)
    def _():
        _load_state(s_scr, s0_ref, nb)

    ones_bd = _head_ones()
    pb = pb_ref[...].reshape(rows, B_PROJ)
    q = pb[:, 0:WIDTH]
    k = pb[:, WIDTH:2 * WIDTH]
    gate = pb[:, 3 * WIDTH:]

    base = jnp.zeros((SUBLANES, LANES), F32) + (pos0 + ti * tt).astype(F32)
    ang = base * invf_ref[...]
    cos_a, sin_a = jnp.cos(ang)[0:1], jnp.sin(ang)[0:1]
    cos_b, sin_b = trig_scr[0], trig_scr[1]
    cos = jnp.concatenate([cos_a * cos_b - sin_a * sin_b] * PAIRS, axis=1)
    sin = jnp.concatenate([sin_a * cos_b + cos_a * sin_b] * PAIRS, axis=1)
    first_half = _mod2(_iota2((rows, WIDTH), 1), HEAD_DIM) < (HEAD_DIM // 2)
    sin = jnp.where(first_half, -sin, sin)

    pr = _iota2((LANES, LANES), 0)
    pc = _iota2((LANES, LANES), 1)
    swap = jnp.where(pr == lax.bitwise_xor(pc, HEAD_DIM // 2), 1.0, 0.0).astype(BF16)

    def rope(x):
        xb = _bf(x)
        partner = jnp.concatenate([_dot(xb[:, LANES * j:LANES * (j + 1)], swap) for j in range(PAIRS)], axis=1)
        return x * cos + partner * sin

    q_scr[...] = _bf(rope(q))
    k_scr[...] = rope(k) * (HEAD_DIM ** -0.5)
    v_scr[...] = _bf(pb[:, 2 * WIDTH:3 * WIDTH])

    lane_lo = _iota2((ROWS, LANES), 1) < HEAD_DIM
    same_head = lambda: head_scr[...]

    envs = []
    for c in range(nchunk):
        for j in range(PAIRS):
            rsl = slice(c * ROWS, (c + 1) * ROWS)
            lsl = slice(LANES * j, LANES * (j + 1))
            qj, kj, vj = q_scr[rsl, lsl], k_scr[rsl, lsl], v_scr[rsl, lsl]
            kjb = _bf(kj)
            envs.append({
                "c": c, "j": j, "rsl": rsl, "lsl": lsl,
                "q2": _bdiag(qj, lane_lo),
                "kst": jnp.concatenate([kjb, kjb], axis=0),
                "k2d": _bdiag(kj, lane_lo) * dec_scr[j, 2],
                "vst": jnp.concatenate([vj, vj], axis=0),
                "dmask": lambda j=j: dec_scr[j, 0],
            })

    def f_ds(k2d, vst):
        return [_dot_ta(_bf(_pad_rows(_seq_rows(k2d, s, ct), LANES)), _pad_rows(_seq_rows(vst, s, ct), LANES))
                * same_head() for s in range(ns)]

    steps = [
        ("sc", ("q2", "kst", "dmask"), lambda q2, kst, dmask: _bf(_dot_tb(q2, kst) * dmask())),
        ("inner", ("sc", "vst"), _dot),
        ("ds", ("k2d", "vst"), f_ds),
    ]
    _emit_by_level(steps, envs)

    if ns == 1:
        per_seq = tt // ROWS
        for s in range(nb):
            state = [s_scr[s, j] for j in range(PAIRS)]
            for c in range(s * per_seq, (s + 1) * per_seq):
                for j in range(PAIRS):
                    env = envs[c * PAIRS + j]
                    env["sb"] = [_bf(state[j])]
                    state[j] = state[j] * dec_scr[j, 3] + env["ds"][0]
            for j in range(PAIRS):
                s_scr[s, j] = state[j]
    else:
        for env in envs:
            j = env["j"]
            env["sb"] = [_bf(s_scr[s, j]) for s in range(ns)]
            for s in range(ns):
                s_scr[s, j] = s_scr[s, j] * dec_scr[j, 3] + env["ds"][s]

    def f_cross(q2, sb):
        return _from_seq_rows([_dot(_seq_rows(q2, s, ct), sb[s]) for s in range(ns)], ct)

    _emit_by_level([("cross", ("q2", "sb"), f_cross)], envs)
    for env in envs:
        o2 = env["inner"] + env["cross"] * dec_scr[env["j"], 1]
        o_scr[env["rsl"], env["lsl"]] = _unstack_heads(o2, lane_lo)

    o = o_scr[...]
    y = jax.nn.silu(gate) * _head_norm(o, gng_ref[...], gnb_ref[...], B_GN_EPS, ones_bd)
    yb_ref[...] = _bf(y).reshape(nb, tt, WIDTH)

    @pl.when(ti == pl.num_programs(1) - 1)
    def _():
        _store_state(sout_ref, s_scr, nb)


def _retention(pb3, s0, invf, gn_g, gn_b, nb, tt, pos0):
    bsz, tlen, _ = pb3.shape
    ct, rows = _check_mixer_blocking(bsz, tlen, nb, tt)
    grid = (bsz // nb, tlen // tt)
    state_spec = pl.BlockSpec((nb, HEADS, HEAD_DIM, HEAD_DIM), lambda b, t: (b, 0, 0, 0))
    in_specs = [
        pl.BlockSpec((nb, tt, B_PROJ), lambda b, t: (b, t, 0)),
        state_spec,
        _row_spec((1, LANES)), _row_spec((1, WIDTH)), _row_spec((1, WIDTH)),
    ]
    out_specs = [
        pl.BlockSpec((nb, tt, WIDTH), lambda b, t: (b, t, 0)),
        state_spec,
    ]
    scratch = [
        pltpu.VMEM((nb, PAIRS, LANES, LANES), F32),
        pltpu.VMEM((rows, WIDTH), BF16),
        pltpu.VMEM((rows, WIDTH), F32),
        pltpu.VMEM((rows, WIDTH), BF16),
        pltpu.VMEM((rows, WIDTH), F32),
        pltpu.VMEM((PAIRS, 4, N2, N2), F32),
        pltpu.VMEM((N2, N2), F32),
        pltpu.VMEM((2, rows, LANES), F32),
    ]
    return pl.pallas_call(
        functools.partial(_ret_kernel, nb, tt, ct, pos0),
        grid=grid,
        in_specs=in_specs,
        out_specs=out_specs,
        out_shape=[
            jax.ShapeDtypeStruct((bsz, tlen, WIDTH), BF16),
            jax.ShapeDtypeStruct(s0.shape, F32),
        ],
        scratch_shapes=scratch,
        compiler_params=pltpu.CompilerParams(
            dimension_semantics=("arbitrary", "arbitrary"), vmem_limit_bytes=VMEM_LIMIT),
        name="retention_mixer",
    )(pb3, s0, invf, gn_g, gn_b)


def _conv_kernel(nb, tt, nsplit, x_ref, buf_ref, win_ref, cw_ref, z_ref, bout_ref, halo_scr):
    rows = nb * tt
    sub = rows // nsplit
    ti = pl.program_id(1)

    @pl.when(ti == 0)
    def _():
        halo_scr[...] = buf_ref[...]

    proj = []
    for i in range(nsplit):
        xi = x_ref[0, i * sub:(i + 1) * sub, :] if nb == 1 else x_ref[...].reshape(rows, D_MODEL)
        xb = _bf(xi)
        bg = _dot(xb, win_ref[:, 0:D_MODEL])
        u = _dot(xb, win_ref[:, D_MODEL:2 * D_MODEL]) * _dot(xb, win_ref[:, 2 * D_MODEL:])
        proj.append((bg, u))

    cw = cw_ref[...]
    st = tt // nsplit if nb == 1 else tt
    t3 = _iota2((nb, st, D_MODEL), 1)
    h0 = halo_scr[:, 0:1, :]
    h1 = halo_scr[:, 1:2, :]
    for i, (bg, u) in enumerate(proj):
        u3 = u.reshape(nb, st, D_MODEL)
        prev1 = jnp.where(t3 == 0, h1, pltpu.roll(u, 1, axis=0).reshape(nb, st, D_MODEL))
        prev2 = jnp.where(t3 == 0, h0,
                          jnp.where(t3 == 1, h1, pltpu.roll(u, 2, axis=0).reshape(nb, st, D_MODEL)))
        conv = prev2 * cw[0:1, :] + prev1 * cw[1:2, :] + u3 * cw[2:3, :]
        z_ref[i * sub:(i + 1) * sub, :] = _bf(bg * conv.reshape(sub, D_MODEL))
        h0 = u3[:, st - 2:st - 1, :]
        h1 = u3[:, st - 1:st, :]
    halo_scr[:, 0:1, :] = h0
    halo_scr[:, 1:2, :] = h1

    @pl.when(ti == pl.num_programs(1) - 1)
    def _():
        bout_ref[...] = halo_scr[...]


def _conv_mixer(x3, buf0, win, cw, nb, tt, nsplit):
    bsz, tlen, _ = x3.shape
    rows = nb * tt
    assert bsz % nb == 0 and tlen % tt == 0
    assert nsplit == 1 or nb == 1
    assert tt % nsplit == 0 and tt // nsplit >= CONV_W - 1
    grid = (bsz // nb, tlen // tt)
    buf_spec = pl.BlockSpec((nb, CONV_W - 1, D_MODEL), lambda b, t: (b, 0, 0))
    return pl.pallas_call(
        functools.partial(_conv_kernel, nb, tt, nsplit),
        grid=grid,
        in_specs=[
            pl.BlockSpec((nb, tt, D_MODEL), lambda b, t: (b, t, 0)),
            buf_spec,
            pl.BlockSpec(win.shape, lambda b, t: (0, 0), pipeline_mode=pl.Buffered(1)),
            _row_spec((CONV_W, D_MODEL)),
        ],
        out_specs=[
            pl.BlockSpec((rows, D_MODEL), lambda b, t: (b * (tlen // tt) + t, 0)),
            buf_spec,
        ],
        out_shape=[
            jax.ShapeDtypeStruct((bsz * tlen, D_MODEL), BF16),
            jax.ShapeDtypeStruct(buf0.shape, F32),
        ],
        scratch_shapes=[pltpu.VMEM((nb, CONV_W - 1, D_MODEL), F32)],
        compiler_params=pltpu.CompilerParams(
            dimension_semantics=("arbitrary", "arbitrary"), vmem_limit_bytes=VMEM_LIMIT),
        name="conv_mixer",
    )(x3, buf0, win, cw)


FF_CHUNK = 1024


def _post_kernel(npieces, nsplit, *refs):
    x_ref = refs[0]
    y_refs = refs[1:1 + npieces]
    wout_ref, g1_ref, b1_ref, wup_ref, wdown_ref, g2_ref, b2_ref, out_ref = refs[1 + npieces:]
    sub = x_ref.shape[0] // nsplit

    def f_y(r):
        y = None
        off = 0
        for y_ref in y_refs:
            n = y_ref.shape[1]
            term = _dot(y_ref[r, :], wout_ref[off:off + n, :])
            y = term if y is None else y + term
            off += n
        return y

    def f_mlp(f):
        def step(x1b, acc):
            h = jnp.maximum(_dot(x1b, wup_ref[:, f * FF_CHUNK:(f + 1) * FF_CHUNK]), 0.0)
            term = _dot(_bf(h * h), wdown_ref[f * FF_CHUNK:(f + 1) * FF_CHUNK, :])
            return term if acc is None else acc + term
        return step

    def f_out(r, x1, acc):
        out_ref[r, :] = _layer_norm(ALPHA * x1 + acc, g2_ref[...], b2_ref[...])
        return None

    steps = [
        ("y", ("r",), f_y),
        ("x1", ("r", "y"), lambda r, y: _layer_norm(ALPHA * x_ref[r, :] + y, g1_ref[...], b1_ref[...])),
        ("x1b", ("x1",), _bf),
        ("acc-1", ("x1b",), lambda x1b: None),
    ]
    nff = D_FF // FF_CHUNK
    for f in range(nff):
        steps.append((f"acc{f}", ("x1b", f"acc{f - 1}"), f_mlp(f)))
    steps.append(("out", ("r", "x1", f"acc{nff - 1}"), f_out))
    envs = [{"r": slice(i * sub, (i + 1) * sub)} for i in range(nsplit)]
    _emit_by_level(steps, envs)


def _post(x2d, ys, wout, g1, b1, wup, wdown, g2, b2, layer, tm, nsplit):
    m = x2d.shape[0]
    assert m % tm == 0 and tm % nsplit == 0
    const = lambda i: (0, 0)
    resident = lambda a: pl.BlockSpec(a.shape, const, pipeline_mode=pl.Buffered(1))
    of_layer = lambda a: pl.BlockSpec((None,) + a.shape[1:], lambda i: (layer, 0, 0),
                                      pipeline_mode=pl.Buffered(1))
    vec = pl.BlockSpec((1, D_MODEL), const)
    in_specs = [pl.BlockSpec((tm, D_MODEL), lambda i: (i, 0))]
    in_specs += [pl.BlockSpec((tm, y.shape[1]), lambda i: (i, 0)) for y in ys]
    in_specs += [resident(wout), vec, vec, of_layer(wup), of_layer(wdown), vec, vec]
    return pl.pallas_call(
        functools.partial(_post_kernel, len(ys), nsplit),
        grid=(m // tm,),
        in_specs=in_specs,
        out_specs=pl.BlockSpec((tm, D_MODEL), lambda i: (i, 0)),
        out_shape=jax.ShapeDtypeStruct((m, D_MODEL), F32),
        compiler_params=pltpu.CompilerParams(
            dimension_semantics=("arbitrary",), vmem_limit_bytes=VMEM_LIMIT),
        name="post_block",
    )(x2d, *ys, wout, g1, b1, wup, wdown, g2, b2)


def _trunk(x, st_shift, st_wkv, st_ret, st_conv, pos0, blk, w):
    bsz, tlen, _ = x.shape
    m = bsz * tlen
    x2d = x.reshape(m, D_MODEL)
    row = lambda a: a.reshape(1, -1)

    pa, pb = _inproj(x2d, w["w_in_ab"], blk["tm_in"])
    pa3 = pa.reshape(bsz, tlen, A_PROJ)
    rw_prm = (row(w["mu_a"]), row(w["w0"]), w["w2p"], row(w["a0"]), w["a2p"], w["g2"],
              row(w["k_k"]), row(w["k_a"]), row(w["r_k"]), row(w["lnx_g"]), row(w["lnx_b"]))
    ya, wkv1 = _rwkv(pa3, st_shift, st_wkv, rw_prm, blk["nb_mix"], blk["tt_mix"])
    yb, ret1 = _retention(pb.reshape(bsz, tlen, B_PROJ), st_ret, w["invf"], row(w["gn_g"]),
                          row(w["gn_b"]), blk["nb_mix"], blk["tt_mix"], pos0)
    shift1 = pa3[:, tlen - 1, :]
    x2d = _post(x2d, [ya.reshape(m, WIDTH), yb.reshape(m, WIDTH)], w["w_out_ab"], row(w["ln1_g"][0]), row(w["ln1_b"][0]),
                w["w_up"], w["w_down"], row(w["ln2_g"][0]), row(w["ln2_b"][0]),
                0, blk["tm_post"], blk["split"])

    z, conv1 = _conv_mixer(x2d.reshape(bsz, tlen, D_MODEL), st_conv, w["w_in_conv"], w["conv_w"],
                           blk["nb_conv"], blk["tt_conv"], blk["split_conv"])
    x2d = _post(x2d, [z], w["w_out_conv"], row(w["ln1_g"][1]), row(w["ln1_b"][1]),
                w["w_up"], w["w_down"], row(w["ln2_g"][1]), row(w["ln2_b"][1]),
                1, blk["tm_post"], blk["split"])
    return x2d.reshape(bsz, tlen, D_MODEL), shift1[None], wkv1[None], ret1[None], conv1[None]


def kernel(x_prompt, x_sample, state_shift, state_wkv, state_ret, state_conv, w_in_ab, mu_a, w0, w2, a0, a2,
           g2, k_k, k_a, r_k, lnx_g, lnx_b, gn_g, gn_b, w_out_ab, w_in_conv, conv_w, w_out_conv,
           ln1_g, ln1_b, ln2_g, ln2_b, w_up, w_down):
    bp, tp, _ = x_prompt.shape
    half = HEAD_DIM // 2
    inv = ROPE_BASE ** (-jnp.arange(half, dtype=F32) / half)
    zpad = jnp.zeros((LANES - DECAY_LORA, WIDTH), F32)
    w = {
        "w_in_ab": _bf(w_in_ab[0]),
        "mu_a": mu_a[0], "w0": w0[0], "a0": a0[0],
        "w2p": _bf(jnp.concatenate([w2[0], zpad], axis=0)),
        "a2p": _bf(jnp.concatenate([zpad, a2[0]], axis=0)),
        "g2": _bf(g2[0]), "k_k": k_k[0], "k_a": k_a[0], "r_k": r_k[0],
        "lnx_g": lnx_g[0], "lnx_b": lnx_b[0], "gn_g": gn_g[0], "gn_b": gn_b[0],
        "invf": jnp.tile(inv, LANES // half).reshape(1, LANES),
        "w_out_ab": _bf(w_out_ab[0]), "w_in_conv": _bf(w_in_conv[0]), "conv_w": conv_w[0],
        "w_out_conv": _bf(w_out_conv[0]),
        "ln1_g": ln1_g, "ln1_b": ln1_b, "ln2_g": ln2_g, "ln2_b": ln2_b,
        "w_up": _bf(w_up), "w_down": _bf(w_down),
    }
    dt = state_wkv.dtype
    z_shift = jnp.zeros((bp, A_PROJ), dt)
    z_state = jnp.zeros((bp, HEADS, HEAD_DIM, HEAD_DIM), dt)
    z_conv = jnp.zeros((bp, CONV_W - 1, D_MODEL), dt)
    blk_p = {"tm_in": 512, "tm_post": 1024, "split": 4, "nb_mix": 2, "tt_mix": 256,
             "nb_conv": 1, "tt_conv": 1024, "split_conv": 2}
    y_p, p_shift, p_wkv, p_ret, p_conv = _trunk(x_prompt, z_shift, z_state, z_state, z_conv, 0, blk_p, w)
    ts = x_sample.shape[1]
    blk_s = {"tm_in": 512, "tm_post": 1024, "split": 2, "nb_mix": ROWS // ts, "tt_mix": ts,
             "nb_conv": 256 // ts, "tt_conv": ts, "split_conv": 1}
    y_s, s_shift, s_wkv, s_ret, s_conv = _trunk(x_sample, state_shift[0], state_wkv[0], state_ret[0],
                                                state_conv[0], PAST_LEN, blk_s, w)
    return (y_p, y_s, p_shift, p_wkv, p_ret, p_conv, s_shift, s_wkv, s_ret, s_conv)
```

```python
import functools

import jax
import jax.numpy as jnp
from jax import lax
from jax.experimental import pallas as pl
from jax.experimental.pallas import tpu as pltpu

F32 = jnp.float32
BF16 = jnp.bfloat16

D_MODEL = 1024
DEPTH = 2
PAST_LEN = 16384
HEADS = 8
HEAD_DIM = 64
WIDTH = HEADS * HEAD_DIM
DECAY_LORA = 64
AAA_LORA = 64
GATE_LORA = 128
A_PROJ = 3 * WIDTH + DECAY_LORA + AAA_LORA + GATE_LORA
B_PROJ = 4 * WIDTH
A_GN_EPS = 64e-5
B_GN_EPS = 1e-5
ROPE_BASE = 10000.0
CONV_W = 3
D_FF = 4 * D_MODEL
LN_EPS = 1e-5
ALPHA = (2.0 * DEPTH) ** 0.25

LANES = 128
SUBLANES = 8
PAIRS = WIDTH // LANES
ROWS = 64
N2 = 2 * ROWS
VMEM_LIMIT = 56 * 1024 * 1024


def _bf(x):
    return x.astype(BF16)


def _dot(a, b):
    return jnp.dot(a, b, preferred_element_type=F32)


def _dot_tb(a, b):
    return lax.dot_general(a, b, (((1,), (1,)), ((), ())), preferred_element_type=F32)


def _dot_ta(a, b):
    return lax.dot_general(a, b, (((0,), (0,)), ((), ())), preferred_element_type=F32)


def _dot_split_lhs(m, x, parts):
    acc = None
    rem = x
    for i in range(parts):
        hi = _bf(rem)
        term = _dot(m, hi)
        acc = term if acc is None else acc + term
        if i + 1 < parts:
            rem = rem - hi.astype(F32)
    return acc


def _iota2(shape, dim):
    return lax.broadcasted_iota(jnp.int32, shape, dim)


def _div2(x, n):
    assert n & (n - 1) == 0
    return lax.shift_right_logical(x, n.bit_length() - 1)


def _mod2(x, n):
    assert n & (n - 1) == 0
    return lax.bitwise_and(x, n - 1)


def _head_ones():
    ri = _iota2((LANES, LANES), 0)
    ci = _iota2((LANES, LANES), 1)
    return jnp.where(_div2(ri, HEAD_DIM) == _div2(ci, HEAD_DIM), 1.0, 0.0).astype(BF16)


def _headsum(x, ones_bd):
    xb = _bf(x)
    return jnp.concatenate([_dot(xb[:, LANES * j:LANES * (j + 1)], ones_bd) for j in range(PAIRS)], axis=1)


def _head_norm(o, g, b, eps, ones_bd):
    mu = _headsum(o, ones_bd) * (1.0 / HEAD_DIM)
    d = o - mu
    var = _headsum(d * d, ones_bd) * (1.0 / HEAD_DIM)
    return d * lax.rsqrt(var + eps) * g + b


def _layer_norm(z, g, b):
    mu = jnp.mean(z, axis=-1, keepdims=True)
    d = z - mu
    var = jnp.mean(d * d, axis=-1, keepdims=True)
    return d * lax.rsqrt(var + LN_EPS) * g + b


def _bdiag(x, lane_lo):
    zero = jnp.zeros_like(x)
    return jnp.concatenate([jnp.where(lane_lo, x, zero), jnp.where(lane_lo, zero, x)], axis=0)


def _unstack_heads(x, lane_lo):
    return jnp.where(lane_lo, x[0:ROWS], x[ROWS:])


def _stream_masks(ct, width):
    ri = _iota2((N2, width), 0)
    ci = _mod2(_iota2((N2, width), 1), N2)
    same = _div2(ri, ct) == _div2(ci, ct)
    return same & (ci < ri), same & (ci <= ri), ri == ci, _div2(ri, ROWS) == _div2(ci, ROWS)


def _load_state(s_scr, s0_ref, nb):
    z = jnp.zeros((HEAD_DIM, HEAD_DIM), F32)
    for s in range(nb):
        for j in range(PAIRS):
            top = jnp.concatenate([s0_ref[s, 2 * j], z], axis=1)
            bot = jnp.concatenate([z, s0_ref[s, 2 * j + 1]], axis=1)
            s_scr[s, j] = jnp.concatenate([top, bot], axis=0)


def _store_state(sout_ref, s_scr, nb):
    for s in range(nb):
        for j in range(PAIRS):
            s2 = s_scr[s, j]
            sout_ref[s, 2 * j] = s2[0:HEAD_DIM, 0:HEAD_DIM]
            sout_ref[s, 2 * j + 1] = s2[HEAD_DIM:, HEAD_DIM:]


def _seq_rows(x, q, ct):
    if ct == ROWS:
        return x
    return jnp.concatenate([x[q * ct:(q + 1) * ct], x[ROWS + q * ct:ROWS + (q + 1) * ct]], axis=0)


def _from_seq_rows(pieces, ct):
    if ct == ROWS:
        return pieces[0]
    return jnp.concatenate([p[0:ct] for p in pieces] + [p[ct:2 * ct] for p in pieces], axis=0)


def _pad_rows(x, rows):
    if x.shape[0] == rows:
        return x
    return jnp.concatenate([x, jnp.zeros((rows - x.shape[0], x.shape[1]), x.dtype)], axis=0)


def _emit_by_level(steps, envs):
    level = {}
    for name, deps, _ in steps:
        level[name] = 1 + max([level.get(d, 0) for d in deps], default=0)
    for lv in sorted(set(level.values())):
        for name, deps, fn in steps:
            if level[name] == lv:
                for env in envs:
                    env[name] = fn(*[env[d] for d in deps])


def _inverse_steps(nfac, eye):
    def advance(last):
        def fn(q, p):
            pb = _bf(p)
            if last:
                return p + _dot(q, pb), None
            x = _dot(q, jnp.concatenate([pb, q], axis=1))
            return p + x[:, 0:N2], _bf(x[:, N2:])
        return fn

    steps = [
        ("q0", ("l",), _bf),
        ("p1", ("l",), lambda l: l + eye()),
        ("q1", ("q0",), lambda q: _bf(_dot(q, q))),
    ]
    for i in range(1, nfac):
        steps.append((f"s{i}", (f"q{i}", f"p{i}"), advance(i == nfac - 1)))
        steps.append((f"p{i + 1}", (f"s{i}",), lambda s: s[0]))
        steps.append((f"q{i + 1}", (f"s{i}",), lambda s: s[1]))
    steps.append(("tb", (f"p{nfac}",), _bf))
    return steps


def _rwkv_score_steps(sc_mask):
    return [
        ("scm", ("a2", "r2", "bk"),
         lambda a2, r2, bk: _dot_tb(jnp.concatenate([a2, r2], axis=0), bk) * sc_mask()),
        ("l", ("scm",), lambda scm: scm[0:N2, 0:N2]),
        ("lkb", ("scm",), lambda scm: _bf(scm[:, N2:])),
        ("arb", ("scm",), lambda scm: _bf(scm[N2:, 0:N2])),
        ("lvark", ("lkb", "vst"), _dot),
    ]


def _rwkv_carried_steps(same_head):
    def f_ro(arb, tal, r2, lvark):
        x = _dot(arb, _bf(tal))
        return _bf(r2.astype(F32) + x[:, 0:N2]), x[:, N2:] + lvark[N2:]

    def f_mn(tal, b2, k2, vst):
        mm = _dot_ta(_bf(tal), b2)
        n2 = _dot_ta(vst, k2)
        return _bf(mm[0:N2]), (mm[N2:] + n2) * same_head()

    return [
        ("tal", ("tb", "a2", "lvark"),
         lambda tb, a2, lvark: _dot(tb, jnp.concatenate([a2, _bf(lvark[0:N2])], axis=1))),
        ("ro", ("arb", "tal", "r2", "lvark"), f_ro),
        ("mn", ("tal", "b2", "k2", "vst"), f_mn),
    ]


def _inproj_kernel(x_ref, w_ref, pa_ref, pb_ref):
    xb = _bf(x_ref[...])
    pa_ref[...] = _dot(xb, w_ref[:, 0:A_PROJ])
    pb_ref[...] = _dot(xb, w_ref[:, A_PROJ:])


def _inproj(x2d, w, tm):
    m = x2d.shape[0]
    return pl.pallas_call(
        _inproj_kernel,
        grid=(m // tm,),
        in_specs=[
            pl.BlockSpec((tm, D_MODEL), lambda i: (i, 0)),
            pl.BlockSpec(w.shape, lambda i: (0, 0), pipeline_mode=pl.Buffered(1)),
        ],
        out_specs=[
            pl.BlockSpec((tm, A_PROJ), lambda i: (i, 0)),
            pl.BlockSpec((tm, B_PROJ), lambda i: (i, 0)),
        ],
        out_shape=[
            jax.ShapeDtypeStruct((m, A_PROJ), F32),
            jax.ShapeDtypeStruct((m, B_PROJ), F32),
        ],
        compiler_params=pltpu.CompilerParams(
            dimension_semantics=("arbitrary",), vmem_limit_bytes=VMEM_LIMIT),
        name="inproj_ab",
    )(x2d, w)


def _rwkv_kernel(nb, tt, ct, pa_ref, shift_ref, s0_ref, mu_ref, w0_ref, w2p_ref, a0_ref, a2p_ref,
                 g2_ref, kk_ref, ka_ref, rk_ref, lng_ref, lnb_ref,
                 ya_ref, sout_ref,
                 s_scr, prev_scr, at_scr, rt_scr, bt_scr, kt_scr, v_scr, p_scr, o_scr,
                 scmask_scr, eye_scr, head_scr, lcum_scr):
    rows = nb * tt
    nchunk = rows // ROWS
    ns = ROWS // ct
    ti = pl.program_id(1)

    @pl.when((pl.program_id(0) == 0) & (ti == 0))
    def _():
        strict, incl, eye, same_head = _stream_masks(ct, 2 * N2)
        one = lambda m: jnp.where(m, 1.0, 0.0)
        scmask_scr[...] = jnp.concatenate([one(strict), one(incl)], axis=0)
        eye_scr[...] = one(eye)[:, 0:N2]
        head_scr[...] = one(same_head)[:, 0:N2]
        ri = _iota2(lcum_scr.shape, 0)
        ci = _iota2(lcum_scr.shape, 1)
        lcum_scr[...] = jnp.where((_div2(ri, ct) == _div2(ci, ct)) & (ci <= ri), 1.0, 0.0).astype(BF16)

    @pl.when(ti == 0)
    def _():
        _load_state(s_scr, s0_ref, nb)
        prev_scr[...] = shift_ref[...]

    ones_bd = _head_ones()

    p3 = pa_ref[...]
    p2 = p3.reshape(rows, A_PROJ)
    t3 = _iota2((nb, tt, A_PROJ), 1)
    rolled = pltpu.roll(p2, 1, axis=0).reshape(nb, tt, A_PROJ)
    pprev = jnp.where(t3 == 0, prev_scr[...], rolled).reshape(rows, A_PROJ)
    prev_scr[...] = p3[:, tt - 1:tt, :]
    m = p2 + (pprev - p2) * mu_ref[...]
    r = m[:, 0:WIDTH]
    k = m[:, WIDTH:2 * WIDTH]
    v = m[:, 2 * WIDTH:3 * WIDTH]
    wa = m[:, 3 * WIDTH:3 * WIDTH + LANES]
    gd = m[:, 3 * WIDTH + LANES:A_PROJ]
    z = -(w0_ref[...] + _dot(_bf(jnp.tanh(wa)), w2p_ref[...]))
    softplus = jnp.maximum(z, 0.0) + jnp.log(1.0 + jnp.exp(-jnp.abs(z)))
    lw = -jnp.exp(-softplus - 0.5)
    a = jax.nn.sigmoid(a0_ref[...] + _dot(_bf(wa), a2p_ref[...]))
    g = _dot(_bf(jax.nn.sigmoid(gd)), g2_ref[...])
    kk = k * kk_ref[...]
    k = k * (1.0 + (a - 1.0) * ka_ref[...])
    kk = kk * lax.rsqrt(jnp.maximum(_headsum(kk * kk, ones_bd), 1e-24))
    bonus = _headsum(r * k * rk_ref[...], ones_bd) * v

    span = lcum_scr.shape[0]
    cs = jnp.concatenate([_dot_split_lhs(lcum_scr[...], lw[i * span:(i + 1) * span], 2)
                          for i in range(rows // span)], axis=0)
    pinv = jnp.exp(-cs)
    p_scr[...] = jnp.exp(cs)
    at_scr[...] = _bf(-kk * jnp.exp(cs - lw))
    rt_scr[...] = _bf(r * p_scr[...])
    bt_scr[...] = _bf(kk * a * pinv)
    kt_scr[...] = _bf(k * pinv)
    v_scr[...] = _bf(v)

    lane_lo = _iota2((ROWS, LANES), 1) < HEAD_DIM
    sc_mask = lambda: scmask_scr[...]
    eye = lambda: eye_scr[...]
    same_head = lambda: head_scr[...]
    nfac = ct.bit_length() - 1

    envs = []
    for c in range(nchunk):
        for j in range(PAIRS):
            rsl = slice(c * ROWS, (c + 1) * ROWS)
            lsl = slice(LANES * j, LANES * (j + 1))
            aj, rj, bj, kj, vj = (s[rsl, lsl] for s in (at_scr, rt_scr, bt_scr, kt_scr, v_scr))
            envs.append({
                "c": c, "j": j, "rsl": rsl, "lsl": lsl,
                "a2": _bdiag(aj, lane_lo), "r2": _bdiag(rj, lane_lo),
                "b2": _bdiag(bj, lane_lo), "k2": _bdiag(kj, lane_lo),
                "bk": jnp.concatenate([bj, bj, kj, kj], axis=0),
                "vst": jnp.concatenate([vj, vj], axis=0),
            })

    steps = _rwkv_score_steps(sc_mask) + _inverse_steps(nfac, eye)
    if ns == 1:
        _emit_by_level(steps + _rwkv_carried_steps(same_head), envs)
        per_seq = tt // ROWS
        state = [[s_scr[s, j] for j in range(PAIRS)] for s in range(nb)]
        for cpos in range(per_seq):
            for s in range(nb):
                c = s * per_seq + cpos
                cenv = envs[c * PAIRS:(c + 1) * PAIRS]
                sb = [_bf(x) for x in state[s]]
                for j, env in enumerate(cenv):
                    rp, oc = env["ro"]
                    o_scr[env["rsl"], env["lsl"]] = _unstack_heads(_dot_tb(rp, sb[j]) + oc, lane_lo)
                for j, env in enumerate(cenv):
                    mk, nn = env["mn"]
                    pc = p_scr[(c + 1) * ROWS - 1:(c + 1) * ROWS, env["lsl"]]
                    state[s][j] = (state[s][j] + _dot(sb[j], mk) + nn) * pc
        for s in range(nb):
            for j in range(PAIRS):
                s_scr[s, j] = state[s][j]
    else:
        for env in envs:
            env["sb"] = [_bf(s_scr[env["c"] * ns + q, env["j"]]) for q in range(ns)]

        def f_h(a2, r2, sb):
            pieces = [_dot_tb(jnp.concatenate([_seq_rows(a2, q, ct), _seq_rows(r2, q, ct)], axis=0), sb[q])
                      for q in range(ns)]
            return (_from_seq_rows([x[0:2 * ct] for x in pieces], ct),
                    _from_seq_rows([x[2 * ct:] for x in pieces], ct))

        def f_ds(u2, vst, b2, k2):
            vst32 = vst.astype(F32)
            out = []
            for q in range(ns):
                uv = jnp.concatenate([_seq_rows(u2, q, ct), _seq_rows(vst32, q, ct)], axis=0)
                bk = jnp.concatenate([_seq_rows(b2, q, ct), _seq_rows(k2, q, ct)], axis=0)
                out.append(_dot_ta(_bf(_pad_rows(uv, LANES)), _pad_rows(bk, LANES)) * same_head())
            return out

        steps += [
            ("h", ("a2", "r2", "sb"), f_h),
            ("u2", ("tb", "h", "lvark"), lambda tb, h, lvark: _dot(tb, _bf(h[0] + lvark[0:N2]))),
            ("o2", ("arb", "u2", "h", "lvark"),
             lambda arb, u2, h, lvark: h[1] + _dot(arb, _bf(u2)) + lvark[N2:]),
            ("ds", ("u2", "vst", "b2", "k2"), f_ds),
        ]
        _emit_by_level(steps, envs)
        for env in envs:
            o_scr[env["rsl"], env["lsl"]] = _unstack_heads(env["o2"], lane_lo)
            for q in range(ns):
                seq = env["c"] * ns + q
                pc = p_scr[(seq + 1) * ct - 1:(seq + 1) * ct, env["lsl"]]
                s_scr[seq, env["j"]] = (s_scr[seq, env["j"]] + env["ds"][q]) * pc

    o = o_scr[...]
    y = (_head_norm(o, lng_ref[...], lnb_ref[...], A_GN_EPS, ones_bd) + bonus) * g
    ya_ref[...] = _bf(y).reshape(nb, tt, WIDTH)

    @pl.when(ti == pl.num_programs(1) - 1)
    def _():
        _store_state(sout_ref, s_scr, nb)


def _row_spec(shape):
    return pl.BlockSpec(shape, lambda b, t: (0,) * len(shape))


def _check_mixer_blocking(bsz, tlen, nb, tt):
    ct = min(ROWS, tt)
    rows = nb * tt
    assert rows % ROWS == 0 and ROWS % ct == 0 and bsz % nb == 0 and tlen % tt == 0
    assert tt % ROWS == 0 or tt == tlen
    return ct, rows


def _rwkv(pa3, shift0, s0, prm, nb, tt):
    bsz, tlen, _ = pa3.shape
    ct, rows = _check_mixer_blocking(bsz, tlen, nb, tt)
    span = tt if tt % ROWS == 0 else rows
    grid = (bsz // nb, tlen // tt)
    state_spec = pl.BlockSpec((nb, HEADS, HEAD_DIM, HEAD_DIM), lambda b, t: (b, 0, 0, 0))
    vec = lambda n: _row_spec((1, n))
    in_specs = [
        pl.BlockSpec((nb, tt, A_PROJ), lambda b, t: (b, t, 0)),
        pl.BlockSpec((nb, 1, A_PROJ), lambda b, t: (b, 0, 0)),
        state_spec,
        vec(A_PROJ), vec(WIDTH), _row_spec((LANES, WIDTH)), vec(WIDTH), _row_spec((LANES, WIDTH)),
        _row_spec((GATE_LORA, WIDTH)), vec(WIDTH), vec(WIDTH), vec(WIDTH), vec(WIDTH), vec(WIDTH),
    ]
    out_specs = [
        pl.BlockSpec((nb, tt, WIDTH), lambda b, t: (b, t, 0)),
        state_spec,
    ]
    scratch = [
        pltpu.VMEM((nb, PAIRS, LANES, LANES), F32),
        pltpu.VMEM((nb, 1, A_PROJ), F32),
        pltpu.VMEM((rows, WIDTH), BF16),
        pltpu.VMEM((rows, WIDTH), BF16),
        pltpu.VMEM((rows, WIDTH), BF16),
        pltpu.VMEM((rows, WIDTH), BF16),
        pltpu.VMEM((rows, WIDTH), BF16),
        pltpu.VMEM((rows, WIDTH), F32),
        pltpu.VMEM((rows, WIDTH), F32),
        pltpu.VMEM((2 * N2, 2 * N2), F32),
        pltpu.VMEM((N2, N2), F32),
        pltpu.VMEM((N2, N2), F32),
        pltpu.VMEM((span, span), BF16),
    ]
    return pl.pallas_call(
        functools.partial(_rwkv_kernel, nb, tt, ct),
        grid=grid,
        in_specs=in_specs,
        out_specs=out_specs,
        out_shape=[
            jax.ShapeDtypeStruct((bsz, tlen, WIDTH), BF16),
            jax.ShapeDtypeStruct(s0.shape, F32),
        ],
        scratch_shapes=scratch,
        compiler_params=pltpu.CompilerParams(
            dimension_semantics=("arbitrary", "arbitrary"), vmem_limit_bytes=VMEM_LIMIT),
        name="rwkv7_mixer",
    )(pa3, shift0.reshape(bsz, 1, A_PROJ), s0, *prm)


def _ret_kernel(nb, tt, ct, pos0, pb_ref, s0_ref, invf_ref, gng_ref, gnb_ref,
                yb_ref, sout_ref,
                s_scr, q_scr, k_scr, v_scr, o_scr, dec_scr, head_scr, trig_scr):
    rows = nb * tt
    nchunk = rows // ROWS
    ns = ROWS // ct
    ti = pl.program_id(1)

    @pl.when((pl.program_id(0) == 0) & (ti == 0))
    def _():
        _, incl, _, same_head = _stream_masks(ct, N2)
        head_scr[...] = jnp.where(same_head, 1.0, 0.0)
        ri = _iota2((N2, N2), 0)
        ci = _iota2((N2, N2), 1)
        tpos = _mod2(ri, ct).astype(F32)
        tdiff = (_mod2(ri, ct) - _mod2(ci, ct)).astype(F32)
        for j in range(PAIRS):
            head = (2 * j + _div2(ri, ROWS)).astype(F32)
            lg = jnp.log1p(-jnp.exp2(-5.0 - head))
            dec_scr[j, 0] = jnp.where(incl, jnp.exp(lg * jnp.maximum(tdiff, 0.0)), 0.0)
            dec_scr[j, 1] = jnp.exp(lg * (tpos + 1.0))
            dec_scr[j, 2] = jnp.exp(lg * (ct - 1.0 - tpos))
            dec_scr[j, 3] = jnp.exp(lg * ct)
        off = _mod2(_iota2((rows, LANES), 0), tt).astype(F32) * invf_ref[...]
        trig_scr[0] = jnp.cos(off)
        trig_scr[1] = jnp.sin(off)

    @pl.when(ti == 0)
    def _():
        _load_state(s_scr, s0_ref, nb)

    ones_bd = _head_ones()
    pb = pb_ref[...].reshape(rows, B_PROJ)
    q = pb[:, 0:WIDTH]
    k = pb[:, WIDTH:2 * WIDTH]
    gate = pb[:, 3 * WIDTH:]

    base = jnp.zeros((SUBLANES, LANES), F32) + (pos0 + ti * tt).astype(F32)
    ang = base * invf_ref[...]
    cos_a, sin_a = jnp.cos(ang)[0:1], jnp.sin(ang)[0:1]
    cos_b, sin_b = trig_scr[0], trig_scr[1]
    cos = jnp.concatenate([cos_a * cos_b - sin_a * sin_b] * PAIRS, axis=1)
    sin = jnp.concatenate([sin_a * cos_b + cos_a * sin_b] * PAIRS, axis=1)
    first_half = _mod2(_iota2((rows, WIDTH), 1), HEAD_DIM) < (HEAD_DIM // 2)
    sin = jnp.where(first_half, -sin, sin)

    pr = _iota2((LANES, LANES), 0)
    pc = _iota2((LANES, LANES), 1)
    swap = jnp.where(pr == lax.bitwise_xor(pc, HEAD_DIM // 2), 1.0, 0.0).astype(BF16)

    def rope(x):
        xb = _bf(x)
        partner = jnp.concatenate([_dot(xb[:, LANES * j:LANES * (j + 1)], swap) for j in range(PAIRS)], axis=1)
        return x * cos + partner * sin

    q_scr[...] = _bf(rope(q))
    k_scr[...] = rope(k) * (HEAD_DIM ** -0.5)
    v_scr[...] = _bf(pb[:, 2 * WIDTH:3 * WIDTH])

    lane_lo = _iota2((ROWS, LANES), 1) < HEAD_DIM
    same_head = lambda: head_scr[...]

    envs = []
    for c in range(nchunk):
        for j in range(PAIRS):
            rsl = slice(c * ROWS, (c + 1) * ROWS)
            lsl = slice(LANES * j, LANES * (j + 1))
            qj, kj, vj = q_scr[rsl, lsl], k_scr[rsl, lsl], v_scr[rsl, lsl]
            kjb = _bf(kj)
            envs.append({
                "c": c, "j": j, "rsl": rsl, "lsl": lsl,
                "q2": _bdiag(qj, lane_lo),
                "kst": jnp.concatenate([kjb, kjb], axis=0),
                "k2d": _bdiag(kj, lane_lo) * dec_scr[j, 2],
                "vst": jnp.concatenate([vj, vj], axis=0),
                "dmask": lambda j=j: dec_scr[j, 0],
            })

    def f_ds(k2d, vst):
        return [_dot_ta(_bf(_pad_rows(_seq_rows(k2d, s, ct), LANES)), _pad_rows(_seq_rows(vst, s, ct), LANES))
                * same_head() for s in range(ns)]

    steps = [
        ("sc", ("q2", "kst", "dmask"), lambda q2, kst, dmask: _bf(_dot_tb(q2, kst) * dmask())),
        ("inner", ("sc", "vst"), _dot),
        ("ds", ("k2d", "vst"), f_ds),
    ]
    _emit_by_level(steps, envs)

    if ns == 1:
        per_seq = tt // ROWS
        for s in range(nb):
            state = [s_scr[s, j] for j in range(PAIRS)]
            for c in range(s * per_seq, (s + 1) * per_seq):
                for j in range(PAIRS):
                    env = envs[c * PAIRS + j]
                    env["sb"] = [_bf(state[j])]
                    state[j] = state[j] * dec_scr[j, 3] + env["ds"][0]
            for j in range(PAIRS):
                s_scr[s, j] = state[j]
    else:
        for env in envs:
            j = env["j"]
            first = env["c"] * ns
            env["sb"] = [_bf(s_scr[first + s, j]) for s in range(ns)]
            for s in range(ns):
                s_scr[first + s, j] = s_scr[first + s, j] * dec_scr[j, 3] + env["ds"][s]

    def f_cross(q2, sb):
        return _from_seq_rows([_dot(_seq_rows(q2, s, ct), sb[s]) for s in range(ns)], ct)

    _emit_by_level([("cross", ("q2", "sb"), f_cross)], envs)
    for env in envs:
        o2 = env["inner"] + env["cross"] * dec_scr[env["j"], 1]
        o_scr[env["rsl"], env["lsl"]] = _unstack_heads(o2, lane_lo)

    o = o_scr[...]
    y = jax.nn.silu(gate) * _head_norm(o, gng_ref[...], gnb_ref[...], B_GN_EPS, ones_bd)
    yb_ref[...] = _bf(y).reshape(nb, tt, WIDTH)

    @pl.when(ti == pl.num_programs(1) - 1)
    def _():
        _store_state(sout_ref, s_scr, nb)


def _retention(pb3, s0, invf, gn_g, gn_b, nb, tt, pos0):
    bsz, tlen, _ = pb3.shape
    ct, rows = _check_mixer_blocking(bsz, tlen, nb, tt)
    grid = (bsz // nb, tlen // tt)
    state_spec = pl.BlockSpec((nb, HEADS, HEAD_DIM, HEAD_DIM), lambda b, t: (b, 0, 0, 0))
    in_specs = [
        pl.BlockSpec((nb, tt, B_PROJ), lambda b, t: (b, t, 0)),
        state_spec,
        _row_spec((1, LANES)), _row_spec((1, WIDTH)), _row_spec((1, WIDTH)),
    ]
    out_specs = [
        pl.BlockSpec((nb, tt, WIDTH), lambda b, t: (b, t, 0)),
        state_spec,
    ]
    scratch = [
        pltpu.VMEM((nb, PAIRS, LANES, LANES), F32),
        pltpu.VMEM((rows, WIDTH), BF16),
        pltpu.VMEM((rows, WIDTH), F32),
        pltpu.VMEM((rows, WIDTH), BF16),
        pltpu.VMEM((rows, WIDTH), F32),
        pltpu.VMEM((PAIRS, 4, N2, N2), F32),
        pltpu.VMEM((N2, N2), F32),
        pltpu.VMEM((2, rows, LANES), F32),
    ]
    return pl.pallas_call(
        functools.partial(_ret_kernel, nb, tt, ct, pos0),
        grid=grid,
        in_specs=in_specs,
        out_specs=out_specs,
        out_shape=[
            jax.ShapeDtypeStruct((bsz, tlen, WIDTH), BF16),
            jax.ShapeDtypeStruct(s0.shape, F32),
        ],
        scratch_shapes=scratch,
        compiler_params=pltpu.CompilerParams(
            dimension_semantics=("arbitrary", "arbitrary"), vmem_limit_bytes=VMEM_LIMIT),
        name="retention_mixer",
    )(pb3, s0, invf, gn_g, gn_b)


def _conv_kernel(nb, tt, nsplit, x_ref, buf_ref, win_ref, cw_ref, z_ref, bout_ref, halo_scr):
    rows = nb * tt
    sub = rows // nsplit
    ti = pl.program_id(1)

    @pl.when(ti == 0)
    def _():
        halo_scr[...] = buf_ref[...]

    proj = []
    for i in range(nsplit):
        xi = x_ref[0, i * sub:(i + 1) * sub, :] if nb == 1 else x_ref[...].reshape(rows, D_MODEL)
        xb = _bf(xi)
        bg = _dot(xb, win_ref[:, 0:D_MODEL])
        u = _dot(xb, win_ref[:, D_MODEL:2 * D_MODEL]) * _dot(xb, win_ref[:, 2 * D_MODEL:])
        proj.append((bg, u))

    cw = cw_ref[...]
    st = tt // nsplit if nb == 1 else tt
    t3 = _iota2((nb, st, D_MODEL), 1)
    h0 = halo_scr[:, 0:1, :]
    h1 = halo_scr[:, 1:2, :]
    for i, (bg, u) in enumerate(proj):
        u3 = u.reshape(nb, st, D_MODEL)
        prev1 = jnp.where(t3 == 0, h1, pltpu.roll(u, 1, axis=0).reshape(nb, st, D_MODEL))
        prev2 = jnp.where(t3 == 0, h0,
                          jnp.where(t3 == 1, h1, pltpu.roll(u, 2, axis=0).reshape(nb, st, D_MODEL)))
        conv = prev2 * cw[0:1, :] + prev1 * cw[1:2, :] + u3 * cw[2:3, :]
        z_ref[i * sub:(i + 1) * sub, :] = _bf(bg * conv.reshape(sub, D_MODEL))
        h0 = u3[:, st - 2:st - 1, :]
        h1 = u3[:, st - 1:st, :]
    halo_scr[:, 0:1, :] = h0
    halo_scr[:, 1:2, :] = h1

    @pl.when(ti == pl.num_programs(1) - 1)
    def _():
        bout_ref[...] = halo_scr[...]


def _conv_mixer(x3, buf0, win, cw, nb, tt, nsplit):
    bsz, tlen, _ = x3.shape
    rows = nb * tt
    assert bsz % nb == 0 and tlen % tt == 0
    assert nsplit == 1 or nb == 1
    assert tt % nsplit == 0 and tt // nsplit >= CONV_W - 1
    grid = (bsz // nb, tlen // tt)
    buf_spec = pl.BlockSpec((nb, CONV_W - 1, D_MODEL), lambda b, t: (b, 0, 0))
    return pl.pallas_call(
        functools.partial(_conv_kernel, nb, tt, nsplit),
        grid=grid,
        in_specs=[
            pl.BlockSpec((nb, tt, D_MODEL), lambda b, t: (b, t, 0)),
            buf_spec,
            pl.BlockSpec(win.shape, lambda b, t: (0, 0), pipeline_mode=pl.Buffered(1)),
            _row_spec((CONV_W, D_MODEL)),
        ],
        out_specs=[
            pl.BlockSpec((rows, D_MODEL), lambda b, t: (b * (tlen // tt) + t, 0)),
            buf_spec,
        ],
        out_shape=[
            jax.ShapeDtypeStruct((bsz * tlen, D_MODEL), BF16),
            jax.ShapeDtypeStruct(buf0.shape, F32),
        ],
        scratch_shapes=[pltpu.VMEM((nb, CONV_W - 1, D_MODEL), F32)],
        compiler_params=pltpu.CompilerParams(
            dimension_semantics=("arbitrary", "arbitrary"), vmem_limit_bytes=VMEM_LIMIT),
        name="conv_mixer",
    )(x3, buf0, win, cw)


FF_CHUNK = 1024


def _post_kernel(npieces, nsplit, *refs):
    x_ref = refs[0]
    y_refs = refs[1:1 + npieces]
    wout_ref, g1_ref, b1_ref, wup_ref, wdown_ref, g2_ref, b2_ref, out_ref = refs[1 + npieces:]
    sub = x_ref.shape[0] // nsplit

    def f_y(r):
        y = None
        off = 0
        for y_ref in y_refs:
            n = y_ref.shape[1]
            term = _dot(y_ref[r, :], wout_ref[off:off + n, :])
            y = term if y is None else y + term
            off += n
        return y

    def f_mlp(f):
        def step(x1b, acc):
            h = jnp.maximum(_dot(x1b, wup_ref[:, f * FF_CHUNK:(f + 1) * FF_CHUNK]), 0.0)
            term = _dot(_bf(h * h), wdown_ref[f * FF_CHUNK:(f + 1) * FF_CHUNK, :])
            return term if acc is None else acc + term
        return step

    def f_out(r, x1, acc):
        out_ref[r, :] = _layer_norm(ALPHA * x1 + acc, g2_ref[...], b2_ref[...])
        return None

    steps = [
        ("y", ("r",), f_y),
        ("x1", ("r", "y"), lambda r, y: _layer_norm(ALPHA * x_ref[r, :] + y, g1_ref[...], b1_ref[...])),
        ("x1b", ("x1",), _bf),
        ("acc-1", ("x1b",), lambda x1b: None),
    ]
    nff = D_FF // FF_CHUNK
    for f in range(nff):
        steps.append((f"acc{f}", ("x1b", f"acc{f - 1}"), f_mlp(f)))
    steps.append(("out", ("r", "x1", f"acc{nff - 1}"), f_out))
    envs = [{"r": slice(i * sub, (i + 1) * sub)} for i in range(nsplit)]
    _emit_by_level(steps, envs)


def _post(x2d, ys, wout, g1, b1, wup, wdown, g2, b2, layer, tm, nsplit):
    m = x2d.shape[0]
    assert m % tm == 0 and tm % nsplit == 0
    const = lambda i: (0, 0)
    resident = lambda a: pl.BlockSpec(a.shape, const, pipeline_mode=pl.Buffered(1))
    of_layer = lambda a: pl.BlockSpec((None,) + a.shape[1:], lambda i: (layer, 0, 0),
                                      pipeline_mode=pl.Buffered(1))
    vec = pl.BlockSpec((1, D_MODEL), const)
    in_specs = [pl.BlockSpec((tm, D_MODEL), lambda i: (i, 0))]
    in_specs += [pl.BlockSpec((tm, y.shape[1]), lambda i: (i, 0)) for y in ys]
    in_specs += [resident(wout), vec, vec, of_layer(wup), of_layer(wdown), vec, vec]
    return pl.pallas_call(
        functools.partial(_post_kernel, len(ys), nsplit),
        grid=(m // tm,),
        in_specs=in_specs,
        out_specs=pl.BlockSpec((tm, D_MODEL), lambda i: (i, 0)),
        out_shape=jax.ShapeDtypeStruct((m, D_MODEL), F32),
        compiler_params=pltpu.CompilerParams(
            dimension_semantics=("arbitrary",), vmem_limit_bytes=VMEM_LIMIT),
        name="post_block",
    )(x2d, *ys, wout, g1, b1, wup, wdown, g2, b2)


def _trunk(x, st_shift, st_wkv, st_ret, st_conv, pos0, blk, w):
    bsz, tlen, _ = x.shape
    m = bsz * tlen
    x2d = x.reshape(m, D_MODEL)
    row = lambda a: a.reshape(1, -1)

    pa, pb = _inproj(x2d, w["w_in_ab"], blk["tm_in"])
    pa3 = pa.reshape(bsz, tlen, A_PROJ)
    rw_prm = (row(w["mu_a"]), row(w["w0"]), w["w2p"], row(w["a0"]), w["a2p"], w["g2"],
              row(w["k_k"]), row(w["k_a"]), row(w["r_k"]), row(w["lnx_g"]), row(w["lnx_b"]))
    ya, wkv1 = _rwkv(pa3, st_shift, st_wkv, rw_prm, blk["nb_mix"], blk["tt_mix"])
    yb, ret1 = _retention(pb.reshape(bsz, tlen, B_PROJ), st_ret, w["invf"], row(w["gn_g"]),
                          row(w["gn_b"]), blk["nb_mix"], blk["tt_mix"], pos0)
    shift1 = pa3[:, tlen - 1, :]
    x2d = _post(x2d, [ya.reshape(m, WIDTH), yb.reshape(m, WIDTH)], w["w_out_ab"], row(w["ln1_g"][0]), row(w["ln1_b"][0]),
                w["w_up"], w["w_down"], row(w["ln2_g"][0]), row(w["ln2_b"][0]),
                0, blk["tm_post"], blk["split"])

    z, conv1 = _conv_mixer(x2d.reshape(bsz, tlen, D_MODEL), st_conv, w["w_in_conv"], w["conv_w"],
                           blk["nb_conv"], blk["tt_conv"], blk["split_conv"])
    x2d = _post(x2d, [z], w["w_out_conv"], row(w["ln1_g"][1]), row(w["ln1_b"][1]),
                w["w_up"], w["w_down"], row(w["ln2_g"][1]), row(w["ln2_b"][1]),
                1, blk["tm_post"], blk["split"])
    return x2d.reshape(bsz, tlen, D_MODEL), shift1[None], wkv1[None], ret1[None], conv1[None]


def kernel(x_prompt, x_sample, state_shift, state_wkv, state_ret, state_conv, w_in_ab, mu_a, w0, w2, a0, a2,
           g2, k_k, k_a, r_k, lnx_g, lnx_b, gn_g, gn_b, w_out_ab, w_in_conv, conv_w, w_out_conv,
           ln1_g, ln1_b, ln2_g, ln2_b, w_up, w_down):
    bp, tp, _ = x_prompt.shape
    half = HEAD_DIM // 2
    inv = ROPE_BASE ** (-jnp.arange(half, dtype=F32) / half)
    zpad = jnp.zeros((LANES - DECAY_LORA, WIDTH), F32)
    w = {
        "w_in_ab": _bf(w_in_ab[0]),
        "mu_a": mu_a[0], "w0": w0[0], "a0": a0[0],
        "w2p": _bf(jnp.concatenate([w2[0], zpad], axis=0)),
        "a2p": _bf(jnp.concatenate([zpad, a2[0]], axis=0)),
        "g2": _bf(g2[0]), "k_k": k_k[0], "k_a": k_a[0], "r_k": r_k[0],
        "lnx_g": lnx_g[0], "lnx_b": lnx_b[0], "gn_g": gn_g[0], "gn_b": gn_b[0],
        "invf": jnp.tile(inv, LANES // half).reshape(1, LANES),
        "w_out_ab": _bf(w_out_ab[0]), "w_in_conv": _bf(w_in_conv[0]), "conv_w": conv_w[0],
        "w_out_conv": _bf(w_out_conv[0]),
        "ln1_g": ln1_g, "ln1_b": ln1_b, "ln2_g": ln2_g, "ln2_b": ln2_b,
        "w_up": _bf(w_up), "w_down": _bf(w_down),
    }
    dt = state_wkv.dtype
    z_shift = jnp.zeros((bp, A_PROJ), dt)
    z_state = jnp.zeros((bp, HEADS, HEAD_DIM, HEAD_DIM), dt)
    z_conv = jnp.zeros((bp, CONV_W - 1, D_MODEL), dt)
    blk_p = {"tm_in": 512, "tm_post": 1024, "split": 4, "nb_mix": 2, "tt_mix": 256,
             "nb_conv": 1, "tt_conv": 1024, "split_conv": 4}
    y_p, p_shift, p_wkv, p_ret, p_conv = _trunk(x_prompt, z_shift, z_state, z_state, z_conv, 0, blk_p, w)
    ts = x_sample.shape[1]
    blk_s = {"tm_in": 512, "tm_post": 1024, "split": 2, "nb_mix": 2 * ROWS // ts, "tt_mix": ts,
             "nb_conv": 256 // ts, "tt_conv": ts, "split_conv": 1}
    y_s, s_shift, s_wkv, s_ret, s_conv = _trunk(x_sample, state_shift[0], state_wkv[0], state_ret[0],
                                                state_conv[0], PAST_LEN, blk_s, w)
    return (y_p, y_s, p_shift, p_wkv, p_ret, p_conv, s_shift, s_wkv, s_ret, s_conv)
```

```python
import functools

import jax
import jax.numpy as jnp
from jax import lax
from jax.experimental import pallas as pl
from jax.experimental.pallas import tpu as pltpu

F32 = jnp.float32
BF16 = jnp.bfloat16

D_MODEL = 1024
DEPTH = 2
PAST_LEN = 16384
HEADS = 8
HEAD_DIM = 64
WIDTH = HEADS * HEAD_DIM
DECAY_LORA = 64
AAA_LORA = 64
GATE_LORA = 128
A_PROJ = 3 * WIDTH + DECAY_LORA + AAA_LORA + GATE_LORA
B_PROJ = 4 * WIDTH
A_GN_EPS = 64e-5
B_GN_EPS = 1e-5
ROPE_BASE = 10000.0
CONV_W = 3
D_FF = 4 * D_MODEL
LN_EPS = 1e-5
ALPHA = (2.0 * DEPTH) ** 0.25

LANES = 128
SUBLANES = 8
PAIRS = WIDTH // LANES
ROWS = 64
N2 = 2 * ROWS
VMEM_LIMIT = 56 * 1024 * 1024


def _bf(x):
    return x.astype(BF16)


def _dot(a, b):
    return jnp.dot(a, b, preferred_element_type=F32)


def _dot_tb(a, b):
    return lax.dot_general(a, b, (((1,), (1,)), ((), ())), preferred_element_type=F32)


def _dot_ta(a, b):
    return lax.dot_general(a, b, (((0,), (0,)), ((), ())), preferred_element_type=F32)


def _dot_split_lhs(m, x, parts):
    acc = None
    rem = x
    for i in range(parts):
        hi = _bf(rem)
        term = _dot(m, hi)
        acc = term if acc is None else acc + term
        if i + 1 < parts:
            rem = rem - hi.astype(F32)
    return acc


def _iota2(shape, dim):
    return lax.broadcasted_iota(jnp.int32, shape, dim)


def _div2(x, n):
    assert n & (n - 1) == 0
    return lax.shift_right_logical(x, n.bit_length() - 1)


def _mod2(x, n):
    assert n & (n - 1) == 0
    return lax.bitwise_and(x, n - 1)


def _head_ones():
    ri = _iota2((LANES, LANES), 0)
    ci = _iota2((LANES, LANES), 1)
    return jnp.where(_div2(ri, HEAD_DIM) == _div2(ci, HEAD_DIM), 1.0, 0.0).astype(BF16)


def _headsum(x, ones_bd):
    xb = _bf(x)
    return jnp.concatenate([_dot(xb[:, LANES * j:LANES * (j + 1)], ones_bd) for j in range(PAIRS)], axis=1)


def _head_norm(o, g, b, eps, ones_bd):
    mu = _headsum(o, ones_bd) * (1.0 / HEAD_DIM)
    d = o - mu
    var = _headsum(d * d, ones_bd) * (1.0 / HEAD_DIM)
    return d * lax.rsqrt(var + eps) * g + b


def _layer_norm(z, g, b):
    mu = jnp.mean(z, axis=-1, keepdims=True)
    d = z - mu
    var = jnp.mean(d * d, axis=-1, keepdims=True)
    return d * lax.rsqrt(var + LN_EPS) * g + b


def _bdiag(x, lane_lo):
    zero = jnp.zeros_like(x)
    return jnp.concatenate([jnp.where(lane_lo, x, zero), jnp.where(lane_lo, zero, x)], axis=0)


def _unstack_heads(x, lane_lo):
    return jnp.where(lane_lo, x[0:ROWS], x[ROWS:])


def _stream_masks(ct, width):
    ri = _iota2((N2, width), 0)
    ci = _mod2(_iota2((N2, width), 1), N2)
    same = _div2(ri, ct) == _div2(ci, ct)
    return same & (ci < ri), same & (ci <= ri), ri == ci, _div2(ri, ROWS) == _div2(ci, ROWS)


def _load_state(s_scr, s0_ref, nb):
    z = jnp.zeros((HEAD_DIM, HEAD_DIM), F32)
    for s in range(nb):
        for j in range(PAIRS):
            top = jnp.concatenate([s0_ref[s, 2 * j], z], axis=1)
            bot = jnp.concatenate([z, s0_ref[s, 2 * j + 1]], axis=1)
            s_scr[s, j] = jnp.concatenate([top, bot], axis=0)


def _store_state(sout_ref, s_scr, nb):
    for s in range(nb):
        for j in range(PAIRS):
            s2 = s_scr[s, j]
            sout_ref[s, 2 * j] = s2[0:HEAD_DIM, 0:HEAD_DIM]
            sout_ref[s, 2 * j + 1] = s2[HEAD_DIM:, HEAD_DIM:]


def _seq_rows(x, q, ct):
    if ct == ROWS:
        return x
    return jnp.concatenate([x[q * ct:(q + 1) * ct], x[ROWS + q * ct:ROWS + (q + 1) * ct]], axis=0)


def _from_seq_rows(pieces, ct):
    if ct == ROWS:
        return pieces[0]
    return jnp.concatenate([p[0:ct] for p in pieces] + [p[ct:2 * ct] for p in pieces], axis=0)


def _pad_rows(x, rows):
    if x.shape[0] == rows:
        return x
    return jnp.concatenate([x, jnp.zeros((rows - x.shape[0], x.shape[1]), x.dtype)], axis=0)


def _emit_by_level(steps, envs):
    level = {}
    for name, deps, _ in steps:
        level[name] = 1 + max([level.get(d, 0) for d in deps], default=0)
    for lv in sorted(set(level.values())):
        for name, deps, fn in steps:
            if level[name] == lv:
                for env in envs:
                    env[name] = fn(*[env[d] for d in deps])


def _inverse_steps(nfac, eye):
    def advance(last):
        def fn(q, p):
            pb = _bf(p)
            if last:
                return p + _dot(q, pb), None
            x = _dot(q, jnp.concatenate([pb, q], axis=1))
            return p + x[:, 0:N2], _bf(x[:, N2:])
        return fn

    steps = [
        ("q0", ("l",), _bf),
        ("p1", ("l",), lambda l: l + eye()),
        ("q1", ("q0",), lambda q: _bf(_dot(q, q))),
    ]
    for i in range(1, nfac):
        steps.append((f"s{i}", (f"q{i}", f"p{i}"), advance(i == nfac - 1)))
        steps.append((f"p{i + 1}", (f"s{i}",), lambda s: s[0]))
        steps.append((f"q{i + 1}", (f"s{i}",), lambda s: s[1]))
    steps.append(("tb", (f"p{nfac}",), _bf))
    return steps


def _rwkv_score_steps(sc_mask):
    return [
        ("scm", ("a2", "r2", "bk"),
         lambda a2, r2, bk: _dot_tb(jnp.concatenate([a2, r2], axis=0), bk) * sc_mask()),
        ("l", ("scm",), lambda scm: scm[0:N2, 0:N2]),
        ("lkb", ("scm",), lambda scm: _bf(scm[:, N2:])),
        ("arb", ("scm",), lambda scm: _bf(scm[N2:, 0:N2])),
        ("lvark", ("lkb", "vst"), _dot),
    ]


def _rwkv_carried_steps(same_head):
    def f_ro(arb, tal, r2, lvark):
        x = _dot(arb, _bf(tal))
        return _bf(r2.astype(F32) + x[:, 0:N2]), x[:, N2:] + lvark[N2:]

    def f_mn(tal, b2, k2, vst):
        mm = _dot_ta(_bf(tal), b2)
        n2 = _dot_ta(vst, k2)
        return _bf(mm[0:N2]), (mm[N2:] + n2) * same_head()

    return [
        ("tal", ("tb", "a2", "lvark"),
         lambda tb, a2, lvark: _dot(tb, jnp.concatenate([a2, _bf(lvark[0:N2])], axis=1))),
        ("ro", ("arb", "tal", "r2", "lvark"), f_ro),
        ("mn", ("tal", "b2", "k2", "vst"), f_mn),
    ]


def _inproj_kernel(x_ref, w_ref, pa_ref, pb_ref):
    xb = _bf(x_ref[...])
    pa_ref[...] = _dot(xb, w_ref[:, 0:A_PROJ])
    pb_ref[...] = _dot(xb, w_ref[:, A_PROJ:])


def _inproj(x2d, w, tm):
    m = x2d.shape[0]
    return pl.pallas_call(
        _inproj_kernel,
        grid=(m // tm,),
        in_specs=[
            pl.BlockSpec((tm, D_MODEL), lambda i: (i, 0)),
            pl.BlockSpec(w.shape, lambda i: (0, 0), pipeline_mode=pl.Buffered(1)),
        ],
        out_specs=[
            pl.BlockSpec((tm, A_PROJ), lambda i: (i, 0)),
            pl.BlockSpec((tm, B_PROJ), lambda i: (i, 0)),
        ],
        out_shape=[
            jax.ShapeDtypeStruct((m, A_PROJ), F32),
            jax.ShapeDtypeStruct((m, B_PROJ), F32),
        ],
        compiler_params=pltpu.CompilerParams(
            dimension_semantics=("arbitrary",), vmem_limit_bytes=VMEM_LIMIT),
        name="inproj_ab",
    )(x2d, w)


def _rwkv_kernel(nb, tt, ct, pa_ref, shift_ref, s0_ref, mu_ref, w0_ref, w2p_ref, a0_ref, a2p_ref,
                 g2_ref, kk_ref, ka_ref, rk_ref, lng_ref, lnb_ref,
                 ya_ref, sout_ref,
                 s_scr, prev_scr, at_scr, rt_scr, bt_scr, kt_scr, v_scr, p_scr, o_scr,
                 scmask_scr, eye_scr, head_scr, lcum_scr):
    rows = nb * tt
    nchunk = rows // ROWS
    ns = ROWS // ct
    ti = pl.program_id(1)

    @pl.when((pl.program_id(0) == 0) & (ti == 0))
    def _():
        strict, incl, eye, same_head = _stream_masks(ct, 2 * N2)
        one = lambda m: jnp.where(m, 1.0, 0.0)
        scmask_scr[...] = jnp.concatenate([one(strict), one(incl)], axis=0)
        eye_scr[...] = one(eye)[:, 0:N2]
        head_scr[...] = one(same_head)[:, 0:N2]
        ri = _iota2(lcum_scr.shape, 0)
        ci = _iota2(lcum_scr.shape, 1)
        lcum_scr[...] = jnp.where((_div2(ri, ct) == _div2(ci, ct)) & (ci <= ri), 1.0, 0.0).astype(BF16)

    @pl.when(ti == 0)
    def _():
        _load_state(s_scr, s0_ref, nb)
        prev_scr[...] = shift_ref[...]

    ones_bd = _head_ones()

    p3 = pa_ref[...]
    p2 = p3.reshape(rows, A_PROJ)
    t3 = _iota2((nb, tt, A_PROJ), 1)
    rolled = pltpu.roll(p2, 1, axis=0).reshape(nb, tt, A_PROJ)
    pprev = jnp.where(t3 == 0, prev_scr[...], rolled).reshape(rows, A_PROJ)
    prev_scr[...] = p3[:, tt - 1:tt, :]
    m = p2 + (pprev - p2) * mu_ref[...]
    r = m[:, 0:WIDTH]
    k = m[:, WIDTH:2 * WIDTH]
    v = m[:, 2 * WIDTH:3 * WIDTH]
    wa = m[:, 3 * WIDTH:3 * WIDTH + LANES]
    gd = m[:, 3 * WIDTH + LANES:A_PROJ]
    z = -(w0_ref[...] + _dot(_bf(jnp.tanh(wa)), w2p_ref[...]))
    softplus = jnp.maximum(z, 0.0) + jnp.log(1.0 + jnp.exp(-jnp.abs(z)))
    lw = -jnp.exp(-softplus - 0.5)
    a = jax.nn.sigmoid(a0_ref[...] + _dot(_bf(wa), a2p_ref[...]))
    g = _dot(_bf(jax.nn.sigmoid(gd)), g2_ref[...])
    kk = k * kk_ref[...]
    k = k * (1.0 + (a - 1.0) * ka_ref[...])
    kk = kk * lax.rsqrt(jnp.maximum(_headsum(kk * kk, ones_bd), 1e-24))
    bonus = _headsum(r * k * rk_ref[...], ones_bd) * v

    span = lcum_scr.shape[0]
    cs = jnp.concatenate([_dot_split_lhs(lcum_scr[...], lw[i * span:(i + 1) * span], 2)
                          for i in range(rows // span)], axis=0)
    pinv = jnp.exp(-cs)
    p_scr[...] = jnp.exp(cs)
    at_scr[...] = _bf(-kk * jnp.exp(cs - lw))
    rt_scr[...] = _bf(r * p_scr[...])
    bt_scr[...] = _bf(kk * a * pinv)
    kt_scr[...] = _bf(k * pinv)
    v_scr[...] = _bf(v)

    lane_lo = _iota2((ROWS, LANES), 1) < HEAD_DIM
    sc_mask = lambda: scmask_scr[...]
    eye = lambda: eye_scr[...]
    same_head = lambda: head_scr[...]
    nfac = ct.bit_length() - 1

    envs = []
    for c in range(nchunk):
        for j in range(PAIRS):
            rsl = slice(c * ROWS, (c + 1) * ROWS)
            lsl = slice(LANES * j, LANES * (j + 1))
            aj, rj, bj, kj, vj = (s[rsl, lsl] for s in (at_scr, rt_scr, bt_scr, kt_scr, v_scr))
            envs.append({
                "c": c, "j": j, "rsl": rsl, "lsl": lsl,
                "a2": _bdiag(aj, lane_lo), "r2": _bdiag(rj, lane_lo),
                "b2": _bdiag(bj, lane_lo), "k2": _bdiag(kj, lane_lo),
                "bk": jnp.concatenate([bj, bj, kj, kj], axis=0),
                "vst": jnp.concatenate([vj, vj], axis=0),
            })

    steps = _rwkv_score_steps(sc_mask) + _inverse_steps(nfac, eye)
    if ns == 1:
        _emit_by_level(steps + _rwkv_carried_steps(same_head), envs)
        per_seq = tt // ROWS
        state = [[s_scr[s, j] for j in range(PAIRS)] for s in range(nb)]
        for cpos in range(per_seq):
            for s in range(nb):
                c = s * per_seq + cpos
                cenv = envs[c * PAIRS:(c + 1) * PAIRS]
                sb = [_bf(x) for x in state[s]]
                for j, env in enumerate(cenv):
                    rp, oc = env["ro"]
                    o_scr[env["rsl"], env["lsl"]] = _unstack_heads(_dot_tb(rp, sb[j]) + oc, lane_lo)
                for j, env in enumerate(cenv):
                    mk, nn = env["mn"]
                    pc = p_scr[(c + 1) * ROWS - 1:(c + 1) * ROWS, env["lsl"]]
                    state[s][j] = (state[s][j] + _dot(sb[j], mk) + nn) * pc
        for s in range(nb):
            for j in range(PAIRS):
                s_scr[s, j] = state[s][j]
    else:
        for env in envs:
            env["sb"] = [_bf(s_scr[env["c"] * ns + q, env["j"]]) for q in range(ns)]

        def f_h(a2, r2, sb):
            pieces = [_dot_tb(jnp.concatenate([_seq_rows(a2, q, ct), _seq_rows(r2, q, ct)], axis=0), sb[q])
                      for q in range(ns)]
            return (_from_seq_rows([x[0:2 * ct] for x in pieces], ct),
                    _from_seq_rows([x[2 * ct:] for x in pieces], ct))

        def f_ds(u2, vst, b2, k2):
            vst32 = vst.astype(F32)
            out = []
            for q in range(ns):
                uv = jnp.concatenate([_seq_rows(u2, q, ct), _seq_rows(vst32, q, ct)], axis=0)
                bk = jnp.concatenate([_seq_rows(b2, q, ct), _seq_rows(k2, q, ct)], axis=0)
                out.append(_dot_ta(_bf(_pad_rows(uv, LANES)), _pad_rows(bk, LANES)) * same_head())
            return out

        steps += [
            ("h", ("a2", "r2", "sb"), f_h),
            ("u2", ("tb", "h", "lvark"), lambda tb, h, lvark: _dot(tb, _bf(h[0] + lvark[0:N2]))),
            ("o2", ("arb", "u2", "h", "lvark"),
             lambda arb, u2, h, lvark: h[1] + _dot(arb, _bf(u2)) + lvark[N2:]),
            ("ds", ("u2", "vst", "b2", "k2"), f_ds),
        ]
        _emit_by_level(steps, envs)
        for env in envs:
            o_scr[env["rsl"], env["lsl"]] = _unstack_heads(env["o2"], lane_lo)
            for q in range(ns):
                seq = env["c"] * ns + q
                pc = p_scr[(seq + 1) * ct - 1:(seq + 1) * ct, env["lsl"]]
                s_scr[seq, env["j"]] = (s_scr[seq, env["j"]] + env["ds"][q]) * pc

    o = o_scr[...]
    y = (_head_norm(o, lng_ref[...], lnb_ref[...], A_GN_EPS, ones_bd) + bonus) * g
    ya_ref[...] = _bf(y).reshape(nb, tt, WIDTH)

    @pl.when(ti == pl.num_programs(1) - 1)
    def _():
        _store_state(sout_ref, s_scr, nb)


def _row_spec(shape):
    return pl.BlockSpec(shape, lambda b, t: (0,) * len(shape))


def _check_mixer_blocking(bsz, tlen, nb, tt):
    ct = min(ROWS, tt)
    rows = nb * tt
    assert rows % ROWS == 0 and ROWS % ct == 0 and bsz % nb == 0 and tlen % tt == 0
    assert tt % ROWS == 0 or tt == tlen
    return ct, rows


def _rwkv(pa3, shift0, s0, prm, nb, tt):
    bsz, tlen, _ = pa3.shape
    ct, rows = _check_mixer_blocking(bsz, tlen, nb, tt)
    span = tt if tt % ROWS == 0 else rows
    grid = (bsz // nb, tlen // tt)
    state_spec = pl.BlockSpec((nb, HEADS, HEAD_DIM, HEAD_DIM), lambda b, t: (b, 0, 0, 0))
    vec = lambda n: _row_spec((1, n))
    in_specs = [
        pl.BlockSpec((nb, tt, A_PROJ), lambda b, t: (b, t, 0)),
        pl.BlockSpec((nb, 1, A_PROJ), lambda b, t: (b, 0, 0)),
        state_spec,
        vec(A_PROJ), vec(WIDTH), _row_spec((LANES, WIDTH)), vec(WIDTH), _row_spec((LANES, WIDTH)),
        _row_spec((GATE_LORA, WIDTH)), vec(WIDTH), vec(WIDTH), vec(WIDTH), vec(WIDTH), vec(WIDTH),
    ]
    out_specs = [
        pl.BlockSpec((nb, tt, WIDTH), lambda b, t: (b, t, 0)),
        state_spec,
    ]
    scratch = [
        pltpu.VMEM((nb, PAIRS, LANES, LANES), F32),
        pltpu.VMEM((nb, 1, A_PROJ), F32),
        pltpu.VMEM((rows, WIDTH), BF16),
        pltpu.VMEM((rows, WIDTH), BF16),
        pltpu.VMEM((rows, WIDTH), BF16),
        pltpu.VMEM((rows, WIDTH), BF16),
        pltpu.VMEM((rows, WIDTH), BF16),
        pltpu.VMEM((rows, WIDTH), F32),
        pltpu.VMEM((rows, WIDTH), F32),
        pltpu.VMEM((2 * N2, 2 * N2), F32),
        pltpu.VMEM((N2, N2), F32),
        pltpu.VMEM((N2, N2), F32),
        pltpu.VMEM((span, span), BF16),
    ]
    return pl.pallas_call(
        functools.partial(_rwkv_kernel, nb, tt, ct),
        grid=grid,
        in_specs=in_specs,
        out_specs=out_specs,
        out_shape=[
            jax.ShapeDtypeStruct((bsz, tlen, WIDTH), BF16),
            jax.ShapeDtypeStruct(s0.shape, F32),
        ],
        scratch_shapes=scratch,
        compiler_params=pltpu.CompilerParams(
            dimension_semantics=("arbitrary", "arbitrary"), vmem_limit_bytes=VMEM_LIMIT),
        name="rwkv7_mixer",
    )(pa3, shift0.reshape(bsz, 1, A_PROJ), s0, *prm)


def _ret_kernel(nb, tt, ct, pos0, pb_ref, s0_ref, invf_ref, gng_ref, gnb_ref,
                yb_ref, sout_ref,
                s_scr, q_scr, k_scr, v_scr, o_scr, dec_scr, head_scr, trig_scr):
    rows = nb * tt
    nchunk = rows // ROWS
    ns = ROWS // ct
    ti = pl.program_id(1)

    @pl.when((pl.program_id(0) == 0) & (ti == 0))
    def _():
        _, incl, _, same_head = _stream_masks(ct, N2)
        head_scr[...] = jnp.where(same_head, 1.0, 0.0)
        ri = _iota2((N2, N2), 0)
        ci = _iota2((N2, N2), 1)
        tpos = _mod2(ri, ct).astype(F32)
        tdiff = (_mod2(ri, ct) - _mod2(ci, ct)).astype(F32)
        for j in range(PAIRS):
            head = (2 * j + _div2(ri, ROWS)).astype(F32)
            lg = jnp.log1p(-jnp.exp2(-5.0 - head))
            dec_scr[j, 0] = jnp.where(incl, jnp.exp(lg * jnp.maximum(tdiff, 0.0)), 0.0)
            dec_scr[j, 1] = jnp.exp(lg * (tpos + 1.0))
            dec_scr[j, 2] = jnp.exp(lg * (ct - 1.0 - tpos))
            dec_scr[j, 3] = jnp.exp(lg * ct)
        off = _mod2(_iota2((rows, LANES), 0), tt).astype(F32) * invf_ref[...]
        trig_scr[0] = jnp.cos(off)
        trig_scr[1] = jnp.sin(off)

    @pl.when(ti == 0)
    def _():
        _load_state(s_scr, s0_ref, nb)

    ones_bd = _head_ones()
    pb = pb_ref[...].reshape(rows, B_PROJ)
    q = pb[:, 0:WIDTH]
    k = pb[:, WIDTH:2 * WIDTH]
    gate = pb[:, 3 * WIDTH:]

    base = jnp.zeros((SUBLANES, LANES), F32) + (pos0 + ti * tt).astype(F32)
    ang = base * invf_ref[...]
    cos_a, sin_a = jnp.cos(ang)[0:1], jnp.sin(ang)[0:1]
    cos_b, sin_b = trig_scr[0], trig_scr[1]
    cos = jnp.concatenate([cos_a * cos_b - sin_a * sin_b] * PAIRS, axis=1)
    sin = jnp.concatenate([sin_a * cos_b + cos_a * sin_b] * PAIRS, axis=1)
    first_half = _mod2(_iota2((rows, WIDTH), 1), HEAD_DIM) < (HEAD_DIM // 2)
    sin = jnp.where(first_half, -sin, sin)

    pr = _iota2((LANES, LANES), 0)
    pc = _iota2((LANES, LANES), 1)
    swap = jnp.where(pr == lax.bitwise_xor(pc, HEAD_DIM // 2), 1.0, 0.0).astype(BF16)

    def rope(x):
        xb = _bf(x)
        partner = jnp.concatenate([_dot(xb[:, LANES * j:LANES * (j + 1)], swap) for j in range(PAIRS)], axis=1)
        return x * cos + partner * sin

    q_scr[...] = _bf(rope(q))
    k_scr[...] = rope(k) * (HEAD_DIM ** -0.5)
    v_scr[...] = _bf(pb[:, 2 * WIDTH:3 * WIDTH])

    lane_lo = _iota2((ROWS, LANES), 1) < HEAD_DIM
    same_head = lambda: head_scr[...]

    envs = []
    for c in range(nchunk):
        for j in range(PAIRS):
            rsl = slice(c * ROWS, (c + 1) * ROWS)
            lsl = slice(LANES * j, LANES * (j + 1))
            qj, kj, vj = q_scr[rsl, lsl], k_scr[rsl, lsl], v_scr[rsl, lsl]
            kjb = _bf(kj)
            envs.append({
                "c": c, "j": j, "rsl": rsl, "lsl": lsl,
                "q2": _bdiag(qj, lane_lo),
                "kst": jnp.concatenate([kjb, kjb], axis=0),
                "k2d": _bdiag(kj, lane_lo) * dec_scr[j, 2],
                "vst": jnp.concatenate([vj, vj], axis=0),
                "dmask": lambda j=j: dec_scr[j, 0],
            })

    def f_ds(k2d, vst):
        return [_dot_ta(_bf(_pad_rows(_seq_rows(k2d, s, ct), LANES)), _pad_rows(_seq_rows(vst, s, ct), LANES))
                * same_head() for s in range(ns)]

    steps = [
        ("sc", ("q2", "kst", "dmask"), lambda q2, kst, dmask: _bf(_dot_tb(q2, kst) * dmask())),
        ("inner", ("sc", "vst"), _dot),
        ("ds", ("k2d", "vst"), f_ds),
    ]
    _emit_by_level(steps, envs)

    if ns == 1:
        per_seq = tt // ROWS
        for s in range(nb):
            state = [s_scr[s, j] for j in range(PAIRS)]
            for c in range(s * per_seq, (s + 1) * per_seq):
                for j in range(PAIRS):
                    env = envs[c * PAIRS + j]
                    env["sb"] = [_bf(state[j])]
                    state[j] = state[j] * dec_scr[j, 3] + env["ds"][0]
            for j in range(PAIRS):
                s_scr[s, j] = state[j]
    else:
        for env in envs:
            j = env["j"]
            first = env["c"] * ns
            env["sb"] = [_bf(s_scr[first + s, j]) for s in range(ns)]
            for s in range(ns):
                s_scr[first + s, j] = s_scr[first + s, j] * dec_scr[j, 3] + env["ds"][s]

    def f_cross(q2, sb):
        return _from_seq_rows([_dot(_seq_rows(q2, s, ct), sb[s]) for s in range(ns)], ct)

    _emit_by_level([("cross", ("q2", "sb"), f_cross)], envs)
    for env in envs:
        o2 = env["inner"] + env["cross"] * dec_scr[env["j"], 1]
        o_scr[env["rsl"], env["lsl"]] = _unstack_heads(o2, lane_lo)

    o = o_scr[...]
    y = jax.nn.silu(gate) * _head_norm(o, gng_ref[...], gnb_ref[...], B_GN_EPS, ones_bd)
    yb_ref[...] = _bf(y).reshape(nb, tt, WIDTH)

    @pl.when(ti == pl.num_programs(1) - 1)
    def _():
        _store_state(sout_ref, s_scr, nb)


def _retention(pb3, s0, invf, gn_g, gn_b, nb, tt, pos0):
    bsz, tlen, _ = pb3.shape
    ct, rows = _check_mixer_blocking(bsz, tlen, nb, tt)
    grid = (bsz // nb, tlen // tt)
    state_spec = pl.BlockSpec((nb, HEADS, HEAD_DIM, HEAD_DIM), lambda b, t: (b, 0, 0, 0))
    in_specs = [
        pl.BlockSpec((nb, tt, B_PROJ), lambda b, t: (b, t, 0)),
        state_spec,
        _row_spec((1, LANES)), _row_spec((1, WIDTH)), _row_spec((1, WIDTH)),
    ]
    out_specs = [
        pl.BlockSpec((nb, tt, WIDTH), lambda b, t: (b, t, 0)),
        state_spec,
    ]
    scratch = [
        pltpu.VMEM((nb, PAIRS, LANES, LANES), F32),
        pltpu.VMEM((rows, WIDTH), BF16),
        pltpu.VMEM((rows, WIDTH), F32),
        pltpu.VMEM((rows, WIDTH), BF16),
        pltpu.VMEM((rows, WIDTH), F32),
        pltpu.VMEM((PAIRS, 4, N2, N2), F32),
        pltpu.VMEM((N2, N2), F32),
        pltpu.VMEM((2, rows, LANES), F32),
    ]
    return pl.pallas_call(
        functools.partial(_ret_kernel, nb, tt, ct, pos0),
        grid=grid,
        in_specs=in_specs,
        out_specs=out_specs,
        out_shape=[
            jax.ShapeDtypeStruct((bsz, tlen, WIDTH), BF16),
            jax.ShapeDtypeStruct(s0.shape, F32),
        ],
        scratch_shapes=scratch,
        compiler_params=pltpu.CompilerParams(
            dimension_semantics=("arbitrary", "arbitrary"), vmem_limit_bytes=VMEM_LIMIT),
        name="retention_mixer",
    )(pb3, s0, invf, gn_g, gn_b)


def _conv_kernel(nb, tt, nsplit, x_ref, buf_ref, win_ref, cw_ref, z_ref, bout_ref, halo_scr):
    rows = nb * tt
    sub = rows // nsplit
    ti = pl.program_id(1)

    @pl.when(ti == 0)
    def _():
        halo_scr[...] = buf_ref[...]

    proj = []
    for i in range(nsplit):
        xi = x_ref[0, i * sub:(i + 1) * sub, :] if nb == 1 else x_ref[...].reshape(rows, D_MODEL)
        xb = _bf(xi)
        bg = _dot(xb, win_ref[:, 0:D_MODEL])
        u = _dot(xb, win_ref[:, D_MODEL:2 * D_MODEL]) * _dot(xb, win_ref[:, 2 * D_MODEL:])
        proj.append((bg, u))

    cw = cw_ref[...]
    st = tt // nsplit if nb == 1 else tt
    t3 = _iota2((nb, st, D_MODEL), 1)
    h0 = halo_scr[:, 0:1, :]
    h1 = halo_scr[:, 1:2, :]
    for i, (bg, u) in enumerate(proj):
        u3 = u.reshape(nb, st, D_MODEL)
        prev1 = jnp.where(t3 == 0, h1, pltpu.roll(u, 1, axis=0).reshape(nb, st, D_MODEL))
        prev2 = jnp.where(t3 == 0, h0,
                          jnp.where(t3 == 1, h1, pltpu.roll(u, 2, axis=0).reshape(nb, st, D_MODEL)))
        conv = prev2 * cw[0:1, :] + prev1 * cw[1:2, :] + u3 * cw[2:3, :]
        z_ref[i * sub:(i + 1) * sub, :] = _bf(bg * conv.reshape(sub, D_MODEL))
        h0 = u3[:, st - 2:st - 1, :]
        h1 = u3[:, st - 1:st, :]
    halo_scr[:, 0:1, :] = h0
    halo_scr[:, 1:2, :] = h1

    @pl.when(ti == pl.num_programs(1) - 1)
    def _():
        bout_ref[...] = halo_scr[...]


def _conv_mixer(x3, buf0, win, cw, nb, tt, nsplit):
    bsz, tlen, _ = x3.shape
    rows = nb * tt
    assert bsz % nb == 0 and tlen % tt == 0
    assert nsplit == 1 or nb == 1
    assert tt % nsplit == 0 and tt // nsplit >= CONV_W - 1
    grid = (bsz // nb, tlen // tt)
    buf_spec = pl.BlockSpec((nb, CONV_W - 1, D_MODEL), lambda b, t: (b, 0, 0))
    return pl.pallas_call(
        functools.partial(_conv_kernel, nb, tt, nsplit),
        grid=grid,
        in_specs=[
            pl.BlockSpec((nb, tt, D_MODEL), lambda b, t: (b, t, 0)),
            buf_spec,
            pl.BlockSpec(win.shape, lambda b, t: (0, 0), pipeline_mode=pl.Buffered(1)),
            _row_spec((CONV_W, D_MODEL)),
        ],
        out_specs=[
            pl.BlockSpec((rows, D_MODEL), lambda b, t: (b * (tlen // tt) + t, 0)),
            buf_spec,
        ],
        out_shape=[
            jax.ShapeDtypeStruct((bsz * tlen, D_MODEL), BF16),
            jax.ShapeDtypeStruct(buf0.shape, F32),
        ],
        scratch_shapes=[pltpu.VMEM((nb, CONV_W - 1, D_MODEL), F32)],
        compiler_params=pltpu.CompilerParams(
            dimension_semantics=("arbitrary", "arbitrary"), vmem_limit_bytes=VMEM_LIMIT),
        name="conv_mixer",
    )(x3, buf0, win, cw)


FF_CHUNK = 1024


def _post_kernel(npieces, nsplit, *refs):
    x_ref = refs[0]
    y_refs = refs[1:1 + npieces]
    wout_ref, g1_ref, b1_ref, wup_ref, wdown_ref, g2_ref, b2_ref, out_ref = refs[1 + npieces:]
    sub = x_ref.shape[0] // nsplit

    def f_y(r):
        y = None
        off = 0
        for y_ref in y_refs:
            n = y_ref.shape[1]
            term = _dot(y_ref[r, :], wout_ref[off:off + n, :])
            y = term if y is None else y + term
            off += n
        return y

    def f_mlp(f):
        def step(x1b, acc):
            h = jnp.maximum(_dot(x1b, wup_ref[:, f * FF_CHUNK:(f + 1) * FF_CHUNK]), 0.0)
            term = _dot(_bf(h * h), wdown_ref[f * FF_CHUNK:(f + 1) * FF_CHUNK, :])
            return term if acc is None else acc + term
        return step

    def f_out(r, x1, acc):
        out_ref[r, :] = _layer_norm(ALPHA * x1 + acc, g2_ref[...], b2_ref[...])
        return None

    steps = [
        ("y", ("r",), f_y),
        ("x1", ("r", "y"), lambda r, y: _layer_norm(ALPHA * x_ref[r, :] + y, g1_ref[...], b1_ref[...])),
        ("x1b", ("x1",), _bf),
        ("acc-1", ("x1b",), lambda x1b: None),
    ]
    nff = D_FF // FF_CHUNK
    for f in range(nff):
        steps.append((f"acc{f}", ("x1b", f"acc{f - 1}"), f_mlp(f)))
    steps.append(("out", ("r", "x1", f"acc{nff - 1}"), f_out))
    envs = [{"r": slice(i * sub, (i + 1) * sub)} for i in range(nsplit)]
    _emit_by_level(steps, envs)


def _post(x2d, ys, wout, g1, b1, wup, wdown, g2, b2, layer, tm, nsplit):
    m = x2d.shape[0]
    assert m % tm == 0 and tm % nsplit == 0
    const = lambda i: (0, 0)
    resident = lambda a: pl.BlockSpec(a.shape, const, pipeline_mode=pl.Buffered(1))
    of_layer = lambda a: pl.BlockSpec((None,) + a.shape[1:], lambda i: (layer, 0, 0),
                                      pipeline_mode=pl.Buffered(1))
    vec = pl.BlockSpec((1, D_MODEL), const)
    in_specs = [pl.BlockSpec((tm, D_MODEL), lambda i: (i, 0))]
    in_specs += [pl.BlockSpec((tm, y.shape[1]), lambda i: (i, 0)) for y in ys]
    in_specs += [resident(wout), vec, vec, of_layer(wup), of_layer(wdown), vec, vec]
    return pl.pallas_call(
        functools.partial(_post_kernel, len(ys), nsplit),
        grid=(m // tm,),
        in_specs=in_specs,
        out_specs=pl.BlockSpec((tm, D_MODEL), lambda i: (i, 0)),
        out_shape=jax.ShapeDtypeStruct((m, D_MODEL), F32),
        compiler_params=pltpu.CompilerParams(
            dimension_semantics=("arbitrary",), vmem_limit_bytes=VMEM_LIMIT),
        name="post_block",
    )(x2d, *ys, wout, g1, b1, wup, wdown, g2, b2)


def _trunk(x, st_shift, st_wkv, st_ret, st_conv, pos0, blk, w):
    bsz, tlen, _ = x.shape
    m = bsz * tlen
    x2d = x.reshape(m, D_MODEL)
    row = lambda a: a.reshape(1, -1)

    pa, pb = _inproj(x2d, w["w_in_ab"], blk["tm_in"])
    pa3 = pa.reshape(bsz, tlen, A_PROJ)
    rw_prm = (row(w["mu_a"]), row(w["w0"]), w["w2p"], row(w["a0"]), w["a2p"], w["g2"],
              row(w["k_k"]), row(w["k_a"]), row(w["r_k"]), row(w["lnx_g"]), row(w["lnx_b"]))
    ya, wkv1 = _rwkv(pa3, st_shift, st_wkv, rw_prm, blk["nb_mix"], blk["tt_mix"])
    yb, ret1 = _retention(pb.reshape(bsz, tlen, B_PROJ), st_ret, w["invf"], row(w["gn_g"]),
                          row(w["gn_b"]), blk["nb_mix"], blk["tt_mix"], pos0)
    shift1 = pa3[:, tlen - 1, :]
    x2d = _post(x2d, [ya.reshape(m, WIDTH), yb.reshape(m, WIDTH)], w["w_out_ab"], row(w["ln1_g"][0]), row(w["ln1_b"][0]),
                w["w_up"], w["w_down"], row(w["ln2_g"][0]), row(w["ln2_b"][0]),
                0, blk["tm_post"], blk["split"])

    z, conv1 = _conv_mixer(x2d.reshape(bsz, tlen, D_MODEL), st_conv, w["w_in_conv"], w["conv_w"],
                           blk["nb_conv"], blk["tt_conv"], blk["split_conv"])
    x2d = _post(x2d, [z], w["w_out_conv"], row(w["ln1_g"][1]), row(w["ln1_b"][1]),
                w["w_up"], w["w_down"], row(w["ln2_g"][1]), row(w["ln2_b"][1]),
                1, blk["tm_post"], blk["split"])
    return x2d.reshape(bsz, tlen, D_MODEL), shift1[None], wkv1[None], ret1[None], conv1[None]


def kernel(x_prompt, x_sample, state_shift, state_wkv, state_ret, state_conv, w_in_ab, mu_a, w0, w2, a0, a2,
           g2, k_k, k_a, r_k, lnx_g, lnx_b, gn_g, gn_b, w_out_ab, w_in_conv, conv_w, w_out_conv,
           ln1_g, ln1_b, ln2_g, ln2_b, w_up, w_down):
    bp, tp, _ = x_prompt.shape
    half = HEAD_DIM // 2
    inv = ROPE_BASE ** (-jnp.arange(half, dtype=F32) / half)
    zpad = jnp.zeros((LANES - DECAY_LORA, WIDTH), F32)
    w = {
        "w_in_ab": _bf(w_in_ab[0]),
        "mu_a": mu_a[0], "w0": w0[0], "a0": a0[0],
        "w2p": _bf(jnp.concatenate([w2[0], zpad], axis=0)),
        "a2p": _bf(jnp.concatenate([zpad, a2[0]], axis=0)),
        "g2": _bf(g2[0]), "k_k": k_k[0], "k_a": k_a[0], "r_k": r_k[0],
        "lnx_g": lnx_g[0], "lnx_b": lnx_b[0], "gn_g": gn_g[0], "gn_b": gn_b[0],
        "invf": jnp.tile(inv, LANES // half).reshape(1, LANES),
        "w_out_ab": _bf(w_out_ab[0]), "w_in_conv": _bf(w_in_conv[0]), "conv_w": conv_w[0],
        "w_out_conv": _bf(w_out_conv[0]),
        "ln1_g": ln1_g, "ln1_b": ln1_b, "ln2_g": ln2_g, "ln2_b": ln2_b,
        "w_up": _bf(w_up), "w_down": _bf(w_down),
    }
    dt = state_wkv.dtype
    z_shift = jnp.zeros((bp, A_PROJ), dt)
    z_state = jnp.zeros((bp, HEADS, HEAD_DIM, HEAD_DIM), dt)
    z_conv = jnp.zeros((bp, CONV_W - 1, D_MODEL), dt)
    blk_p = {"tm_in": 1024, "tm_post": 1024, "split": 4, "nb_mix": 2, "tt_mix": 256,
             "nb_conv": 1, "tt_conv": 1024, "split_conv": 2}
    y_p, p_shift, p_wkv, p_ret, p_conv = _trunk(x_prompt, z_shift, z_state, z_state, z_conv, 0, blk_p, w)
    ts = x_sample.shape[1]
    blk_s = {"tm_in": 1024, "tm_post": 1024, "split": 4, "nb_mix": 4 * ROWS // ts, "tt_mix": ts,
             "nb_conv": 256 // ts, "tt_conv": ts, "split_conv": 1}
    y_s, s_shift, s_wkv, s_ret, s_conv = _trunk(x_sample, state_shift[0], state_wkv[0], state_ret[0],
                                                state_conv[0], PAST_LEN, blk_s, w)
    return (y_p, y_s, p_shift, p_wkv, p_ret, p_conv, s_shift, s_wkv, s_ret, s_conv)
```

```python
import functools

import jax
import jax.numpy as jnp
from jax import lax
from jax.experimental import pallas as pl
from jax.experimental.pallas import tpu as pltpu

F32 = jnp.float32
BF16 = jnp.bfloat16

D_MODEL = 1024
DEPTH = 2
PAST_LEN = 16384
HEADS = 8
HEAD_DIM = 64
WIDTH = HEADS * HEAD_DIM
DECAY_LORA = 64
AAA_LORA = 64
GATE_LORA = 128
A_PROJ = 3 * WIDTH + DECAY_LORA + AAA_LORA + GATE_LORA
B_PROJ = 4 * WIDTH
A_GN_EPS = 64e-5
B_GN_EPS = 1e-5
ROPE_BASE = 10000.0
CONV_W = 3
D_FF = 4 * D_MODEL
LN_EPS = 1e-5
ALPHA = (2.0 * DEPTH) ** 0.25

LANES = 128
SUBLANES = 8
PAIRS = WIDTH // LANES
ROWS = 64
N2 = 2 * ROWS
VMEM_LIMIT = 56 * 1024 * 1024


def _bf(x):
    return x.astype(BF16)


def _dot(a, b):
    return jnp.dot(a, b, preferred_element_type=F32)


def _dot_tb(a, b):
    return lax.dot_general(a, b, (((1,), (1,)), ((), ())), preferred_element_type=F32)


def _dot_ta(a, b):
    return lax.dot_general(a, b, (((0,), (0,)), ((), ())), preferred_element_type=F32)


def _dot_split_lhs(m, x, parts):
    acc = None
    rem = x
    for i in range(parts):
        hi = _bf(rem)
        term = _dot(m, hi)
        acc = term if acc is None else acc + term
        if i + 1 < parts:
            rem = rem - hi.astype(F32)
    return acc


def _iota2(shape, dim):
    return lax.broadcasted_iota(jnp.int32, shape, dim)


def _div2(x, n):
    assert n & (n - 1) == 0
    return lax.shift_right_logical(x, n.bit_length() - 1)


def _mod2(x, n):
    assert n & (n - 1) == 0
    return lax.bitwise_and(x, n - 1)


def _head_ones():
    ri = _iota2((LANES, LANES), 0)
    ci = _iota2((LANES, LANES), 1)
    return jnp.where(_div2(ri, HEAD_DIM) == _div2(ci, HEAD_DIM), 1.0, 0.0).astype(BF16)


def _headsum(x, ones_bd):
    xb = _bf(x)
    return jnp.concatenate([_dot(xb[:, LANES * j:LANES * (j + 1)], ones_bd) for j in range(PAIRS)], axis=1)


def _head_norm(o, g, b, eps, ones_bd):
    mu = _headsum(o, ones_bd) * (1.0 / HEAD_DIM)
    d = o - mu
    var = _headsum(d * d, ones_bd) * (1.0 / HEAD_DIM)
    return d * lax.rsqrt(var + eps) * g + b


def _layer_norm(z, g, b):
    mu = jnp.mean(z, axis=-1, keepdims=True)
    d = z - mu
    var = jnp.mean(d * d, axis=-1, keepdims=True)
    return d * lax.rsqrt(var + LN_EPS) * g + b


def _bdiag(x, lane_lo):
    zero = jnp.zeros_like(x)
    return jnp.concatenate([jnp.where(lane_lo, x, zero), jnp.where(lane_lo, zero, x)], axis=0)


def _unstack_heads(x, lane_lo):
    return jnp.where(lane_lo, x[0:ROWS], x[ROWS:])


def _stream_masks(ct, width):
    ri = _iota2((N2, width), 0)
    ci = _mod2(_iota2((N2, width), 1), N2)
    same = _div2(ri, ct) == _div2(ci, ct)
    return same & (ci < ri), same & (ci <= ri), ri == ci, _div2(ri, ROWS) == _div2(ci, ROWS)


def _load_state(s_scr, s0_ref, nb):
    z = jnp.zeros((HEAD_DIM, HEAD_DIM), F32)
    for s in range(nb):
        for j in range(PAIRS):
            top = jnp.concatenate([s0_ref[s, 2 * j], z], axis=1)
            bot = jnp.concatenate([z, s0_ref[s, 2 * j + 1]], axis=1)
            s_scr[s, j] = jnp.concatenate([top, bot], axis=0)


def _store_state(sout_ref, s_scr, nb):
    for s in range(nb):
        for j in range(PAIRS):
            s2 = s_scr[s, j]
            sout_ref[s, 2 * j] = s2[0:HEAD_DIM, 0:HEAD_DIM]
            sout_ref[s, 2 * j + 1] = s2[HEAD_DIM:, HEAD_DIM:]


def _seq_rows(x, q, ct):
    if ct == ROWS:
        return x
    return jnp.concatenate([x[q * ct:(q + 1) * ct], x[ROWS + q * ct:ROWS + (q + 1) * ct]], axis=0)


def _from_seq_rows(pieces, ct):
    if ct == ROWS:
        return pieces[0]
    return jnp.concatenate([p[0:ct] for p in pieces] + [p[ct:2 * ct] for p in pieces], axis=0)


def _pad_rows(x, rows):
    if x.shape[0] == rows:
        return x
    return jnp.concatenate([x, jnp.zeros((rows - x.shape[0], x.shape[1]), x.dtype)], axis=0)


def _emit_by_level(*requests):
    tasks = []
    for steps, envs in requests:
        level = {}
        for name, deps, fn in steps:
            level[name] = 1 + max([level.get(d, 0) for d in deps], default=0)
            tasks += [(level[name], len(tasks) + i, env, name, deps, fn) for i, env in enumerate(envs)]
    for _, _, env, name, deps, fn in sorted(tasks, key=lambda t: t[:2]):
        env[name] = fn(*[env[d] for d in deps])


def _run_bodies(bodies):
    bodies = list(bodies)
    _emit_by_level(*[next(b) for b in bodies])
    for b in bodies:
        for request in b:
            _emit_by_level(request)


def _inverse_steps(nfac, eye):
    def advance(last):
        def fn(q, p):
            pb = _bf(p)
            if last:
                return p + _dot(q, pb), None
            x = _dot(q, jnp.concatenate([pb, q], axis=1))
            return p + x[:, 0:N2], _bf(x[:, N2:])
        return fn

    steps = [
        ("q0", ("l",), _bf),
        ("p1", ("l",), lambda l: l + eye()),
        ("q1", ("q0",), lambda q: _bf(_dot(q, q))),
    ]
    for i in range(1, nfac):
        steps.append((f"s{i}", (f"q{i}", f"p{i}"), advance(i == nfac - 1)))
        steps.append((f"p{i + 1}", (f"s{i}",), lambda s: s[0]))
        steps.append((f"q{i + 1}", (f"s{i}",), lambda s: s[1]))
    steps.append(("tb", (f"p{nfac}",), _bf))
    return steps


def _rwkv_score_steps(sc_mask):
    return [
        ("scm", ("a2", "r2", "bk"),
         lambda a2, r2, bk: _dot_tb(jnp.concatenate([a2, r2], axis=0), bk) * sc_mask()),
        ("l", ("scm",), lambda scm: scm[0:N2, 0:N2]),
        ("lkb", ("scm",), lambda scm: _bf(scm[:, N2:])),
        ("arb", ("scm",), lambda scm: _bf(scm[N2:, 0:N2])),
        ("lvark", ("lkb", "vst"), _dot),
    ]


def _rwkv_carried_steps(same_head):
    def f_ro(arb, tal, r2, lvark):
        x = _dot(arb, _bf(tal))
        return _bf(r2.astype(F32) + x[:, 0:N2]), x[:, N2:] + lvark[N2:]

    def f_mn(tal, b2, k2, vst):
        mm = _dot_ta(_bf(tal), b2)
        n2 = _dot_ta(vst, k2)
        return _bf(mm[0:N2]), (mm[N2:] + n2) * same_head()

    return [
        ("tal", ("tb", "a2", "lvark"),
         lambda tb, a2, lvark: _dot(tb, jnp.concatenate([a2, _bf(lvark[0:N2])], axis=1))),
        ("ro", ("arb", "tal", "r2", "lvark"), f_ro),
        ("mn", ("tal", "b2", "k2", "vst"), f_mn),
    ]


def _inproj_kernel(x_ref, w_ref, pa_ref, pb_ref):
    xb = _bf(x_ref[...])
    pa_ref[...] = _dot(xb, w_ref[:, 0:A_PROJ])
    pb_ref[...] = _dot(xb, w_ref[:, A_PROJ:])


def _inproj(x2d, w, tm):
    m = x2d.shape[0]
    return pl.pallas_call(
        _inproj_kernel,
        grid=(m // tm,),
        in_specs=[
            pl.BlockSpec((tm, D_MODEL), lambda i: (i, 0)),
            pl.BlockSpec(w.shape, lambda i: (0, 0), pipeline_mode=pl.Buffered(1)),
        ],
        out_specs=[
            pl.BlockSpec((tm, A_PROJ), lambda i: (i, 0)),
            pl.BlockSpec((tm, B_PROJ), lambda i: (i, 0)),
        ],
        out_shape=[
            jax.ShapeDtypeStruct((m, A_PROJ), F32),
            jax.ShapeDtypeStruct((m, B_PROJ), F32),
        ],
        compiler_params=pltpu.CompilerParams(
            dimension_semantics=("arbitrary",), vmem_limit_bytes=VMEM_LIMIT),
        name="inproj_ab",
    )(x2d, w)


def _rwkv_kernel(nb, tt, ct, pa_ref, shift_ref, s0_ref, mu_ref, w0_ref, w2p_ref, a0_ref, a2p_ref,
                 g2_ref, kk_ref, ka_ref, rk_ref, lng_ref, lnb_ref,
                 ya_ref, sout_ref,
                 s_scr, prev_scr, at_scr, rt_scr, bt_scr, kt_scr, v_scr, p_scr, o_scr,
                 scmask_scr, eye_scr, head_scr, lcum_scr):
    rows = nb * tt
    nchunk = rows // ROWS
    ns = ROWS // ct
    ti = pl.program_id(1)

    @pl.when((pl.program_id(0) == 0) & (ti == 0))
    def _():
        strict, incl, eye, same_head = _stream_masks(ct, 2 * N2)
        one = lambda m: jnp.where(m, 1.0, 0.0)
        scmask_scr[...] = jnp.concatenate([one(strict), one(incl)], axis=0)
        eye_scr[...] = one(eye)[:, 0:N2]
        head_scr[...] = one(same_head)[:, 0:N2]
        ri = _iota2(lcum_scr.shape, 0)
        ci = _iota2(lcum_scr.shape, 1)
        lcum_scr[...] = jnp.where((_div2(ri, ct) == _div2(ci, ct)) & (ci <= ri), 1.0, 0.0).astype(BF16)

    @pl.when(ti == 0)
    def _():
        _load_state(s_scr, s0_ref, nb)
        prev_scr[...] = shift_ref[...]

    ones_bd = _head_ones()

    p3 = pa_ref[...]
    p2 = p3.reshape(rows, A_PROJ)
    t3 = _iota2((nb, tt, A_PROJ), 1)
    rolled = pltpu.roll(p2, 1, axis=0).reshape(nb, tt, A_PROJ)
    pprev = jnp.where(t3 == 0, prev_scr[...], rolled).reshape(rows, A_PROJ)
    prev_scr[...] = p3[:, tt - 1:tt, :]
    m = p2 + (pprev - p2) * mu_ref[...]
    r = m[:, 0:WIDTH]
    k = m[:, WIDTH:2 * WIDTH]
    v = m[:, 2 * WIDTH:3 * WIDTH]
    wa = m[:, 3 * WIDTH:3 * WIDTH + LANES]
    gd = m[:, 3 * WIDTH + LANES:A_PROJ]
    z = -(w0_ref[...] + _dot(_bf(jnp.tanh(wa)), w2p_ref[...]))
    softplus = jnp.maximum(z, 0.0) + jnp.log(1.0 + jnp.exp(-jnp.abs(z)))
    lw = -jnp.exp(-softplus - 0.5)
    a = jax.nn.sigmoid(a0_ref[...] + _dot(_bf(wa), a2p_ref[...]))
    g = _dot(_bf(jax.nn.sigmoid(gd)), g2_ref[...])
    kk = k * kk_ref[...]
    k = k * (1.0 + (a - 1.0) * ka_ref[...])
    kk = kk * lax.rsqrt(jnp.maximum(_headsum(kk * kk, ones_bd), 1e-24))
    bonus = _headsum(r * k * rk_ref[...], ones_bd) * v

    span = lcum_scr.shape[0]
    cs = jnp.concatenate([_dot_split_lhs(lcum_scr[...], lw[i * span:(i + 1) * span], 2)
                          for i in range(rows // span)], axis=0)
    pinv = jnp.exp(-cs)
    p_scr[...] = jnp.exp(cs)
    at_scr[...] = _bf(-kk * jnp.exp(cs - lw))
    rt_scr[...] = _bf(r * p_scr[...])
    bt_scr[...] = _bf(kk * a * pinv)
    kt_scr[...] = _bf(k * pinv)
    v_scr[...] = _bf(v)

    lane_lo = _iota2((ROWS, LANES), 1) < HEAD_DIM
    sc_mask = lambda: scmask_scr[...]
    eye = lambda: eye_scr[...]
    same_head = lambda: head_scr[...]
    nfac = ct.bit_length() - 1

    envs = []
    for c in range(nchunk):
        for j in range(PAIRS):
            rsl = slice(c * ROWS, (c + 1) * ROWS)
            lsl = slice(LANES * j, LANES * (j + 1))
            aj, rj, bj, kj, vj = (s[rsl, lsl] for s in (at_scr, rt_scr, bt_scr, kt_scr, v_scr))
            envs.append({
                "c": c, "j": j, "rsl": rsl, "lsl": lsl,
                "a2": _bdiag(aj, lane_lo), "r2": _bdiag(rj, lane_lo),
                "b2": _bdiag(bj, lane_lo), "k2": _bdiag(kj, lane_lo),
                "bk": jnp.concatenate([bj, bj, kj, kj], axis=0),
                "vst": jnp.concatenate([vj, vj], axis=0),
            })

    steps = _rwkv_score_steps(sc_mask) + _inverse_steps(nfac, eye)
    if ns == 1:
        yield steps + _rwkv_carried_steps(same_head), envs
        per_seq = tt // ROWS
        state = [[s_scr[s, j] for j in range(PAIRS)] for s in range(nb)]
        for cpos in range(per_seq):
            for s in range(nb):
                c = s * per_seq + cpos
                cenv = envs[c * PAIRS:(c + 1) * PAIRS]
                sb = [_bf(x) for x in state[s]]
                for j, env in enumerate(cenv):
                    rp, oc = env["ro"]
                    o_scr[env["rsl"], env["lsl"]] = _unstack_heads(_dot_tb(rp, sb[j]) + oc, lane_lo)
                for j, env in enumerate(cenv):
                    mk, nn = env["mn"]
                    pc = p_scr[(c + 1) * ROWS - 1:(c + 1) * ROWS, env["lsl"]]
                    state[s][j] = (state[s][j] + _dot(sb[j], mk) + nn) * pc
        for s in range(nb):
            for j in range(PAIRS):
                s_scr[s, j] = state[s][j]
    else:
        for env in envs:
            env["sb"] = [_bf(s_scr[env["c"] * ns + q, env["j"]]) for q in range(ns)]

        def f_h(a2, r2, sb):
            pieces = [_dot_tb(jnp.concatenate([_seq_rows(a2, q, ct), _seq_rows(r2, q, ct)], axis=0), sb[q])
                      for q in range(ns)]
            return (_from_seq_rows([x[0:2 * ct] for x in pieces], ct),
                    _from_seq_rows([x[2 * ct:] for x in pieces], ct))

        def f_ds(u2, vst, b2, k2):
            vst32 = vst.astype(F32)
            out = []
            for q in range(ns):
                uv = jnp.concatenate([_seq_rows(u2, q, ct), _seq_rows(vst32, q, ct)], axis=0)
                bk = jnp.concatenate([_seq_rows(b2, q, ct), _seq_rows(k2, q, ct)], axis=0)
                out.append(_dot_ta(_bf(_pad_rows(uv, LANES)), _pad_rows(bk, LANES)) * same_head())
            return out

        steps += [
            ("h", ("a2", "r2", "sb"), f_h),
            ("u2", ("tb", "h", "lvark"), lambda tb, h, lvark: _dot(tb, _bf(h[0] + lvark[0:N2]))),
            ("o2", ("arb", "u2", "h", "lvark"),
             lambda arb, u2, h, lvark: h[1] + _dot(arb, _bf(u2)) + lvark[N2:]),
            ("ds", ("u2", "vst", "b2", "k2"), f_ds),
        ]
        yield steps, envs
        for env in envs:
            o_scr[env["rsl"], env["lsl"]] = _unstack_heads(env["o2"], lane_lo)
            for q in range(ns):
                seq = env["c"] * ns + q
                pc = p_scr[(seq + 1) * ct - 1:(seq + 1) * ct, env["lsl"]]
                s_scr[seq, env["j"]] = (s_scr[seq, env["j"]] + env["ds"][q]) * pc

    o = o_scr[...]
    y = (_head_norm(o, lng_ref[...], lnb_ref[...], A_GN_EPS, ones_bd) + bonus) * g
    ya_ref[...] = _bf(y).reshape(nb, tt, WIDTH)

    @pl.when(ti == pl.num_programs(1) - 1)
    def _():
        _store_state(sout_ref, s_scr, nb)


def _row_spec(shape):
    return pl.BlockSpec(shape, lambda b, t: (0,) * len(shape))


def _check_mixer_blocking(bsz, tlen, nb, tt):
    ct = min(ROWS, tt)
    rows = nb * tt
    assert rows % ROWS == 0 and ROWS % ct == 0 and bsz % nb == 0 and tlen % tt == 0
    assert tt % ROWS == 0 or tt == tlen
    return ct, rows


def _rwkv(pa3, shift0, s0, prm, nb, tt):
    bsz, tlen, _ = pa3.shape
    ct, rows = _check_mixer_blocking(bsz, tlen, nb, tt)
    span = tt if tt % ROWS == 0 else rows
    grid = (bsz // nb, tlen // tt)
    state_spec = pl.BlockSpec((nb, HEADS, HEAD_DIM, HEAD_DIM), lambda b, t: (b, 0, 0, 0))
    vec = lambda n: _row_spec((1, n))
    in_specs = [
        pl.BlockSpec((nb, tt, A_PROJ), lambda b, t: (b, t, 0)),
        pl.BlockSpec((nb, 1, A_PROJ), lambda b, t: (b, 0, 0)),
        state_spec,
        vec(A_PROJ), vec(WIDTH), _row_spec((LANES, WIDTH)), vec(WIDTH), _row_spec((LANES, WIDTH)),
        _row_spec((GATE_LORA, WIDTH)), vec(WIDTH), vec(WIDTH), vec(WIDTH), vec(WIDTH), vec(WIDTH),
    ]
    out_specs = [
        pl.BlockSpec((nb, tt, WIDTH), lambda b, t: (b, t, 0)),
        state_spec,
    ]
    scratch = [
        pltpu.VMEM((nb, PAIRS, LANES, LANES), F32),
        pltpu.VMEM((nb, 1, A_PROJ), F32),
        pltpu.VMEM((rows, WIDTH), BF16),
        pltpu.VMEM((rows, WIDTH), BF16),
        pltpu.VMEM((rows, WIDTH), BF16),
        pltpu.VMEM((rows, WIDTH), BF16),
        pltpu.VMEM((rows, WIDTH), BF16),
        pltpu.VMEM((rows, WIDTH), F32),
        pltpu.VMEM((rows, WIDTH), F32),
        pltpu.VMEM((2 * N2, 2 * N2), F32),
        pltpu.VMEM((N2, N2), F32),
        pltpu.VMEM((N2, N2), F32),
        pltpu.VMEM((span, span), BF16),
    ]
    return {
        "body": functools.partial(_rwkv_kernel, nb, tt, ct), "grid": grid,
        "in_specs": in_specs, "out_specs": out_specs, "scratch": scratch,
        "out_shape": [jax.ShapeDtypeStruct((bsz, tlen, WIDTH), BF16), jax.ShapeDtypeStruct(s0.shape, F32)],
        "args": (pa3, shift0.reshape(bsz, 1, A_PROJ), s0, *prm),
    }


def _mixers(parts):
    counts = [(len(p["in_specs"]), len(p["out_specs"]), len(p["scratch"])) for p in parts]

    def kernel(*refs):
        groups = []
        pos = 0
        for kind in range(3):
            for i, c in enumerate(counts):
                groups.append(refs[pos:pos + c[kind]])
                pos += c[kind]
        n = len(parts)
        _run_bodies([p["body"](*groups[i], *groups[n + i], *groups[2 * n + i]) for i, p in enumerate(parts)])

    flat = lambda key: [x for p in parts for x in p[key]]
    outs = pl.pallas_call(
        kernel,
        grid=parts[0]["grid"],
        in_specs=flat("in_specs"),
        out_specs=flat("out_specs"),
        out_shape=flat("out_shape"),
        scratch_shapes=flat("scratch"),
        compiler_params=pltpu.CompilerParams(
            dimension_semantics=("arbitrary", "arbitrary"), vmem_limit_bytes=VMEM_LIMIT),
        name="mixers",
    )(*flat("args"))
    result, pos = [], 0
    for _, n_out, _ in counts:
        result.append(outs[pos:pos + n_out])
        pos += n_out
    return result


def _ret_kernel(nb, tt, ct, pos0, pb_ref, s0_ref, invf_ref, gng_ref, gnb_ref,
                yb_ref, sout_ref,
                s_scr, q_scr, k_scr, v_scr, o_scr, dec_scr, head_scr, trig_scr):
    rows = nb * tt
    nchunk = rows // ROWS
    ns = ROWS // ct
    ti = pl.program_id(1)

    @pl.when((pl.program_id(0) == 0) & (ti == 0))
    def _():
        _, incl, _, same_head = _stream_masks(ct, N2)
        head_scr[...] = jnp.where(same_head, 1.0, 0.0)
        ri = _iota2((N2, N2), 0)
        ci = _iota2((N2, N2), 1)
        tpos = _mod2(ri, ct).astype(F32)
        tdiff = (_mod2(ri, ct) - _mod2(ci, ct)).astype(F32)
        for j in range(PAIRS):
            head = (2 * j + _div2(ri, ROWS)).astype(F32)
            lg = jnp.log1p(-jnp.exp2(-5.0 - head))
            dec_scr[j, 0] = jnp.where(incl, jnp.exp(lg * jnp.maximum(tdiff, 0.0)), 0.0)
            dec_scr[j, 1] = jnp.exp(lg * (tpos + 1.0))
            dec_scr[j, 2] = jnp.exp(lg * (ct - 1.0 - tpos))
            dec_scr[j, 3] = jnp.exp(lg * ct)
        off = _mod2(_iota2((rows, LANES), 0), tt).astype(F32) * invf_ref[...]
        trig_scr[0] = jnp.cos(off)
        trig_scr[1] = jnp.sin(off)

    @pl.when(ti == 0)
    def _():
        _load_state(s_scr, s0_ref, nb)

    ones_bd = _head_ones()
    pb = pb_ref[...].reshape(rows, B_PROJ)
    q = pb[:, 0:WIDTH]
    k = pb[:, WIDTH:2 * WIDTH]
    gate = pb[:, 3 * WIDTH:]

    base = jnp.zeros((SUBLANES, LANES), F32) + (pos0 + ti * tt).astype(F32)
    ang = base * invf_ref[...]
    cos_a, sin_a = jnp.cos(ang)[0:1], jnp.sin(ang)[0:1]
    cos_b, sin_b = trig_scr[0], trig_scr[1]
    cos = jnp.concatenate([cos_a * cos_b - sin_a * sin_b] * PAIRS, axis=1)
    sin = jnp.concatenate([sin_a * cos_b + cos_a * sin_b] * PAIRS, axis=1)
    first_half = _mod2(_iota2((rows, WIDTH), 1), HEAD_DIM) < (HEAD_DIM // 2)
    sin = jnp.where(first_half, -sin, sin)

    pr = _iota2((LANES, LANES), 0)
    pc = _iota2((LANES, LANES), 1)
    swap = jnp.where(pr == lax.bitwise_xor(pc, HEAD_DIM // 2), 1.0, 0.0).astype(BF16)

    def rope(x):
        xb = _bf(x)
        partner = jnp.concatenate([_dot(xb[:, LANES * j:LANES * (j + 1)], swap) for j in range(PAIRS)], axis=1)
        return x * cos + partner * sin

    q_scr[...] = _bf(rope(q))
    k_scr[...] = rope(k) * (HEAD_DIM ** -0.5)
    v_scr[...] = _bf(pb[:, 2 * WIDTH:3 * WIDTH])

    lane_lo = _iota2((ROWS, LANES), 1) < HEAD_DIM
    same_head = lambda: head_scr[...]

    envs = []
    for c in range(nchunk):
        for j in range(PAIRS):
            rsl = slice(c * ROWS, (c + 1) * ROWS)
            lsl = slice(LANES * j, LANES * (j + 1))
            qj, kj, vj = q_scr[rsl, lsl], k_scr[rsl, lsl], v_scr[rsl, lsl]
            kjb = _bf(kj)
            envs.append({
                "c": c, "j": j, "rsl": rsl, "lsl": lsl,
                "q2": _bdiag(qj, lane_lo),
                "kst": jnp.concatenate([kjb, kjb], axis=0),
                "k2d": _bdiag(kj, lane_lo) * dec_scr[j, 2],
                "vst": jnp.concatenate([vj, vj], axis=0),
                "dmask": lambda j=j: dec_scr[j, 0],
            })

    def f_ds(k2d, vst):
        return [_dot_ta(_bf(_pad_rows(_seq_rows(k2d, s, ct), LANES)), _pad_rows(_seq_rows(vst, s, ct), LANES))
                * same_head() for s in range(ns)]

    steps = [
        ("sc", ("q2", "kst", "dmask"), lambda q2, kst, dmask: _bf(_dot_tb(q2, kst) * dmask())),
        ("inner", ("sc", "vst"), _dot),
        ("ds", ("k2d", "vst"), f_ds),
    ]
    yield steps, envs

    if ns == 1:
        per_seq = tt // ROWS
        for s in range(nb):
            state = [s_scr[s, j] for j in range(PAIRS)]
            for c in range(s * per_seq, (s + 1) * per_seq):
                for j in range(PAIRS):
                    env = envs[c * PAIRS + j]
                    env["sb"] = [_bf(state[j])]
                    state[j] = state[j] * dec_scr[j, 3] + env["ds"][0]
            for j in range(PAIRS):
                s_scr[s, j] = state[j]
    else:
        for env in envs:
            j = env["j"]
            first = env["c"] * ns
            env["sb"] = [_bf(s_scr[first + s, j]) for s in range(ns)]
            for s in range(ns):
                s_scr[first + s, j] = s_scr[first + s, j] * dec_scr[j, 3] + env["ds"][s]

    def f_cross(q2, sb):
        return _from_seq_rows([_dot(_seq_rows(q2, s, ct), sb[s]) for s in range(ns)], ct)

    yield [("cross", ("q2", "sb"), f_cross)], envs
    for env in envs:
        o2 = env["inner"] + env["cross"] * dec_scr[env["j"], 1]
        o_scr[env["rsl"], env["lsl"]] = _unstack_heads(o2, lane_lo)

    o = o_scr[...]
    y = jax.nn.silu(gate) * _head_norm(o, gng_ref[...], gnb_ref[...], B_GN_EPS, ones_bd)
    yb_ref[...] = _bf(y).reshape(nb, tt, WIDTH)

    @pl.when(ti == pl.num_programs(1) - 1)
    def _():
        _store_state(sout_ref, s_scr, nb)


def _retention(pb3, s0, invf, gn_g, gn_b, nb, tt, pos0):
    bsz, tlen, _ = pb3.shape
    ct, rows = _check_mixer_blocking(bsz, tlen, nb, tt)
    grid = (bsz // nb, tlen // tt)
    state_spec = pl.BlockSpec((nb, HEADS, HEAD_DIM, HEAD_DIM), lambda b, t: (b, 0, 0, 0))
    in_specs = [
        pl.BlockSpec((nb, tt, B_PROJ), lambda b, t: (b, t, 0)),
        state_spec,
        _row_spec((1, LANES)), _row_spec((1, WIDTH)), _row_spec((1, WIDTH)),
    ]
    out_specs = [
        pl.BlockSpec((nb, tt, WIDTH), lambda b, t: (b, t, 0)),
        state_spec,
    ]
    scratch = [
        pltpu.VMEM((nb, PAIRS, LANES, LANES), F32),
        pltpu.VMEM((rows, WIDTH), BF16),
        pltpu.VMEM((rows, WIDTH), F32),
        pltpu.VMEM((rows, WIDTH), BF16),
        pltpu.VMEM((rows, WIDTH), F32),
        pltpu.VMEM((PAIRS, 4, N2, N2), F32),
        pltpu.VMEM((N2, N2), F32),
        pltpu.VMEM((2, rows, LANES), F32),
    ]
    return {
        "body": functools.partial(_ret_kernel, nb, tt, ct, pos0), "grid": grid,
        "in_specs": in_specs, "out_specs": out_specs, "scratch": scratch,
        "out_shape": [jax.ShapeDtypeStruct((bsz, tlen, WIDTH), BF16), jax.ShapeDtypeStruct(s0.shape, F32)],
        "args": (pb3, s0, invf, gn_g, gn_b),
    }


def _conv_kernel(nb, tt, nsplit, x_ref, buf_ref, win_ref, cw_ref, z_ref, bout_ref, halo_scr):
    rows = nb * tt
    sub = rows // nsplit
    ti = pl.program_id(1)

    @pl.when(ti == 0)
    def _():
        halo_scr[...] = buf_ref[...]

    proj = []
    for i in range(nsplit):
        xi = x_ref[0, i * sub:(i + 1) * sub, :] if nb == 1 else x_ref[...].reshape(rows, D_MODEL)
        xb = _bf(xi)
        bg = _dot(xb, win_ref[:, 0:D_MODEL])
        u = _dot(xb, win_ref[:, D_MODEL:2 * D_MODEL]) * _dot(xb, win_ref[:, 2 * D_MODEL:])
        proj.append((bg, u))

    cw = cw_ref[...]
    st = tt // nsplit if nb == 1 else tt
    t3 = _iota2((nb, st, D_MODEL), 1)
    h0 = halo_scr[:, 0:1, :]
    h1 = halo_scr[:, 1:2, :]
    for i, (bg, u) in enumerate(proj):
        u3 = u.reshape(nb, st, D_MODEL)
        prev1 = jnp.where(t3 == 0, h1, pltpu.roll(u, 1, axis=0).reshape(nb, st, D_MODEL))
        prev2 = jnp.where(t3 == 0, h0,
                          jnp.where(t3 == 1, h1, pltpu.roll(u, 2, axis=0).reshape(nb, st, D_MODEL)))
        conv = prev2 * cw[0:1, :] + prev1 * cw[1:2, :] + u3 * cw[2:3, :]
        z_ref[i * sub:(i + 1) * sub, :] = _bf(bg * conv.reshape(sub, D_MODEL))
        h0 = u3[:, st - 2:st - 1, :]
        h1 = u3[:, st - 1:st, :]
    halo_scr[:, 0:1, :] = h0
    halo_scr[:, 1:2, :] = h1

    @pl.when(ti == pl.num_programs(1) - 1)
    def _():
        bout_ref[...] = halo_scr[...]


def _conv_mixer(x3, buf0, win, cw, nb, tt, nsplit):
    bsz, tlen, _ = x3.shape
    rows = nb * tt
    assert bsz % nb == 0 and tlen % tt == 0
    assert nsplit == 1 or nb == 1
    assert tt % nsplit == 0 and tt // nsplit >= CONV_W - 1
    grid = (bsz // nb, tlen // tt)
    buf_spec = pl.BlockSpec((nb, CONV_W - 1, D_MODEL), lambda b, t: (b, 0, 0))
    return pl.pallas_call(
        functools.partial(_conv_kernel, nb, tt, nsplit),
        grid=grid,
        in_specs=[
            pl.BlockSpec((nb, tt, D_MODEL), lambda b, t: (b, t, 0)),
            buf_spec,
            pl.BlockSpec(win.shape, lambda b, t: (0, 0), pipeline_mode=pl.Buffered(1)),
            _row_spec((CONV_W, D_MODEL)),
        ],
        out_specs=[
            pl.BlockSpec((rows, D_MODEL), lambda b, t: (b * (tlen // tt) + t, 0)),
            buf_spec,
        ],
        out_shape=[
            jax.ShapeDtypeStruct((bsz * tlen, D_MODEL), BF16),
            jax.ShapeDtypeStruct(buf0.shape, F32),
        ],
        scratch_shapes=[pltpu.VMEM((nb, CONV_W - 1, D_MODEL), F32)],
        compiler_params=pltpu.CompilerParams(
            dimension_semantics=("arbitrary", "arbitrary"), vmem_limit_bytes=VMEM_LIMIT),
        name="conv_mixer",
    )(x3, buf0, win, cw)


FF_CHUNK = 1024


def _post_kernel(npieces, nsplit, *refs):
    x_ref = refs[0]
    y_refs = refs[1:1 + npieces]
    wout_ref, g1_ref, b1_ref, wup_ref, wdown_ref, g2_ref, b2_ref, out_ref = refs[1 + npieces:]
    sub = x_ref.shape[0] // nsplit

    def f_y(r):
        y = None
        off = 0
        for y_ref in y_refs:
            n = y_ref.shape[1]
            term = _dot(y_ref[r, :], wout_ref[off:off + n, :])
            y = term if y is None else y + term
            off += n
        return y

    def f_mlp(f):
        def step(x1b, acc):
            h = jnp.maximum(_dot(x1b, wup_ref[:, f * FF_CHUNK:(f + 1) * FF_CHUNK]), 0.0)
            term = _dot(_bf(h * h), wdown_ref[f * FF_CHUNK:(f + 1) * FF_CHUNK, :])
            return term if acc is None else acc + term
        return step

    def f_out(r, x1, acc):
        out_ref[r, :] = _layer_norm(ALPHA * x1 + acc, g2_ref[...], b2_ref[...])
        return None

    steps = [
        ("y", ("r",), f_y),
        ("x1", ("r", "y"), lambda r, y: _layer_norm(ALPHA * x_ref[r, :] + y, g1_ref[...], b1_ref[...])),
        ("x1b", ("x1",), _bf),
        ("acc-1", ("x1b",), lambda x1b: None),
    ]
    nff = D_FF // FF_CHUNK
    for f in range(nff):
        steps.append((f"acc{f}", ("x1b", f"acc{f - 1}"), f_mlp(f)))
    steps.append(("out", ("r", "x1", f"acc{nff - 1}"), f_out))
    envs = [{"r": slice(i * sub, (i + 1) * sub)} for i in range(nsplit)]
    _emit_by_level((steps, envs))


def _post(x2d, ys, wout, g1, b1, wup, wdown, g2, b2, layer, tm, nsplit):
    m = x2d.shape[0]
    assert m % tm == 0 and tm % nsplit == 0
    const = lambda i: (0, 0)
    resident = lambda a: pl.BlockSpec(a.shape, const, pipeline_mode=pl.Buffered(1))
    of_layer = lambda a: pl.BlockSpec((None,) + a.shape[1:], lambda i: (layer, 0, 0),
                                      pipeline_mode=pl.Buffered(1))
    vec = pl.BlockSpec((1, D_MODEL), const)
    in_specs = [pl.BlockSpec((tm, D_MODEL), lambda i: (i, 0))]
    in_specs += [pl.BlockSpec((tm, y.shape[1]), lambda i: (i, 0)) for y in ys]
    in_specs += [resident(wout), vec, vec, of_layer(wup), of_layer(wdown), vec, vec]
    return pl.pallas_call(
        functools.partial(_post_kernel, len(ys), nsplit),
        grid=(m // tm,),
        in_specs=in_specs,
        out_specs=pl.BlockSpec((tm, D_MODEL), lambda i: (i, 0)),
        out_shape=jax.ShapeDtypeStruct((m, D_MODEL), F32),
        compiler_params=pltpu.CompilerParams(
            dimension_semantics=("arbitrary",), vmem_limit_bytes=VMEM_LIMIT),
        name="post_block",
    )(x2d, *ys, wout, g1, b1, wup, wdown, g2, b2)


def _trunk(x, st_shift, st_wkv, st_ret, st_conv, pos0, blk, w):
    bsz, tlen, _ = x.shape
    m = bsz * tlen
    x2d = x.reshape(m, D_MODEL)
    row = lambda a: a.reshape(1, -1)

    pa, pb = _inproj(x2d, w["w_in_ab"], blk["tm_in"])
    pa3 = pa.reshape(bsz, tlen, A_PROJ)
    rw_prm = (row(w["mu_a"]), row(w["w0"]), w["w2p"], row(w["a0"]), w["a2p"], w["g2"],
              row(w["k_k"]), row(w["k_a"]), row(w["r_k"]), row(w["lnx_g"]), row(w["lnx_b"]))
    (yb, ret1), (ya, wkv1) = _mixers([
        _retention(pb.reshape(bsz, tlen, B_PROJ), st_ret, w["invf"], row(w["gn_g"]), row(w["gn_b"]),
                   blk["nb_mix"], blk["tt_mix"], pos0),
        _rwkv(pa3, st_shift, st_wkv, rw_prm, blk["nb_mix"], blk["tt_mix"]),
    ])
    shift1 = pa3[:, tlen - 1, :]
    x2d = _post(x2d, [ya.reshape(m, WIDTH), yb.reshape(m, WIDTH)], w["w_out_ab"], row(w["ln1_g"][0]), row(w["ln1_b"][0]),
                w["w_up"], w["w_down"], row(w["ln2_g"][0]), row(w["ln2_b"][0]),
                0, blk["tm_post"], blk["split"])

    z, conv1 = _conv_mixer(x2d.reshape(bsz, tlen, D_MODEL), st_conv, w["w_in_conv"], w["conv_w"],
                           blk["nb_conv"], blk["tt_conv"], blk["split_conv"])
    x2d = _post(x2d, [z], w["w_out_conv"], row(w["ln1_g"][1]), row(w["ln1_b"][1]),
                w["w_up"], w["w_down"], row(w["ln2_g"][1]), row(w["ln2_b"][1]),
                1, blk["tm_post"], blk["split"])
    return x2d.reshape(bsz, tlen, D_MODEL), shift1[None], wkv1[None], ret1[None], conv1[None]


def kernel(x_prompt, x_sample, state_shift, state_wkv, state_ret, state_conv, w_in_ab, mu_a, w0, w2, a0, a2,
           g2, k_k, k_a, r_k, lnx_g, lnx_b, gn_g, gn_b, w_out_ab, w_in_conv, conv_w, w_out_conv,
           ln1_g, ln1_b, ln2_g, ln2_b, w_up, w_down):
    bp, tp, _ = x_prompt.shape
    half = HEAD_DIM // 2
    inv = ROPE_BASE ** (-jnp.arange(half, dtype=F32) / half)
    zpad = jnp.zeros((LANES - DECAY_LORA, WIDTH), F32)
    w = {
        "w_in_ab": _bf(w_in_ab[0]),
        "mu_a": mu_a[0], "w0": w0[0], "a0": a0[0],
        "w2p": _bf(jnp.concatenate([w2[0], zpad], axis=0)),
        "a2p": _bf(jnp.concatenate([zpad, a2[0]], axis=0)),
        "g2": _bf(g2[0]), "k_k": k_k[0], "k_a": k_a[0], "r_k": r_k[0],
        "lnx_g": lnx_g[0], "lnx_b": lnx_b[0], "gn_g": gn_g[0], "gn_b": gn_b[0],
        "invf": jnp.tile(inv, LANES // half).reshape(1, LANES),
        "w_out_ab": _bf(w_out_ab[0]), "w_in_conv": _bf(w_in_conv[0]), "conv_w": conv_w[0],
        "w_out_conv": _bf(w_out_conv[0]),
        "ln1_g": ln1_g, "ln1_b": ln1_b, "ln2_g": ln2_g, "ln2_b": ln2_b,
        "w_up": _bf(w_up), "w_down": _bf(w_down),
    }
    dt = state_wkv.dtype
    z_shift = jnp.zeros((bp, A_PROJ), dt)
    z_state = jnp.zeros((bp, HEADS, HEAD_DIM, HEAD_DIM), dt)
    z_conv = jnp.zeros((bp, CONV_W - 1, D_MODEL), dt)
    blk_p = {"tm_in": 1024, "tm_post": 1024, "split": 4, "nb_mix": 2, "tt_mix": 256,
             "nb_conv": 1, "tt_conv": 1024, "split_conv": 2}
    y_p, p_shift, p_wkv, p_ret, p_conv = _trunk(x_prompt, z_shift, z_state, z_state, z_conv, 0, blk_p, w)
    ts = x_sample.shape[1]
    blk_s = {"tm_in": 512, "tm_post": 1024, "split": 4, "nb_mix": 2 * ROWS // ts, "tt_mix": ts,
             "nb_conv": 256 // ts, "tt_conv": ts, "split_conv": 1}
    y_s, s_shift, s_wkv, s_ret, s_conv = _trunk(x_sample, state_shift[0], state_wkv[0], state_ret[0],
                                                state_conv[0], PAST_LEN, blk_s, w)
    return (y_p, y_s, p_shift, p_wkv, p_ret, p_conv, s_shift, s_wkv, s_ret, s_conv)
```

```python
import functools

import jax
import jax.numpy as jnp
from jax import lax
from jax.experimental import pallas as pl
from jax.experimental.pallas import tpu as pltpu

F32 = jnp.float32
BF16 = jnp.bfloat16

D_MODEL = 1024
DEPTH = 2
PAST_LEN = 16384
HEADS = 8
HEAD_DIM = 64
WIDTH = HEADS * HEAD_DIM
DECAY_LORA = 64
AAA_LORA = 64
GATE_LORA = 128
A_PROJ = 3 * WIDTH + DECAY_LORA + AAA_LORA + GATE_LORA
B_PROJ = 4 * WIDTH
A_GN_EPS = 64e-5
B_GN_EPS = 1e-5
ROPE_BASE = 10000.0
CONV_W = 3
D_FF = 4 * D_MODEL
LN_EPS = 1e-5
ALPHA = (2.0 * DEPTH) ** 0.25

LANES = 128
SUBLANES = 8
PAIRS = WIDTH // LANES
ROWS = 64
N2 = 2 * ROWS
VMEM_LIMIT = 56 * 1024 * 1024


def _bf(x):
    return x.astype(BF16)


def _dot(a, b):
    return jnp.dot(a, b, preferred_element_type=F32)


def _dot_tb(a, b):
    return lax.dot_general(a, b, (((1,), (1,)), ((), ())), preferred_element_type=F32)


def _dot_ta(a, b):
    return lax.dot_general(a, b, (((0,), (0,)), ((), ())), preferred_element_type=F32)


def _dot_split_lhs(m, x, parts):
    acc = None
    rem = x
    for i in range(parts):
        hi = _bf(rem)
        term = _dot(m, hi)
        acc = term if acc is None else acc + term
        if i + 1 < parts:
            rem = rem - hi.astype(F32)
    return acc


def _iota2(shape, dim):
    return lax.broadcasted_iota(jnp.int32, shape, dim)


def _div2(x, n):
    assert n & (n - 1) == 0
    return lax.shift_right_logical(x, n.bit_length() - 1)


def _mod2(x, n):
    assert n & (n - 1) == 0
    return lax.bitwise_and(x, n - 1)


def _head_ones():
    ri = _iota2((LANES, LANES), 0)
    ci = _iota2((LANES, LANES), 1)
    return jnp.where(_div2(ri, HEAD_DIM) == _div2(ci, HEAD_DIM), 1.0, 0.0).astype(BF16)


def _headsum(x, ones_bd):
    xb = _bf(x)
    return jnp.concatenate([_dot(xb[:, LANES * j:LANES * (j + 1)], ones_bd) for j in range(PAIRS)], axis=1)


def _head_norm(o, g, b, eps, ones_bd):
    mu = _headsum(o, ones_bd) * (1.0 / HEAD_DIM)
    d = o - mu
    var = _headsum(d * d, ones_bd) * (1.0 / HEAD_DIM)
    return d * lax.rsqrt(var + eps) * g + b


def _layer_norm(z, g, b):
    mu = jnp.mean(z, axis=-1, keepdims=True)
    d = z - mu
    var = jnp.mean(d * d, axis=-1, keepdims=True)
    return d * lax.rsqrt(var + LN_EPS) * g + b


def _bdiag(x, lane_lo):
    zero = jnp.zeros_like(x)
    return jnp.concatenate([jnp.where(lane_lo, x, zero), jnp.where(lane_lo, zero, x)], axis=0)


def _unstack_heads(x, lane_lo):
    return jnp.where(lane_lo, x[0:ROWS], x[ROWS:])


def _stream_masks(ct, width):
    ri = _iota2((N2, width), 0)
    ci = _mod2(_iota2((N2, width), 1), N2)
    same = _div2(ri, ct) == _div2(ci, ct)
    return same & (ci < ri), same & (ci <= ri), ri == ci, _div2(ri, ROWS) == _div2(ci, ROWS)


def _load_state(s_scr, s0_ref, nb):
    z = jnp.zeros((HEAD_DIM, HEAD_DIM), F32)
    for s in range(nb):
        for j in range(PAIRS):
            top = jnp.concatenate([s0_ref[s, 2 * j], z], axis=1)
            bot = jnp.concatenate([z, s0_ref[s, 2 * j + 1]], axis=1)
            s_scr[s, j] = jnp.concatenate([top, bot], axis=0)


def _store_state(sout_ref, s_scr, nb):
    for s in range(nb):
        for j in range(PAIRS):
            s2 = s_scr[s, j]
            sout_ref[s, 2 * j] = s2[0:HEAD_DIM, 0:HEAD_DIM]
            sout_ref[s, 2 * j + 1] = s2[HEAD_DIM:, HEAD_DIM:]


def _seq_rows(x, q, ct):
    if ct == ROWS:
        return x
    return jnp.concatenate([x[q * ct:(q + 1) * ct], x[ROWS + q * ct:ROWS + (q + 1) * ct]], axis=0)


def _from_seq_rows(pieces, ct):
    if ct == ROWS:
        return pieces[0]
    return jnp.concatenate([p[0:ct] for p in pieces] + [p[ct:2 * ct] for p in pieces], axis=0)


def _pad_rows(x, rows):
    if x.shape[0] == rows:
        return x
    return jnp.concatenate([x, jnp.zeros((rows - x.shape[0], x.shape[1]), x.dtype)], axis=0)


def _emit_by_level(steps, envs):
    level = {}
    for name, deps, _ in steps:
        level[name] = 1 + max([level.get(d, 0) for d in deps], default=0)
    for lv in sorted(set(level.values())):
        for name, deps, fn in steps:
            if level[name] == lv:
                for env in envs:
                    env[name] = fn(*[env[d] for d in deps])


def _inverse_steps(nfac, eye):
    def advance(last):
        def fn(q, p):
            pb = _bf(p)
            if last:
                return p + _dot(q, pb), None
            x = _dot(q, jnp.concatenate([pb, q], axis=1))
            return p + x[:, 0:N2], _bf(x[:, N2:])
        return fn

    steps = [
        ("q0", ("l",), _bf),
        ("p1", ("l",), lambda l: l + eye()),
        ("q1", ("q0",), lambda q: _bf(_dot(q, q))),
    ]
    for i in range(1, nfac):
        steps.append((f"s{i}", (f"q{i}", f"p{i}"), advance(i == nfac - 1)))
        steps.append((f"p{i + 1}", (f"s{i}",), lambda s: s[0]))
        steps.append((f"q{i + 1}", (f"s{i}",), lambda s: s[1]))
    steps.append(("tb", (f"p{nfac}",), _bf))
    return steps


def _rwkv_score_steps(sc_mask):
    return [
        ("scm", ("a2", "r2", "bk"),
         lambda a2, r2, bk: _dot_tb(jnp.concatenate([a2, r2], axis=0), bk) * sc_mask()),
        ("l", ("scm",), lambda scm: scm[0:N2, 0:N2]),
        ("lkb", ("scm",), lambda scm: _bf(scm[:, N2:])),
        ("arb", ("scm",), lambda scm: _bf(scm[N2:, 0:N2])),
        ("lvark", ("lkb", "vst"), _dot),
    ]


def _rwkv_carried_steps(same_head):
    def f_ro(arb, tal, r2, lvark):
        x = _dot(arb, _bf(tal))
        return _bf(r2.astype(F32) + x[:, 0:N2]), x[:, N2:] + lvark[N2:]

    def f_mn(tal, b2, k2, vst):
        mm = _dot_ta(_bf(tal), b2)
        n2 = _dot_ta(vst, k2)
        return _bf(mm[0:N2]), (mm[N2:] + n2) * same_head()

    return [
        ("tal", ("tb", "a2", "lvark"),
         lambda tb, a2, lvark: _dot(tb, jnp.concatenate([a2, _bf(lvark[0:N2])], axis=1))),
        ("ro", ("arb", "tal", "r2", "lvark"), f_ro),
        ("mn", ("tal", "b2", "k2", "vst"), f_mn),
    ]


def _side_casts(side, nsteps, step_of):
    in_specs, out_specs, out_shapes = [], [], []
    for a, layer in side:
        rows, cols = a.shape[-2:]
        slab = rows // nsteps
        assert rows % nsteps == 0 and slab % (2 * SUBLANES) == 0
        if layer is None:
            in_specs.append(pl.BlockSpec((slab, cols), lambda *g: (step_of(*g), 0)))
        else:
            in_specs.append(pl.BlockSpec((None, slab, cols), lambda *g, layer=layer: (layer, step_of(*g), 0)))
        out_specs.append(pl.BlockSpec((slab, cols), lambda *g: (step_of(*g), 0)))
        out_shapes.append(jax.ShapeDtypeStruct((rows, cols), BF16))
    return in_specs, out_specs, out_shapes, [a for a, _ in side]


def _run_side_casts(in_refs, out_refs):
    for src, dst in zip(in_refs, out_refs):
        dst[...] = _bf(src[...])


def _inproj_kernel(nside, x_ref, w_ref, *refs):
    side_in, (pa_ref, pb_ref), side_out = refs[:nside], refs[nside:nside + 2], refs[nside + 2:]
    xb = _bf(x_ref[...])
    pa_ref[...] = _dot(xb, w_ref[:, 0:A_PROJ])
    pb_ref[...] = _dot(xb, w_ref[:, A_PROJ:])
    _run_side_casts(side_in, side_out)


def _inproj(x2d, w, tm, side=()):
    m = x2d.shape[0]
    s_in, s_out, s_shapes, s_args = _side_casts(side, m // tm, lambda i: i)
    return pl.pallas_call(
        functools.partial(_inproj_kernel, len(side)),
        grid=(m // tm,),
        in_specs=[
            pl.BlockSpec((tm, D_MODEL), lambda i: (i, 0)),
            pl.BlockSpec(w.shape, lambda i: (0, 0), pipeline_mode=pl.Buffered(1)),
        ] + s_in,
        out_specs=[
            pl.BlockSpec((tm, A_PROJ), lambda i: (i, 0)),
            pl.BlockSpec((tm, B_PROJ), lambda i: (i, 0)),
        ] + s_out,
        out_shape=[
            jax.ShapeDtypeStruct((m, A_PROJ), F32),
            jax.ShapeDtypeStruct((m, B_PROJ), F32),
        ] + s_shapes,
        compiler_params=pltpu.CompilerParams(
            dimension_semantics=("arbitrary",), vmem_limit_bytes=VMEM_LIMIT),
        name="inproj_ab",
    )(x2d, w, *s_args)


def _rwkv_kernel(nb, tt, ct, pa_ref, shift_ref, s0_ref, mu_ref, w0_ref, w2p_ref, a0_ref, a2p_ref,
                 g2_ref, kk_ref, ka_ref, rk_ref, lng_ref, lnb_ref,
                 ya_ref, sout_ref,
                 s_scr, prev_scr, at_scr, rt_scr, bt_scr, kt_scr, v_scr, p_scr, o_scr,
                 scmask_scr, eye_scr, head_scr, lcum_scr):
    rows = nb * tt
    nchunk = rows // ROWS
    ns = ROWS // ct
    ti = pl.program_id(1)

    @pl.when((pl.program_id(0) == 0) & (ti == 0))
    def _():
        strict, incl, eye, same_head = _stream_masks(ct, 2 * N2)
        one = lambda m: jnp.where(m, 1.0, 0.0)
        scmask_scr[...] = jnp.concatenate([one(strict), one(incl)], axis=0)
        eye_scr[...] = one(eye)[:, 0:N2]
        head_scr[...] = one(same_head)[:, 0:N2]
        ri = _iota2(lcum_scr.shape, 0)
        ci = _iota2(lcum_scr.shape, 1)
        lcum_scr[...] = jnp.where((_div2(ri, ct) == _div2(ci, ct)) & (ci <= ri), 1.0, 0.0).astype(BF16)

    @pl.when(ti == 0)
    def _():
        _load_state(s_scr, s0_ref, nb)
        prev_scr[...] = shift_ref[...]

    ones_bd = _head_ones()

    p3 = pa_ref[...]
    p2 = p3.reshape(rows, A_PROJ)
    t3 = _iota2((nb, tt, A_PROJ), 1)
    rolled = pltpu.roll(p2, 1, axis=0).reshape(nb, tt, A_PROJ)
    pprev = jnp.where(t3 == 0, prev_scr[...], rolled).reshape(rows, A_PROJ)
    prev_scr[...] = p3[:, tt - 1:tt, :]
    m = p2 + (pprev - p2) * mu_ref[...]
    r = m[:, 0:WIDTH]
    k = m[:, WIDTH:2 * WIDTH]
    v = m[:, 2 * WIDTH:3 * WIDTH]
    wa = m[:, 3 * WIDTH:3 * WIDTH + LANES]
    gd = m[:, 3 * WIDTH + LANES:A_PROJ]
    z = -(w0_ref[...] + _dot(_bf(jnp.tanh(wa)), w2p_ref[...]))
    softplus = jnp.maximum(z, 0.0) + jnp.log(1.0 + jnp.exp(-jnp.abs(z)))
    lw = -jnp.exp(-softplus - 0.5)
    a = jax.nn.sigmoid(a0_ref[...] + _dot(_bf(wa), a2p_ref[...]))
    g = _dot(_bf(jax.nn.sigmoid(gd)), g2_ref[...])
    kk = k * kk_ref[...]
    k = k * (1.0 + (a - 1.0) * ka_ref[...])
    kk = kk * lax.rsqrt(jnp.maximum(_headsum(kk * kk, ones_bd), 1e-24))
    bonus = _headsum(r * k * rk_ref[...], ones_bd) * v

    span = lcum_scr.shape[0]
    cs = jnp.concatenate([_dot_split_lhs(lcum_scr[...], lw[i * span:(i + 1) * span], 2)
                          for i in range(rows // span)], axis=0)
    pinv = jnp.exp(-cs)
    p_scr[...] = jnp.exp(cs)
    at_scr[...] = _bf(-kk * jnp.exp(cs - lw))
    rt_scr[...] = _bf(r * p_scr[...])
    bt_scr[...] = _bf(kk * a * pinv)
    kt_scr[...] = _bf(k * pinv)
    v_scr[...] = _bf(v)

    lane_lo = _iota2((ROWS, LANES), 1) < HEAD_DIM
    sc_mask = lambda: scmask_scr[...]
    eye = lambda: eye_scr[...]
    same_head = lambda: head_scr[...]
    nfac = ct.bit_length() - 1

    envs = []
    for c in range(nchunk):
        for j in range(PAIRS):
            rsl = slice(c * ROWS, (c + 1) * ROWS)
            lsl = slice(LANES * j, LANES * (j + 1))
            aj, rj, bj, kj, vj = (s[rsl, lsl] for s in (at_scr, rt_scr, bt_scr, kt_scr, v_scr))
            envs.append({
                "c": c, "j": j, "rsl": rsl, "lsl": lsl,
                "a2": _bdiag(aj, lane_lo), "r2": _bdiag(rj, lane_lo),
                "b2": _bdiag(bj, lane_lo), "k2": _bdiag(kj, lane_lo),
                "bk": jnp.concatenate([bj, bj, kj, kj], axis=0),
                "vst": jnp.concatenate([vj, vj], axis=0),
            })

    steps = _rwkv_score_steps(sc_mask) + _inverse_steps(nfac, eye)
    if ns == 1:
        _emit_by_level(steps + _rwkv_carried_steps(same_head), envs)
        per_seq = tt // ROWS
        state = [[s_scr[s, j] for j in range(PAIRS)] for s in range(nb)]
        for cpos in range(per_seq):
            for s in range(nb):
                c = s * per_seq + cpos
                cenv = envs[c * PAIRS:(c + 1) * PAIRS]
                sb = [_bf(x) for x in state[s]]
                for j, env in enumerate(cenv):
                    rp, oc = env["ro"]
                    o_scr[env["rsl"], env["lsl"]] = _unstack_heads(_dot_tb(rp, sb[j]) + oc, lane_lo)
                for j, env in enumerate(cenv):
                    mk, nn = env["mn"]
                    pc = p_scr[(c + 1) * ROWS - 1:(c + 1) * ROWS, env["lsl"]]
                    state[s][j] = (state[s][j] + _dot(sb[j], mk) + nn) * pc
        for s in range(nb):
            for j in range(PAIRS):
                s_scr[s, j] = state[s][j]
    else:
        for env in envs:
            env["sb"] = [_bf(s_scr[env["c"] * ns + q, env["j"]]) for q in range(ns)]

        def f_h(a2, r2, sb):
            pieces = [_dot_tb(jnp.concatenate([_seq_rows(a2, q, ct), _seq_rows(r2, q, ct)], axis=0), sb[q])
                      for q in range(ns)]
            return (_from_seq_rows([x[0:2 * ct] for x in pieces], ct),
                    _from_seq_rows([x[2 * ct:] for x in pieces], ct))

        def f_ds(u2, vst, b2, k2):
            vst32 = vst.astype(F32)
            out = []
            for q in range(ns):
                uv = jnp.concatenate([_seq_rows(u2, q, ct), _seq_rows(vst32, q, ct)], axis=0)
                bk = jnp.concatenate([_seq_rows(b2, q, ct), _seq_rows(k2, q, ct)], axis=0)
                out.append(_dot_ta(_bf(_pad_rows(uv, LANES)), _pad_rows(bk, LANES)) * same_head())
            return out

        steps += [
            ("h", ("a2", "r2", "sb"), f_h),
            ("u2", ("tb", "h", "lvark"), lambda tb, h, lvark: _dot(tb, _bf(h[0] + lvark[0:N2]))),
            ("o2", ("arb", "u2", "h", "lvark"),
             lambda arb, u2, h, lvark: h[1] + _dot(arb, _bf(u2)) + lvark[N2:]),
            ("ds", ("u2", "vst", "b2", "k2"), f_ds),
        ]
        _emit_by_level(steps, envs)
        for env in envs:
            o_scr[env["rsl"], env["lsl"]] = _unstack_heads(env["o2"], lane_lo)
            for q in range(ns):
                seq = env["c"] * ns + q
                pc = p_scr[(seq + 1) * ct - 1:(seq + 1) * ct, env["lsl"]]
                s_scr[seq, env["j"]] = (s_scr[seq, env["j"]] + env["ds"][q]) * pc

    o = o_scr[...]
    y = (_head_norm(o, lng_ref[...], lnb_ref[...], A_GN_EPS, ones_bd) + bonus) * g
    ya_ref[...] = _bf(y).reshape(nb, tt, WIDTH)

    @pl.when(ti == pl.num_programs(1) - 1)
    def _():
        _store_state(sout_ref, s_scr, nb)


def _row_spec(shape):
    return pl.BlockSpec(shape, lambda b, t: (0,) * len(shape))


def _check_mixer_blocking(bsz, tlen, nb, tt):
    ct = min(ROWS, tt)
    rows = nb * tt
    assert rows % ROWS == 0 and ROWS % ct == 0 and bsz % nb == 0 and tlen % tt == 0
    assert tt % ROWS == 0 or tt == tlen
    return ct, rows


def _rwkv(pa3, shift0, s0, prm, nb, tt):
    bsz, tlen, _ = pa3.shape
    ct, rows = _check_mixer_blocking(bsz, tlen, nb, tt)
    span = tt if tt % ROWS == 0 else rows
    grid = (bsz // nb, tlen // tt)
    state_spec = pl.BlockSpec((nb, HEADS, HEAD_DIM, HEAD_DIM), lambda b, t: (b, 0, 0, 0))
    vec = lambda n: _row_spec((1, n))
    in_specs = [
        pl.BlockSpec((nb, tt, A_PROJ), lambda b, t: (b, t, 0)),
        pl.BlockSpec((nb, 1, A_PROJ), lambda b, t: (b, 0, 0)),
        state_spec,
        vec(A_PROJ), vec(WIDTH), _row_spec((LANES, WIDTH)), vec(WIDTH), _row_spec((LANES, WIDTH)),
        _row_spec((GATE_LORA, WIDTH)), vec(WIDTH), vec(WIDTH), vec(WIDTH), vec(WIDTH), vec(WIDTH),
    ]
    out_specs = [
        pl.BlockSpec((nb, tt, WIDTH), lambda b, t: (b, t, 0)),
        state_spec,
    ]
    scratch = [
        pltpu.VMEM((nb, PAIRS, LANES, LANES), F32),
        pltpu.VMEM((nb, 1, A_PROJ), F32),
        pltpu.VMEM((rows, WIDTH), BF16),
        pltpu.VMEM((rows, WIDTH), BF16),
        pltpu.VMEM((rows, WIDTH), BF16),
        pltpu.VMEM((rows, WIDTH), BF16),
        pltpu.VMEM((rows, WIDTH), BF16),
        pltpu.VMEM((rows, WIDTH), F32),
        pltpu.VMEM((rows, WIDTH), F32),
        pltpu.VMEM((2 * N2, 2 * N2), F32),
        pltpu.VMEM((N2, N2), F32),
        pltpu.VMEM((N2, N2), F32),
        pltpu.VMEM((span, span), BF16),
    ]
    return pl.pallas_call(
        functools.partial(_rwkv_kernel, nb, tt, ct),
        grid=grid,
        in_specs=in_specs,
        out_specs=out_specs,
        out_shape=[
            jax.ShapeDtypeStruct((bsz, tlen, WIDTH), BF16),
            jax.ShapeDtypeStruct(s0.shape, F32),
        ],
        scratch_shapes=scratch,
        compiler_params=pltpu.CompilerParams(
            dimension_semantics=("arbitrary", "arbitrary"), vmem_limit_bytes=VMEM_LIMIT),
        name="rwkv7_mixer",
    )(pa3, shift0.reshape(bsz, 1, A_PROJ), s0, *prm)


def _ret_kernel(nb, tt, ct, pos0, pb_ref, s0_ref, invf_ref, gng_ref, gnb_ref,
                yb_ref, sout_ref,
                s_scr, q_scr, k_scr, v_scr, o_scr, dec_scr, head_scr, trig_scr):
    rows = nb * tt
    nchunk = rows // ROWS
    ns = ROWS // ct
    ti = pl.program_id(1)

    @pl.when((pl.program_id(0) == 0) & (ti == 0))
    def _():
        _, incl, _, same_head = _stream_masks(ct, N2)
        head_scr[...] = jnp.where(same_head, 1.0, 0.0)
        ri = _iota2((N2, N2), 0)
        ci = _iota2((N2, N2), 1)
        tpos = _mod2(ri, ct).astype(F32)
        tdiff = (_mod2(ri, ct) - _mod2(ci, ct)).astype(F32)
        for j in range(PAIRS):
            head = (2 * j + _div2(ri, ROWS)).astype(F32)
            lg = jnp.log1p(-jnp.exp2(-5.0 - head))
            dec_scr[j, 0] = jnp.where(incl, jnp.exp(lg * jnp.maximum(tdiff, 0.0)), 0.0)
            dec_scr[j, 1] = jnp.exp(lg * (tpos + 1.0))
            dec_scr[j, 2] = jnp.exp(lg * (ct - 1.0 - tpos))
            dec_scr[j, 3] = jnp.exp(lg * ct)
        off = _mod2(_iota2((rows, LANES), 0), tt).astype(F32) * invf_ref[...]
        trig_scr[0] = jnp.cos(off)
        trig_scr[1] = jnp.sin(off)

    @pl.when(ti == 0)
    def _():
        _load_state(s_scr, s0_ref, nb)

    ones_bd = _head_ones()
    pb = pb_ref[...].reshape(rows, B_PROJ)
    q = pb[:, 0:WIDTH]
    k = pb[:, WIDTH:2 * WIDTH]
    gate = pb[:, 3 * WIDTH:]

    base = jnp.zeros((SUBLANES, LANES), F32) + (pos0 + ti * tt).astype(F32)
    ang = base * invf_ref[...]
    cos_a, sin_a = jnp.cos(ang)[0:1], jnp.sin(ang)[0:1]
    cos_b, sin_b = trig_scr[0], trig_scr[1]
    cos = jnp.concatenate([cos_a * cos_b - sin_a * sin_b] * PAIRS, axis=1)
    sin = jnp.concatenate([sin_a * cos_b + cos_a * sin_b] * PAIRS, axis=1)
    first_half = _mod2(_iota2((rows, WIDTH), 1), HEAD_DIM) < (HEAD_DIM // 2)
    sin = jnp.where(first_half, -sin, sin)

    pr = _iota2((LANES, LANES), 0)
    pc = _iota2((LANES, LANES), 1)
    swap = jnp.where(pr == lax.bitwise_xor(pc, HEAD_DIM // 2), 1.0, 0.0).astype(BF16)

    def rope(x):
        xb = _bf(x)
        partner = jnp.concatenate([_dot(xb[:, LANES * j:LANES * (j + 1)], swap) for j in range(PAIRS)], axis=1)
        return x * cos + partner * sin

    q_scr[...] = _bf(rope(q))
    k_scr[...] = rope(k) * (HEAD_DIM ** -0.5)
    v_scr[...] = _bf(pb[:, 2 * WIDTH:3 * WIDTH])

    lane_lo = _iota2((ROWS, LANES), 1) < HEAD_DIM
    same_head = lambda: head_scr[...]

    envs = []
    for c in range(nchunk):
        for j in range(PAIRS):
            rsl = slice(c * ROWS, (c + 1) * ROWS)
            lsl = slice(LANES * j, LANES * (j + 1))
            qj, kj, vj = q_scr[rsl, lsl], k_scr[rsl, lsl], v_scr[rsl, lsl]
            kjb = _bf(kj)
            envs.append({
                "c": c, "j": j, "rsl": rsl, "lsl": lsl,
                "q2": _bdiag(qj, lane_lo),
                "kst": jnp.concatenate([kjb, kjb], axis=0),
                "k2d": _bdiag(kj, lane_lo) * dec_scr[j, 2],
                "vst": jnp.concatenate([vj, vj], axis=0),
                "dmask": lambda j=j: dec_scr[j, 0],
            })

    def f_ds(k2d, vst):
        return [_dot_ta(_bf(_pad_rows(_seq_rows(k2d, s, ct), LANES)), _pad_rows(_seq_rows(vst, s, ct), LANES))
                * same_head() for s in range(ns)]

    steps = [
        ("sc", ("q2", "kst", "dmask"), lambda q2, kst, dmask: _bf(_dot_tb(q2, kst) * dmask())),
        ("inner", ("sc", "vst"), _dot),
        ("ds", ("k2d", "vst"), f_ds),
    ]
    _emit_by_level(steps, envs)

    if ns == 1:
        per_seq = tt // ROWS
        for s in range(nb):
            state = [s_scr[s, j] for j in range(PAIRS)]
            for c in range(s * per_seq, (s + 1) * per_seq):
                for j in range(PAIRS):
                    env = envs[c * PAIRS + j]
                    env["sb"] = [_bf(state[j])]
                    state[j] = state[j] * dec_scr[j, 3] + env["ds"][0]
            for j in range(PAIRS):
                s_scr[s, j] = state[j]
    else:
        for env in envs:
            j = env["j"]
            first = env["c"] * ns
            env["sb"] = [_bf(s_scr[first + s, j]) for s in range(ns)]
            for s in range(ns):
                s_scr[first + s, j] = s_scr[first + s, j] * dec_scr[j, 3] + env["ds"][s]

    def f_cross(q2, sb):
        return _from_seq_rows([_dot(_seq_rows(q2, s, ct), sb[s]) for s in range(ns)], ct)

    _emit_by_level([("cross", ("q2", "sb"), f_cross)], envs)
    for env in envs:
        o2 = env["inner"] + env["cross"] * dec_scr[env["j"], 1]
        o_scr[env["rsl"], env["lsl"]] = _unstack_heads(o2, lane_lo)

    o = o_scr[...]
    y = jax.nn.silu(gate) * _head_norm(o, gng_ref[...], gnb_ref[...], B_GN_EPS, ones_bd)
    yb_ref[...] = _bf(y).reshape(nb, tt, WIDTH)

    @pl.when(ti == pl.num_programs(1) - 1)
    def _():
        _store_state(sout_ref, s_scr, nb)


def _retention(pb3, s0, invf, gn_g, gn_b, nb, tt, pos0):
    bsz, tlen, _ = pb3.shape
    ct, rows = _check_mixer_blocking(bsz, tlen, nb, tt)
    grid = (bsz // nb, tlen // tt)
    state_spec = pl.BlockSpec((nb, HEADS, HEAD_DIM, HEAD_DIM), lambda b, t: (b, 0, 0, 0))
    in_specs = [
        pl.BlockSpec((nb, tt, B_PROJ), lambda b, t: (b, t, 0)),
        state_spec,
        _row_spec((1, LANES)), _row_spec((1, WIDTH)), _row_spec((1, WIDTH)),
    ]
    out_specs = [
        pl.BlockSpec((nb, tt, WIDTH), lambda b, t: (b, t, 0)),
        state_spec,
    ]
    scratch = [
        pltpu.VMEM((nb, PAIRS, LANES, LANES), F32),
        pltpu.VMEM((rows, WIDTH), BF16),
        pltpu.VMEM((rows, WIDTH), F32),
        pltpu.VMEM((rows, WIDTH), BF16),
        pltpu.VMEM((rows, WIDTH), F32),
        pltpu.VMEM((PAIRS, 4, N2, N2), F32),
        pltpu.VMEM((N2, N2), F32),
        pltpu.VMEM((2, rows, LANES), F32),
    ]
    return pl.pallas_call(
        functools.partial(_ret_kernel, nb, tt, ct, pos0),
        grid=grid,
        in_specs=in_specs,
        out_specs=out_specs,
        out_shape=[
            jax.ShapeDtypeStruct((bsz, tlen, WIDTH), BF16),
            jax.ShapeDtypeStruct(s0.shape, F32),
        ],
        scratch_shapes=scratch,
        compiler_params=pltpu.CompilerParams(
            dimension_semantics=("arbitrary", "arbitrary"), vmem_limit_bytes=VMEM_LIMIT),
        name="retention_mixer",
    )(pb3, s0, invf, gn_g, gn_b)


def _conv_kernel(nb, tt, nsplit, nside, x_ref, buf_ref, win_ref, cw_ref, *refs):
    side_in, (z_ref, bout_ref), side_out, halo_scr = (
        refs[:nside], refs[nside:nside + 2], refs[nside + 2:2 * nside + 2], refs[-1])
    _run_side_casts(side_in, side_out)
    rows = nb * tt
    sub = rows // nsplit
    ti = pl.program_id(1)

    @pl.when(ti == 0)
    def _():
        halo_scr[...] = buf_ref[...]

    proj = []
    for i in range(nsplit):
        xi = x_ref[0, i * sub:(i + 1) * sub, :] if nb == 1 else x_ref[...].reshape(rows, D_MODEL)
        xb = _bf(xi)
        bg = _dot(xb, win_ref[:, 0:D_MODEL])
        u = _dot(xb, win_ref[:, D_MODEL:2 * D_MODEL]) * _dot(xb, win_ref[:, 2 * D_MODEL:])
        proj.append((bg, u))

    cw = cw_ref[...]
    st = tt // nsplit if nb == 1 else tt
    t3 = _iota2((nb, st, D_MODEL), 1)
    h0 = halo_scr[:, 0:1, :]
    h1 = halo_scr[:, 1:2, :]
    for i, (bg, u) in enumerate(proj):
        u3 = u.reshape(nb, st, D_MODEL)
        prev1 = jnp.where(t3 == 0, h1, pltpu.roll(u, 1, axis=0).reshape(nb, st, D_MODEL))
        prev2 = jnp.where(t3 == 0, h0,
                          jnp.where(t3 == 1, h1, pltpu.roll(u, 2, axis=0).reshape(nb, st, D_MODEL)))
        conv = prev2 * cw[0:1, :] + prev1 * cw[1:2, :] + u3 * cw[2:3, :]
        z_ref[i * sub:(i + 1) * sub, :] = _bf(bg * conv.reshape(sub, D_MODEL))
        h0 = u3[:, st - 2:st - 1, :]
        h1 = u3[:, st - 1:st, :]
    halo_scr[:, 0:1, :] = h0
    halo_scr[:, 1:2, :] = h1

    @pl.when(ti == pl.num_programs(1) - 1)
    def _():
        bout_ref[...] = halo_scr[...]


def _conv_mixer(x3, buf0, win, cw, nb, tt, nsplit, side=()):
    bsz, tlen, _ = x3.shape
    rows = nb * tt
    assert bsz % nb == 0 and tlen % tt == 0
    assert nsplit == 1 or nb == 1
    assert tt % nsplit == 0 and tt // nsplit >= CONV_W - 1
    nt = tlen // tt
    grid = (bsz // nb, nt)
    buf_spec = pl.BlockSpec((nb, CONV_W - 1, D_MODEL), lambda b, t: (b, 0, 0))
    s_in, s_out, s_shapes, s_args = _side_casts(side, grid[0] * nt, lambda b, t: b * nt + t)
    return pl.pallas_call(
        functools.partial(_conv_kernel, nb, tt, nsplit, len(side)),
        grid=grid,
        in_specs=[
            pl.BlockSpec((nb, tt, D_MODEL), lambda b, t: (b, t, 0)),
            buf_spec,
            pl.BlockSpec(win.shape, lambda b, t: (0, 0), pipeline_mode=pl.Buffered(1)),
            _row_spec((CONV_W, D_MODEL)),
        ] + s_in,
        out_specs=[
            pl.BlockSpec((rows, D_MODEL), lambda b, t: (b * nt + t, 0)),
            buf_spec,
        ] + s_out,
        out_shape=[
            jax.ShapeDtypeStruct((bsz * tlen, D_MODEL), BF16),
            jax.ShapeDtypeStruct(buf0.shape, F32),
        ] + s_shapes,
        scratch_shapes=[pltpu.VMEM((nb, CONV_W - 1, D_MODEL), F32)],
        compiler_params=pltpu.CompilerParams(
            dimension_semantics=("arbitrary", "arbitrary"), vmem_limit_bytes=VMEM_LIMIT),
        name="conv_mixer",
    )(x3, buf0, win, cw, *s_args)


FF_CHUNK = 1024


def _post_kernel(npieces, nsplit, nside, *refs):
    x_ref = refs[0]
    y_refs = refs[1:1 + npieces]
    wout_ref, g1_ref, b1_ref, wup_ref, wdown_ref, g2_ref, b2_ref = refs[1 + npieces:8 + npieces]
    side_in = refs[8 + npieces:8 + npieces + nside]
    out_ref = refs[8 + npieces + nside]
    _run_side_casts(side_in, refs[9 + npieces + nside:])
    sub = x_ref.shape[0] // nsplit

    def f_y(r):
        y = None
        off = 0
        for y_ref in y_refs:
            n = y_ref.shape[1]
            term = _dot(y_ref[r, :], wout_ref[off:off + n, :])
            y = term if y is None else y + term
            off += n
        return y

    def f_mlp(f):
        def step(x1b, acc):
            h = jnp.maximum(_dot(x1b, wup_ref[:, f * FF_CHUNK:(f + 1) * FF_CHUNK]), 0.0)
            term = _dot(_bf(h * h), wdown_ref[f * FF_CHUNK:(f + 1) * FF_CHUNK, :])
            return term if acc is None else acc + term
        return step

    def f_out(r, x1, acc):
        out_ref[r, :] = _layer_norm(ALPHA * x1 + acc, g2_ref[...], b2_ref[...])
        return None

    steps = [
        ("y", ("r",), f_y),
        ("x1", ("r", "y"), lambda r, y: _layer_norm(ALPHA * x_ref[r, :] + y, g1_ref[...], b1_ref[...])),
        ("x1b", ("x1",), _bf),
        ("acc-1", ("x1b",), lambda x1b: None),
    ]
    nff = D_FF // FF_CHUNK
    for f in range(nff):
        steps.append((f"acc{f}", ("x1b", f"acc{f - 1}"), f_mlp(f)))
    steps.append(("out", ("r", "x1", f"acc{nff - 1}"), f_out))
    envs = [{"r": slice(i * sub, (i + 1) * sub)} for i in range(nsplit)]
    _emit_by_level(steps, envs)


def _post(x2d, ys, wout, g1, b1, wup, wdown, g2, b2, tm, nsplit, side=()):
    m = x2d.shape[0]
    assert m % tm == 0 and tm % nsplit == 0
    const = lambda i: (0, 0)
    resident = lambda a: pl.BlockSpec(a.shape, const, pipeline_mode=pl.Buffered(1))
    vec = pl.BlockSpec((1, D_MODEL), const)
    s_in, s_out, s_shapes, s_args = _side_casts(side, m // tm, lambda i: i)
    in_specs = [pl.BlockSpec((tm, D_MODEL), lambda i: (i, 0))]
    in_specs += [pl.BlockSpec((tm, y.shape[1]), lambda i: (i, 0)) for y in ys]
    in_specs += [resident(wout), vec, vec, resident(wup), resident(wdown), vec, vec] + s_in
    return pl.pallas_call(
        functools.partial(_post_kernel, len(ys), nsplit, len(side)),
        grid=(m // tm,),
        in_specs=in_specs,
        out_specs=[pl.BlockSpec((tm, D_MODEL), lambda i: (i, 0))] + s_out,
        out_shape=[jax.ShapeDtypeStruct((m, D_MODEL), F32)] + s_shapes,
        compiler_params=pltpu.CompilerParams(
            dimension_semantics=("arbitrary",), vmem_limit_bytes=VMEM_LIMIT),
        name="post_block",
    )(x2d, *ys, wout, g1, b1, wup, wdown, g2, b2, *s_args)


def _trunk(x, st_shift, st_wkv, st_ret, st_conv, pos0, blk, w, dense):
    bsz, tlen, _ = x.shape
    m = bsz * tlen
    x2d = x.reshape(m, D_MODEL)
    row = lambda a: a.reshape(1, -1)
    cast = dense is None
    dense = {} if cast else dense
    side = lambda *items: items if cast else ()

    pa, pb, *made = _inproj(x2d, w["w_in_ab"], blk["tm_in"],
                            side((w["w_out_ab"], 0), (w["w_up"], 0), (w["w_down"], 0)))
    if cast:
        dense["w_out_ab"], dense["w_up0"], dense["w_down0"] = made
    pa3 = pa.reshape(bsz, tlen, A_PROJ)
    rw_prm = (row(w["mu_a"]), row(w["w0"]), w["w2p"], row(w["a0"]), w["a2p"], w["g2"],
              row(w["k_k"]), row(w["k_a"]), row(w["r_k"]), row(w["lnx_g"]), row(w["lnx_b"]))
    ya, wkv1 = _rwkv(pa3, st_shift, st_wkv, rw_prm, blk["nb_mix"], blk["tt_mix"])
    yb, ret1 = _retention(pb.reshape(bsz, tlen, B_PROJ), st_ret, w["invf"], row(w["gn_g"]),
                          row(w["gn_b"]), blk["nb_mix"], blk["tt_mix"], pos0)
    shift1 = pa3[:, tlen - 1, :]
    x2d, *made = _post(x2d, [ya.reshape(m, WIDTH), yb.reshape(m, WIDTH)], dense["w_out_ab"],
                       row(w["ln1_g"][0]), row(w["ln1_b"][0]), dense["w_up0"], dense["w_down0"],
                       row(w["ln2_g"][0]), row(w["ln2_b"][0]), blk["tm_post"], blk["split"],
                       side((w["w_in_conv"], 0)))
    if cast:
        dense["w_in_conv"], = made

    z, conv1, *made = _conv_mixer(x2d.reshape(bsz, tlen, D_MODEL), st_conv, dense["w_in_conv"], w["conv_w"],
                                  blk["nb_conv"], blk["tt_conv"], blk["split_conv"],
                                  side((w["w_out_conv"], 0), (w["w_up"], 1), (w["w_down"], 1)))
    if cast:
        dense["w_out_conv"], dense["w_up1"], dense["w_down1"] = made
    x2d, = _post(x2d, [z], dense["w_out_conv"], row(w["ln1_g"][1]), row(w["ln1_b"][1]),
                 dense["w_up1"], dense["w_down1"], row(w["ln2_g"][1]), row(w["ln2_b"][1]),
                 blk["tm_post"], blk["split"])
    outs = (x2d.reshape(bsz, tlen, D_MODEL), shift1[None], wkv1[None], ret1[None], conv1[None])
    return outs, dense


def kernel(x_prompt, x_sample, state_shift, state_wkv, state_ret, state_conv, w_in_ab, mu_a, w0, w2, a0, a2,
           g2, k_k, k_a, r_k, lnx_g, lnx_b, gn_g, gn_b, w_out_ab, w_in_conv, conv_w, w_out_conv,
           ln1_g, ln1_b, ln2_g, ln2_b, w_up, w_down):
    bp = x_prompt.shape[0]
    half = HEAD_DIM // 2
    inv = ROPE_BASE ** (-jnp.arange(half, dtype=F32) / half)
    zpad = jnp.zeros((LANES - DECAY_LORA, WIDTH), F32)
    w = {
        "w_in_ab": _bf(w_in_ab[0]),
        "mu_a": mu_a[0], "w0": w0[0], "a0": a0[0],
        "w2p": _bf(jnp.concatenate([w2[0], zpad], axis=0)),
        "a2p": _bf(jnp.concatenate([zpad, a2[0]], axis=0)),
        "g2": _bf(g2[0]), "k_k": k_k[0], "k_a": k_a[0], "r_k": r_k[0],
        "lnx_g": lnx_g[0], "lnx_b": lnx_b[0], "gn_g": gn_g[0], "gn_b": gn_b[0],
        "invf": jnp.tile(inv, LANES // half).reshape(1, LANES),
        "w_out_ab": w_out_ab, "w_in_conv": w_in_conv, "conv_w": conv_w[0], "w_out_conv": w_out_conv,
        "ln1_g": ln1_g, "ln1_b": ln1_b, "ln2_g": ln2_g, "ln2_b": ln2_b,
        "w_up": w_up, "w_down": w_down,
    }
    dt = state_wkv.dtype
    z_shift = jnp.zeros((bp, A_PROJ), dt)
    z_state = jnp.zeros((bp, HEADS, HEAD_DIM, HEAD_DIM), dt)
    z_conv = jnp.zeros((bp, CONV_W - 1, D_MODEL), dt)
    blk_p = {"tm_in": 1024, "tm_post": 1024, "split": 4, "nb_mix": 2, "tt_mix": 256,
             "nb_conv": 1, "tt_conv": 1024, "split_conv": 2}
    (y_p, p_shift, p_wkv, p_ret, p_conv), dense = _trunk(x_prompt, z_shift, z_state, z_state, z_conv, 0,
                                                         blk_p, w, None)
    ts = x_sample.shape[1]
    blk_s = {"tm_in": 512, "tm_post": 1024, "split": 4, "nb_mix": 2 * ROWS // ts, "tt_mix": ts,
             "nb_conv": 256 // ts, "tt_conv": ts, "split_conv": 1}
    (y_s, s_shift, s_wkv, s_ret, s_conv), _ = _trunk(x_sample, state_shift[0], state_wkv[0], state_ret[0],
                                                     state_conv[0], PAST_LEN, blk_s, w, dense)
    return (y_p, y_s, p_shift, p_wkv, p_ret, p_conv, s_shift, s_wkv, s_ret, s_conv)
```

```python
import functools

import jax
import jax.numpy as jnp
from jax import lax
from jax.experimental import pallas as pl
from jax.experimental.pallas import tpu as pltpu

F32 = jnp.float32
BF16 = jnp.bfloat16

D_MODEL = 1024
DEPTH = 2
PAST_LEN = 16384
HEADS = 8
HEAD_DIM = 64
WIDTH = HEADS * HEAD_DIM
DECAY_LORA = 64
AAA_LORA = 64
GATE_LORA = 128
A_PROJ = 3 * WIDTH + DECAY_LORA + AAA_LORA + GATE_LORA
B_PROJ = 4 * WIDTH
A_GN_EPS = 64e-5
B_GN_EPS = 1e-5
ROPE_BASE = 10000.0
CONV_W = 3
D_FF = 4 * D_MODEL
LN_EPS = 1e-5
ALPHA = (2.0 * DEPTH) ** 0.25

LANES = 128
SUBLANES = 8
PAIRS = WIDTH // LANES
ROWS = 64
N2 = 2 * ROWS
VMEM_LIMIT = 56 * 1024 * 1024


def _bf(x):
    return x.astype(BF16)


def _dot(a, b):
    return jnp.dot(a, b, preferred_element_type=F32)


def _dot_tb(a, b):
    return lax.dot_general(a, b, (((1,), (1,)), ((), ())), preferred_element_type=F32)


def _dot_ta(a, b):
    return lax.dot_general(a, b, (((0,), (0,)), ((), ())), preferred_element_type=F32)


def _dot_split_lhs(m, x, parts):
    acc = None
    rem = x
    for i in range(parts):
        hi = _bf(rem)
        term = _dot(m, hi)
        acc = term if acc is None else acc + term
        if i + 1 < parts:
            rem = rem - hi.astype(F32)
    return acc


def _iota2(shape, dim):
    return lax.broadcasted_iota(jnp.int32, shape, dim)


def _div2(x, n):
    assert n & (n - 1) == 0
    return lax.shift_right_logical(x, n.bit_length() - 1)


def _mod2(x, n):
    assert n & (n - 1) == 0
    return lax.bitwise_and(x, n - 1)


def _head_ones():
    ri = _iota2((LANES, LANES), 0)
    ci = _iota2((LANES, LANES), 1)
    return jnp.where(_div2(ri, HEAD_DIM) == _div2(ci, HEAD_DIM), 1.0, 0.0).astype(BF16)


def _headsum(x, ones_bd):
    xb = _bf(x)
    return jnp.concatenate([_dot(xb[:, LANES * j:LANES * (j + 1)], ones_bd) for j in range(PAIRS)], axis=1)


def _head_norm(o, g, b, eps, ones_bd):
    mu = _headsum(o, ones_bd) * (1.0 / HEAD_DIM)
    d = o - mu
    var = _headsum(d * d, ones_bd) * (1.0 / HEAD_DIM)
    return d * lax.rsqrt(var + eps) * g + b


def _layer_norm(z, g, b):
    mu = jnp.mean(z, axis=-1, keepdims=True)
    d = z - mu
    var = jnp.mean(d * d, axis=-1, keepdims=True)
    return d * lax.rsqrt(var + LN_EPS) * g + b


def _bdiag(x, lane_lo):
    zero = jnp.zeros_like(x)
    return jnp.concatenate([jnp.where(lane_lo, x, zero), jnp.where(lane_lo, zero, x)], axis=0)


def _unstack_heads(x, lane_lo):
    return jnp.where(lane_lo, x[0:ROWS], x[ROWS:])


def _stream_masks(ct, width):
    ri = _iota2((N2, width), 0)
    ci = _mod2(_iota2((N2, width), 1), N2)
    same = _div2(ri, ct) == _div2(ci, ct)
    return same & (ci < ri), same & (ci <= ri), ri == ci, _div2(ri, ROWS) == _div2(ci, ROWS)


def _load_state(s_scr, s0_ref, nb):
    z = jnp.zeros((HEAD_DIM, HEAD_DIM), F32)
    for s in range(nb):
        for j in range(PAIRS):
            top = jnp.concatenate([s0_ref[s, 2 * j], z], axis=1)
            bot = jnp.concatenate([z, s0_ref[s, 2 * j + 1]], axis=1)
            s_scr[s, j] = jnp.concatenate([top, bot], axis=0)


def _store_state(sout_ref, s_scr, nb):
    for s in range(nb):
        for j in range(PAIRS):
            s2 = s_scr[s, j]
            sout_ref[s, 2 * j] = s2[0:HEAD_DIM, 0:HEAD_DIM]
            sout_ref[s, 2 * j + 1] = s2[HEAD_DIM:, HEAD_DIM:]


def _seq_rows(x, q, ct):
    if ct == ROWS:
        return x
    return jnp.concatenate([x[q * ct:(q + 1) * ct], x[ROWS + q * ct:ROWS + (q + 1) * ct]], axis=0)


def _from_seq_rows(pieces, ct):
    if ct == ROWS:
        return pieces[0]
    return jnp.concatenate([p[0:ct] for p in pieces] + [p[ct:2 * ct] for p in pieces], axis=0)


def _pad_rows(x, rows):
    if x.shape[0] == rows:
        return x
    return jnp.concatenate([x, jnp.zeros((rows - x.shape[0], x.shape[1]), x.dtype)], axis=0)


def _emit_by_level(steps, envs):
    level = {}
    for name, deps, _ in steps:
        level[name] = 1 + max([level.get(d, 0) for d in deps], default=0)
    for lv in sorted(set(level.values())):
        for name, deps, fn in steps:
            if level[name] == lv:
                for env in envs:
                    env[name] = fn(*[env[d] for d in deps])


def _inverse_steps(nfac, eye):
    def advance(last):
        def fn(q, p):
            pb = _bf(p)
            if last:
                return p + _dot(q, pb), None
            x = _dot(q, jnp.concatenate([pb, q], axis=1))
            return p + x[:, 0:N2], _bf(x[:, N2:])
        return fn

    steps = [
        ("q0", ("l",), _bf),
        ("p1", ("l",), lambda l: l + eye()),
        ("q1", ("q0",), lambda q: _bf(_dot(q, q))),
    ]
    for i in range(1, nfac):
        steps.append((f"s{i}", (f"q{i}", f"p{i}"), advance(i == nfac - 1)))
        steps.append((f"p{i + 1}", (f"s{i}",), lambda s: s[0]))
        steps.append((f"q{i + 1}", (f"s{i}",), lambda s: s[1]))
    steps.append(("tb", (f"p{nfac}",), _bf))
    return steps


def _rwkv_score_steps(sc_mask):
    return [
        ("scm", ("a2", "r2", "bk"),
         lambda a2, r2, bk: _dot_tb(jnp.concatenate([a2, r2], axis=0), bk) * sc_mask()),
        ("l", ("scm",), lambda scm: scm[0:N2, 0:N2]),
        ("lkb", ("scm",), lambda scm: _bf(scm[:, N2:])),
        ("arb", ("scm",), lambda scm: _bf(scm[N2:, 0:N2])),
        ("lvark", ("lkb", "vst"), _dot),
    ]


def _rwkv_carried_steps(same_head):
    def f_ro(arb, tal, r2, lvark):
        x = _dot(arb, _bf(tal))
        return _bf(r2.astype(F32) + x[:, 0:N2]), x[:, N2:] + lvark[N2:]

    def f_mn(tal, b2, k2, vst):
        mm = _dot_ta(_bf(tal), b2)
        n2 = _dot_ta(vst, k2)
        return _bf(mm[0:N2]), (mm[N2:] + n2) * same_head()

    return [
        ("tal", ("tb", "a2", "lvark"),
         lambda tb, a2, lvark: _dot(tb, jnp.concatenate([a2, _bf(lvark[0:N2])], axis=1))),
        ("ro", ("arb", "tal", "r2", "lvark"), f_ro),
        ("mn", ("tal", "b2", "k2", "vst"), f_mn),
    ]


def _side_casts(side, nsteps, step_of):
    in_specs, out_specs, out_shapes = [], [], []
    for a, layer in side:
        rows, cols = a.shape[-2:]
        slab = rows // nsteps
        assert rows % nsteps == 0 and slab % (2 * SUBLANES) == 0
        if layer is None:
            in_specs.append(pl.BlockSpec((slab, cols), lambda *g: (step_of(*g), 0)))
        else:
            in_specs.append(pl.BlockSpec((None, slab, cols), lambda *g, layer=layer: (layer, step_of(*g), 0)))
        out_specs.append(pl.BlockSpec((slab, cols), lambda *g: (step_of(*g), 0)))
        out_shapes.append(jax.ShapeDtypeStruct((rows, cols), BF16))
    return in_specs, out_specs, out_shapes, [a for a, _ in side]


def _with_side_casts(body, n_in, n_out, nside):
    def kernel(*refs):
        ins, rest = refs[:n_in], refs[n_in:]
        side_in, rest = rest[:nside], rest[nside:]
        outs, rest = rest[:n_out], rest[n_out:]
        side_out, scratch = rest[:nside], rest[nside:]
        for src, dst in zip(side_in, side_out):
            dst[...] = _bf(src[...])
        body(*ins, *outs, *scratch)
    return kernel


def _inproj_kernel(x_ref, w_ref, pa_ref, pb_ref):
    xb = _bf(x_ref[...])
    pa_ref[...] = _dot(xb, w_ref[:, 0:A_PROJ])
    pb_ref[...] = _dot(xb, w_ref[:, A_PROJ:])


def _inproj(x2d, w, tm):
    m = x2d.shape[0]
    return pl.pallas_call(
        _inproj_kernel,
        grid=(m // tm,),
        in_specs=[
            pl.BlockSpec((tm, D_MODEL), lambda i: (i, 0)),
            pl.BlockSpec(w.shape, lambda i: (0, 0), pipeline_mode=pl.Buffered(1)),
        ],
        out_specs=[
            pl.BlockSpec((tm, A_PROJ), lambda i: (i, 0)),
            pl.BlockSpec((tm, B_PROJ), lambda i: (i, 0)),
        ],
        out_shape=[
            jax.ShapeDtypeStruct((m, A_PROJ), F32),
            jax.ShapeDtypeStruct((m, B_PROJ), F32),
        ],
        compiler_params=pltpu.CompilerParams(
            dimension_semantics=("arbitrary",), vmem_limit_bytes=VMEM_LIMIT),
        name="inproj_ab",
    )(x2d, w)


def _rwkv_kernel(nb, tt, ct, pa_ref, shift_ref, s0_ref, mu_ref, w0_ref, w2p_ref, a0_ref, a2p_ref,
                 g2_ref, kk_ref, ka_ref, rk_ref, lng_ref, lnb_ref,
                 ya_ref, sout_ref,
                 s_scr, prev_scr, at_scr, rt_scr, bt_scr, kt_scr, v_scr, p_scr, o_scr,
                 scmask_scr, eye_scr, head_scr, lcum_scr):
    rows = nb * tt
    nchunk = rows // ROWS
    ns = ROWS // ct
    ti = pl.program_id(1)

    @pl.when((pl.program_id(0) == 0) & (ti == 0))
    def _():
        strict, incl, eye, same_head = _stream_masks(ct, 2 * N2)
        one = lambda m: jnp.where(m, 1.0, 0.0)
        scmask_scr[...] = jnp.concatenate([one(strict), one(incl)], axis=0)
        eye_scr[...] = one(eye)[:, 0:N2]
        head_scr[...] = one(same_head)[:, 0:N2]
        ri = _iota2(lcum_scr.shape, 0)
        ci = _iota2(lcum_scr.shape, 1)
        lcum_scr[...] = jnp.where((_div2(ri, ct) == _div2(ci, ct)) & (ci <= ri), 1.0, 0.0).astype(BF16)

    @pl.when(ti == 0)
    def _():
        _load_state(s_scr, s0_ref, nb)
        prev_scr[...] = shift_ref[...]

    ones_bd = _head_ones()

    p3 = pa_ref[...]
    p2 = p3.reshape(rows, A_PROJ)
    t3 = _iota2((nb, tt, A_PROJ), 1)
    rolled = pltpu.roll(p2, 1, axis=0).reshape(nb, tt, A_PROJ)
    pprev = jnp.where(t3 == 0, prev_scr[...], rolled).reshape(rows, A_PROJ)
    prev_scr[...] = p3[:, tt - 1:tt, :]
    m = p2 + (pprev - p2) * mu_ref[...]
    r = m[:, 0:WIDTH]
    k = m[:, WIDTH:2 * WIDTH]
    v = m[:, 2 * WIDTH:3 * WIDTH]
    wa = m[:, 3 * WIDTH:3 * WIDTH + LANES]
    gd = m[:, 3 * WIDTH + LANES:A_PROJ]
    z = -(w0_ref[...] + _dot(_bf(jnp.tanh(wa)), w2p_ref[...]))
    softplus = jnp.maximum(z, 0.0) + jnp.log(1.0 + jnp.exp(-jnp.abs(z)))
    lw = -jnp.exp(-softplus - 0.5)
    a = jax.nn.sigmoid(a0_ref[...] + _dot(_bf(wa), a2p_ref[...]))
    g = _dot(_bf(jax.nn.sigmoid(gd)), g2_ref[...])
    kk = k * kk_ref[...]
    k = k * (1.0 + (a - 1.0) * ka_ref[...])
    kk = kk * lax.rsqrt(jnp.maximum(_headsum(kk * kk, ones_bd), 1e-24))
    bonus = _headsum(r * k * rk_ref[...], ones_bd) * v

    span = lcum_scr.shape[0]
    cs = jnp.concatenate([_dot_split_lhs(lcum_scr[...], lw[i * span:(i + 1) * span], 2)
                          for i in range(rows // span)], axis=0)
    pinv = jnp.exp(-cs)
    p_scr[...] = jnp.exp(cs)
    at_scr[...] = _bf(-kk * jnp.exp(cs - lw))
    rt_scr[...] = _bf(r * p_scr[...])
    bt_scr[...] = _bf(kk * a * pinv)
    kt_scr[...] = _bf(k * pinv)
    v_scr[...] = _bf(v)

    lane_lo = _iota2((ROWS, LANES), 1) < HEAD_DIM
    sc_mask = lambda: scmask_scr[...]
    eye = lambda: eye_scr[...]
    same_head = lambda: head_scr[...]
    nfac = ct.bit_length() - 1

    envs = []
    for c in range(nchunk):
        for j in range(PAIRS):
            rsl = slice(c * ROWS, (c + 1) * ROWS)
            lsl = slice(LANES * j, LANES * (j + 1))
            aj, rj, bj, kj, vj = (s[rsl, lsl] for s in (at_scr, rt_scr, bt_scr, kt_scr, v_scr))
            envs.append({
                "c": c, "j": j, "rsl": rsl, "lsl": lsl,
                "a2": _bdiag(aj, lane_lo), "r2": _bdiag(rj, lane_lo),
                "b2": _bdiag(bj, lane_lo), "k2": _bdiag(kj, lane_lo),
                "bk": jnp.concatenate([bj, bj, kj, kj], axis=0),
                "vst": jnp.concatenate([vj, vj], axis=0),
            })

    steps = _rwkv_score_steps(sc_mask) + _inverse_steps(nfac, eye)
    if ns == 1:
        _emit_by_level(steps + _rwkv_carried_steps(same_head), envs)
        per_seq = tt // ROWS
        state = [[s_scr[s, j] for j in range(PAIRS)] for s in range(nb)]
        for cpos in range(per_seq):
            for s in range(nb):
                c = s * per_seq + cpos
                cenv = envs[c * PAIRS:(c + 1) * PAIRS]
                sb = [_bf(x) for x in state[s]]
                for j, env in enumerate(cenv):
                    rp, oc = env["ro"]
                    o_scr[env["rsl"], env["lsl"]] = _unstack_heads(_dot_tb(rp, sb[j]) + oc, lane_lo)
                for j, env in enumerate(cenv):
                    mk, nn = env["mn"]
                    pc = p_scr[(c + 1) * ROWS - 1:(c + 1) * ROWS, env["lsl"]]
                    state[s][j] = (state[s][j] + _dot(sb[j], mk) + nn) * pc
        for s in range(nb):
            for j in range(PAIRS):
                s_scr[s, j] = state[s][j]
    else:
        for env in envs:
            env["sb"] = [_bf(s_scr[env["c"] * ns + q, env["j"]]) for q in range(ns)]

        def f_h(a2, r2, sb):
            pieces = [_dot_tb(jnp.concatenate([_seq_rows(a2, q, ct), _seq_rows(r2, q, ct)], axis=0), sb[q])
                      for q in range(ns)]
            return (_from_seq_rows([x[0:2 * ct] for x in pieces], ct),
                    _from_seq_rows([x[2 * ct:] for x in pieces], ct))

        def f_ds(u2, vst, b2, k2):
            vst32 = vst.astype(F32)
            out = []
            for q in range(ns):
                uv = jnp.concatenate([_seq_rows(u2, q, ct), _seq_rows(vst32, q, ct)], axis=0)
                bk = jnp.concatenate([_seq_rows(b2, q, ct), _seq_rows(k2, q, ct)], axis=0)
                out.append(_dot_ta(_bf(_pad_rows(uv, LANES)), _pad_rows(bk, LANES)) * same_head())
            return out

        steps += [
            ("h", ("a2", "r2", "sb"), f_h),
            ("u2", ("tb", "h", "lvark"), lambda tb, h, lvark: _dot(tb, _bf(h[0] + lvark[0:N2]))),
            ("o2", ("arb", "u2", "h", "lvark"),
             lambda arb, u2, h, lvark: h[1] + _dot(arb, _bf(u2)) + lvark[N2:]),
            ("ds", ("u2", "vst", "b2", "k2"), f_ds),
        ]
        _emit_by_level(steps, envs)
        for env in envs:
            o_scr[env["rsl"], env["lsl"]] = _unstack_heads(env["o2"], lane_lo)
            for q in range(ns):
                seq = env["c"] * ns + q
                pc = p_scr[(seq + 1) * ct - 1:(seq + 1) * ct, env["lsl"]]
                s_scr[seq, env["j"]] = (s_scr[seq, env["j"]] + env["ds"][q]) * pc

    o = o_scr[...]
    y = (_head_norm(o, lng_ref[...], lnb_ref[...], A_GN_EPS, ones_bd) + bonus) * g
    ya_ref[...] = _bf(y).reshape(nb, tt, WIDTH)

    @pl.when(ti == pl.num_programs(1) - 1)
    def _():
        _store_state(sout_ref, s_scr, nb)


def _row_spec(shape):
    return pl.BlockSpec(shape, lambda b, t: (0,) * len(shape))


def _check_mixer_blocking(bsz, tlen, nb, tt):
    ct = min(ROWS, tt)
    rows = nb * tt
    assert rows % ROWS == 0 and ROWS % ct == 0 and bsz % nb == 0 and tlen % tt == 0
    assert tt % ROWS == 0 or tt == tlen
    return ct, rows


def _rwkv(pa3, shift0, s0, prm, nb, tt, side=()):
    bsz, tlen, _ = pa3.shape
    ct, rows = _check_mixer_blocking(bsz, tlen, nb, tt)
    span = tt if tt % ROWS == 0 else rows
    grid = (bsz // nb, tlen // tt)
    s_in, s_out, s_shapes, s_args = _side_casts(side, grid[0] * grid[1], lambda b, t: b * grid[1] + t)
    state_spec = pl.BlockSpec((nb, HEADS, HEAD_DIM, HEAD_DIM), lambda b, t: (b, 0, 0, 0))
    vec = lambda n: _row_spec((1, n))
    in_specs = [
        pl.BlockSpec((nb, tt, A_PROJ), lambda b, t: (b, t, 0)),
        pl.BlockSpec((nb, 1, A_PROJ), lambda b, t: (b, 0, 0)),
        state_spec,
        vec(A_PROJ), vec(WIDTH), _row_spec((LANES, WIDTH)), vec(WIDTH), _row_spec((LANES, WIDTH)),
        _row_spec((GATE_LORA, WIDTH)), vec(WIDTH), vec(WIDTH), vec(WIDTH), vec(WIDTH), vec(WIDTH),
    ]
    out_specs = [
        pl.BlockSpec((nb, tt, WIDTH), lambda b, t: (b, t, 0)),
        state_spec,
    ]
    scratch = [
        pltpu.VMEM((nb, PAIRS, LANES, LANES), F32),
        pltpu.VMEM((nb, 1, A_PROJ), F32),
        pltpu.VMEM((rows, WIDTH), BF16),
        pltpu.VMEM((rows, WIDTH), BF16),
        pltpu.VMEM((rows, WIDTH), BF16),
        pltpu.VMEM((rows, WIDTH), BF16),
        pltpu.VMEM((rows, WIDTH), BF16),
        pltpu.VMEM((rows, WIDTH), F32),
        pltpu.VMEM((rows, WIDTH), F32),
        pltpu.VMEM((2 * N2, 2 * N2), F32),
        pltpu.VMEM((N2, N2), F32),
        pltpu.VMEM((N2, N2), F32),
        pltpu.VMEM((span, span), BF16),
    ]
    return pl.pallas_call(
        _with_side_casts(functools.partial(_rwkv_kernel, nb, tt, ct), len(in_specs), len(out_specs), len(side)),
        grid=grid,
        in_specs=in_specs + s_in,
        out_specs=out_specs + s_out,
        out_shape=[
            jax.ShapeDtypeStruct((bsz, tlen, WIDTH), BF16),
            jax.ShapeDtypeStruct(s0.shape, F32),
        ] + s_shapes,
        scratch_shapes=scratch,
        compiler_params=pltpu.CompilerParams(
            dimension_semantics=("arbitrary", "arbitrary"), vmem_limit_bytes=VMEM_LIMIT),
        name="rwkv7_mixer",
    )(pa3, shift0.reshape(bsz, 1, A_PROJ), s0, *prm, *s_args)


def _ret_kernel(nb, tt, ct, pos0, pb_ref, s0_ref, invf_ref, gng_ref, gnb_ref,
                yb_ref, sout_ref,
                s_scr, q_scr, k_scr, v_scr, o_scr, dec_scr, head_scr, trig_scr):
    rows = nb * tt
    nchunk = rows // ROWS
    ns = ROWS // ct
    ti = pl.program_id(1)

    @pl.when((pl.program_id(0) == 0) & (ti == 0))
    def _():
        _, incl, _, same_head = _stream_masks(ct, N2)
        head_scr[...] = jnp.where(same_head, 1.0, 0.0)
        ri = _iota2((N2, N2), 0)
        ci = _iota2((N2, N2), 1)
        tpos = _mod2(ri, ct).astype(F32)
        tdiff = (_mod2(ri, ct) - _mod2(ci, ct)).astype(F32)
        for j in range(PAIRS):
            head = (2 * j + _div2(ri, ROWS)).astype(F32)
            lg = jnp.log1p(-jnp.exp2(-5.0 - head))
            dec_scr[j, 0] = jnp.where(incl, jnp.exp(lg * jnp.maximum(tdiff, 0.0)), 0.0)
            dec_scr[j, 1] = jnp.exp(lg * (tpos + 1.0))
            dec_scr[j, 2] = jnp.exp(lg * (ct - 1.0 - tpos))
            dec_scr[j, 3] = jnp.exp(lg * ct)
        off = _mod2(_iota2((rows, LANES), 0), tt).astype(F32) * invf_ref[...]
        trig_scr[0] = jnp.cos(off)
        trig_scr[1] = jnp.sin(off)

    @pl.when(ti == 0)
    def _():
        _load_state(s_scr, s0_ref, nb)

    ones_bd = _head_ones()
    pb = pb_ref[...].reshape(rows, B_PROJ)
    q = pb[:, 0:WIDTH]
    k = pb[:, WIDTH:2 * WIDTH]
    gate = pb[:, 3 * WIDTH:]

    base = jnp.zeros((SUBLANES, LANES), F32) + (pos0 + ti * tt).astype(F32)
    ang = base * invf_ref[...]
    cos_a, sin_a = jnp.cos(ang)[0:1], jnp.sin(ang)[0:1]
    cos_b, sin_b = trig_scr[0], trig_scr[1]
    cos = jnp.concatenate([cos_a * cos_b - sin_a * sin_b] * PAIRS, axis=1)
    sin = jnp.concatenate([sin_a * cos_b + cos_a * sin_b] * PAIRS, axis=1)
    first_half = _mod2(_iota2((rows, WIDTH), 1), HEAD_DIM) < (HEAD_DIM // 2)
    sin = jnp.where(first_half, -sin, sin)

    pr = _iota2((LANES, LANES), 0)
    pc = _iota2((LANES, LANES), 1)
    swap = jnp.where(pr == lax.bitwise_xor(pc, HEAD_DIM // 2), 1.0, 0.0).astype(BF16)

    def rope(x):
        xb = _bf(x)
        partner = jnp.concatenate([_dot(xb[:, LANES * j:LANES * (j + 1)], swap) for j in range(PAIRS)], axis=1)
        return x * cos + partner * sin

    q_scr[...] = _bf(rope(q))
    k_scr[...] = rope(k) * (HEAD_DIM ** -0.5)
    v_scr[...] = _bf(pb[:, 2 * WIDTH:3 * WIDTH])

    lane_lo = _iota2((ROWS, LANES), 1) < HEAD_DIM
    same_head = lambda: head_scr[...]

    envs = []
    for c in range(nchunk):
        for j in range(PAIRS):
            rsl = slice(c * ROWS, (c + 1) * ROWS)
            lsl = slice(LANES * j, LANES * (j + 1))
            qj, kj, vj = q_scr[rsl, lsl], k_scr[rsl, lsl], v_scr[rsl, lsl]
            kjb = _bf(kj)
            envs.append({
                "c": c, "j": j, "rsl": rsl, "lsl": lsl,
                "q2": _bdiag(qj, lane_lo),
                "kst": jnp.concatenate([kjb, kjb], axis=0),
                "k2d": _bdiag(kj, lane_lo) * dec_scr[j, 2],
                "vst": jnp.concatenate([vj, vj], axis=0),
                "dmask": lambda j=j: dec_scr[j, 0],
            })

    def f_ds(k2d, vst):
        return [_dot_ta(_bf(_pad_rows(_seq_rows(k2d, s, ct), LANES)), _pad_rows(_seq_rows(vst, s, ct), LANES))
                * same_head() for s in range(ns)]

    steps = [
        ("sc", ("q2", "kst", "dmask"), lambda q2, kst, dmask: _bf(_dot_tb(q2, kst) * dmask())),
        ("inner", ("sc", "vst"), _dot),
        ("ds", ("k2d", "vst"), f_ds),
    ]
    _emit_by_level(steps, envs)

    if ns == 1:
        per_seq = tt // ROWS
        for s in range(nb):
            state = [s_scr[s, j] for j in range(PAIRS)]
            for c in range(s * per_seq, (s + 1) * per_seq):
                for j in range(PAIRS):
                    env = envs[c * PAIRS + j]
                    env["sb"] = [_bf(state[j])]
                    state[j] = state[j] * dec_scr[j, 3] + env["ds"][0]
            for j in range(PAIRS):
                s_scr[s, j] = state[j]
    else:
        for env in envs:
            j = env["j"]
            first = env["c"] * ns
            env["sb"] = [_bf(s_scr[first + s, j]) for s in range(ns)]
            for s in range(ns):
                s_scr[first + s, j] = s_scr[first + s, j] * dec_scr[j, 3] + env["ds"][s]

    def f_cross(q2, sb):
        return _from_seq_rows([_dot(_seq_rows(q2, s, ct), sb[s]) for s in range(ns)], ct)

    _emit_by_level([("cross", ("q2", "sb"), f_cross)], envs)
    for env in envs:
        o2 = env["inner"] + env["cross"] * dec_scr[env["j"], 1]
        o_scr[env["rsl"], env["lsl"]] = _unstack_heads(o2, lane_lo)

    o = o_scr[...]
    y = jax.nn.silu(gate) * _head_norm(o, gng_ref[...], gnb_ref[...], B_GN_EPS, ones_bd)
    yb_ref[...] = _bf(y).reshape(nb, tt, WIDTH)

    @pl.when(ti == pl.num_programs(1) - 1)
    def _():
        _store_state(sout_ref, s_scr, nb)


def _retention(pb3, s0, invf, gn_g, gn_b, nb, tt, pos0, side=()):
    bsz, tlen, _ = pb3.shape
    ct, rows = _check_mixer_blocking(bsz, tlen, nb, tt)
    grid = (bsz // nb, tlen // tt)
    s_in, s_out, s_shapes, s_args = _side_casts(side, grid[0] * grid[1], lambda b, t: b * grid[1] + t)
    state_spec = pl.BlockSpec((nb, HEADS, HEAD_DIM, HEAD_DIM), lambda b, t: (b, 0, 0, 0))
    in_specs = [
        pl.BlockSpec((nb, tt, B_PROJ), lambda b, t: (b, t, 0)),
        state_spec,
        _row_spec((1, LANES)), _row_spec((1, WIDTH)), _row_spec((1, WIDTH)),
    ]
    out_specs = [
        pl.BlockSpec((nb, tt, WIDTH), lambda b, t: (b, t, 0)),
        state_spec,
    ]
    scratch = [
        pltpu.VMEM((nb, PAIRS, LANES, LANES), F32),
        pltpu.VMEM((rows, WIDTH), BF16),
        pltpu.VMEM((rows, WIDTH), F32),
        pltpu.VMEM((rows, WIDTH), BF16),
        pltpu.VMEM((rows, WIDTH), F32),
        pltpu.VMEM((PAIRS, 4, N2, N2), F32),
        pltpu.VMEM((N2, N2), F32),
        pltpu.VMEM((2, rows, LANES), F32),
    ]
    return pl.pallas_call(
        _with_side_casts(functools.partial(_ret_kernel, nb, tt, ct, pos0), len(in_specs), len(out_specs),
                         len(side)),
        grid=grid,
        in_specs=in_specs + s_in,
        out_specs=out_specs + s_out,
        out_shape=[
            jax.ShapeDtypeStruct((bsz, tlen, WIDTH), BF16),
            jax.ShapeDtypeStruct(s0.shape, F32),
        ] + s_shapes,
        scratch_shapes=scratch,
        compiler_params=pltpu.CompilerParams(
            dimension_semantics=("arbitrary", "arbitrary"), vmem_limit_bytes=VMEM_LIMIT),
        name="retention_mixer",
    )(pb3, s0, invf, gn_g, gn_b, *s_args)


def _conv_kernel(nb, tt, nsplit, x_ref, buf_ref, win_ref, cw_ref, z_ref, bout_ref, halo_scr):
    rows = nb * tt
    sub = rows // nsplit
    ti = pl.program_id(1)

    @pl.when(ti == 0)
    def _():
        halo_scr[...] = buf_ref[...]

    proj = []
    for i in range(nsplit):
        xi = x_ref[0, i * sub:(i + 1) * sub, :] if nb == 1 else x_ref[...].reshape(rows, D_MODEL)
        xb = _bf(xi)
        bg = _dot(xb, win_ref[:, 0:D_MODEL])
        u = _dot(xb, win_ref[:, D_MODEL:2 * D_MODEL]) * _dot(xb, win_ref[:, 2 * D_MODEL:])
        proj.append((bg, u))

    cw = cw_ref[...]
    st = tt // nsplit if nb == 1 else tt
    t3 = _iota2((nb, st, D_MODEL), 1)
    h0 = halo_scr[:, 0:1, :]
    h1 = halo_scr[:, 1:2, :]
    for i, (bg, u) in enumerate(proj):
        u3 = u.reshape(nb, st, D_MODEL)
        prev1 = jnp.where(t3 == 0, h1, pltpu.roll(u, 1, axis=0).reshape(nb, st, D_MODEL))
        prev2 = jnp.where(t3 == 0, h0,
                          jnp.where(t3 == 1, h1, pltpu.roll(u, 2, axis=0).reshape(nb, st, D_MODEL)))
        conv = prev2 * cw[0:1, :] + prev1 * cw[1:2, :] + u3 * cw[2:3, :]
        z_ref[i * sub:(i + 1) * sub, :] = _bf(bg * conv.reshape(sub, D_MODEL))
        h0 = u3[:, st - 2:st - 1, :]
        h1 = u3[:, st - 1:st, :]
    halo_scr[:, 0:1, :] = h0
    halo_scr[:, 1:2, :] = h1

    @pl.when(ti == pl.num_programs(1) - 1)
    def _():
        bout_ref[...] = halo_scr[...]


def _conv_mixer(x3, buf0, win, cw, nb, tt, nsplit):
    bsz, tlen, _ = x3.shape
    rows = nb * tt
    assert bsz % nb == 0 and tlen % tt == 0
    assert nsplit == 1 or nb == 1
    assert tt % nsplit == 0 and tt // nsplit >= CONV_W - 1
    grid = (bsz // nb, tlen // tt)
    buf_spec = pl.BlockSpec((nb, CONV_W - 1, D_MODEL), lambda b, t: (b, 0, 0))
    return pl.pallas_call(
        functools.partial(_conv_kernel, nb, tt, nsplit),
        grid=grid,
        in_specs=[
            pl.BlockSpec((nb, tt, D_MODEL), lambda b, t: (b, t, 0)),
            buf_spec,
            pl.BlockSpec(win.shape, lambda b, t: (0, 0), pipeline_mode=pl.Buffered(1)),
            _row_spec((CONV_W, D_MODEL)),
        ],
        out_specs=[
            pl.BlockSpec((rows, D_MODEL), lambda b, t: (b * (tlen // tt) + t, 0)),
            buf_spec,
        ],
        out_shape=[
            jax.ShapeDtypeStruct((bsz * tlen, D_MODEL), BF16),
            jax.ShapeDtypeStruct(buf0.shape, F32),
        ],
        scratch_shapes=[pltpu.VMEM((nb, CONV_W - 1, D_MODEL), F32)],
        compiler_params=pltpu.CompilerParams(
            dimension_semantics=("arbitrary", "arbitrary"), vmem_limit_bytes=VMEM_LIMIT),
        name="conv_mixer",
    )(x3, buf0, win, cw)


FF_CHUNK = 1024


def _post_kernel(npieces, nsplit, *refs):
    x_ref = refs[0]
    y_refs = refs[1:1 + npieces]
    wout_ref, g1_ref, b1_ref, wup_ref, wdown_ref, g2_ref, b2_ref, out_ref = refs[1 + npieces:]
    sub = x_ref.shape[0] // nsplit

    def f_y(r):
        y = None
        off = 0
        for y_ref in y_refs:
            n = y_ref.shape[1]
            term = _dot(y_ref[r, :], wout_ref[off:off + n, :])
            y = term if y is None else y + term
            off += n
        return y

    def f_mlp(f):
        def step(x1b, acc):
            h = jnp.maximum(_dot(x1b, wup_ref[:, f * FF_CHUNK:(f + 1) * FF_CHUNK]), 0.0)
            term = _dot(_bf(h * h), wdown_ref[f * FF_CHUNK:(f + 1) * FF_CHUNK, :])
            return term if acc is None else acc + term
        return step

    def f_out(r, x1, acc):
        out_ref[r, :] = _layer_norm(ALPHA * x1 + acc, g2_ref[...], b2_ref[...])
        return None

    steps = [
        ("y", ("r",), f_y),
        ("x1", ("r", "y"), lambda r, y: _layer_norm(ALPHA * x_ref[r, :] + y, g1_ref[...], b1_ref[...])),
        ("x1b", ("x1",), _bf),
        ("acc-1", ("x1b",), lambda x1b: None),
    ]
    nff = D_FF // FF_CHUNK
    for f in range(nff):
        steps.append((f"acc{f}", ("x1b", f"acc{f - 1}"), f_mlp(f)))
    steps.append(("out", ("r", "x1", f"acc{nff - 1}"), f_out))
    envs = [{"r": slice(i * sub, (i + 1) * sub)} for i in range(nsplit)]
    _emit_by_level(steps, envs)


def _post(x2d, ys, wout, g1, b1, wup, wdown, g2, b2, tm, nsplit):
    m = x2d.shape[0]
    assert m % tm == 0 and tm % nsplit == 0
    const = lambda i: (0, 0)
    resident = lambda a: pl.BlockSpec(a.shape, const, pipeline_mode=pl.Buffered(1))
    vec = pl.BlockSpec((1, D_MODEL), const)
    in_specs = [pl.BlockSpec((tm, D_MODEL), lambda i: (i, 0))]
    in_specs += [pl.BlockSpec((tm, y.shape[1]), lambda i: (i, 0)) for y in ys]
    in_specs += [resident(wout), vec, vec, resident(wup), resident(wdown), vec, vec]
    return pl.pallas_call(
        functools.partial(_post_kernel, len(ys), nsplit),
        grid=(m // tm,),
        in_specs=in_specs,
        out_specs=pl.BlockSpec((tm, D_MODEL), lambda i: (i, 0)),
        out_shape=jax.ShapeDtypeStruct((m, D_MODEL), F32),
        compiler_params=pltpu.CompilerParams(
            dimension_semantics=("arbitrary",), vmem_limit_bytes=VMEM_LIMIT),
        name="post_block",
    )(x2d, *ys, wout, g1, b1, wup, wdown, g2, b2)


def _trunk(x, st_shift, st_wkv, st_ret, st_conv, pos0, blk, w, dense):
    bsz, tlen, _ = x.shape
    m = bsz * tlen
    x2d = x.reshape(m, D_MODEL)
    row = lambda a: a.reshape(1, -1)
    cast = dense is None
    dense = {} if cast else dense
    side = lambda *items: items if cast else ()

    pa, pb = _inproj(x2d, w["w_in_ab"], blk["tm_in"])
    pa3 = pa.reshape(bsz, tlen, A_PROJ)
    rw_prm = (row(w["mu_a"]), row(w["w0"]), w["w2p"], row(w["a0"]), w["a2p"], w["g2"],
              row(w["k_k"]), row(w["k_a"]), row(w["r_k"]), row(w["lnx_g"]), row(w["lnx_b"]))
    ya, wkv1, *made = _rwkv(pa3, st_shift, st_wkv, rw_prm, blk["nb_mix"], blk["tt_mix"],
                            side((w["w_out_ab"], 0), (w["w_up"], 0), (w["w_down"], 0)))
    if cast:
        dense["w_out_ab"], dense["w_up0"], dense["w_down0"] = made
    yb, ret1, *made = _retention(pb.reshape(bsz, tlen, B_PROJ), st_ret, w["invf"], row(w["gn_g"]),
                                 row(w["gn_b"]), blk["nb_mix"], blk["tt_mix"], pos0,
                                 side((w["w_in_conv"], 0), (w["w_out_conv"], 0), (w["w_up"], 1), (w["w_down"], 1)))
    if cast:
        dense["w_in_conv"], dense["w_out_conv"], dense["w_up1"], dense["w_down1"] = made
    shift1 = pa3[:, tlen - 1, :]
    x2d = _post(x2d, [ya.reshape(m, WIDTH), yb.reshape(m, WIDTH)], dense["w_out_ab"],
                row(w["ln1_g"][0]), row(w["ln1_b"][0]), dense["w_up0"], dense["w_down0"],
                row(w["ln2_g"][0]), row(w["ln2_b"][0]), blk["tm_post"], blk["split"])

    z, conv1 = _conv_mixer(x2d.reshape(bsz, tlen, D_MODEL), st_conv, dense["w_in_conv"], w["conv_w"],
                           blk["nb_conv"], blk["tt_conv"], blk["split_conv"])
    x2d = _post(x2d, [z], dense["w_out_conv"], row(w["ln1_g"][1]), row(w["ln1_b"][1]),
                dense["w_up1"], dense["w_down1"], row(w["ln2_g"][1]), row(w["ln2_b"][1]),
                blk["tm_post"], blk["split"])
    outs = (x2d.reshape(bsz, tlen, D_MODEL), shift1[None], wkv1[None], ret1[None], conv1[None])
    return outs, dense


def kernel(x_prompt, x_sample, state_shift, state_wkv, state_ret, state_conv, w_in_ab, mu_a, w0, w2, a0, a2,
           g2, k_k, k_a, r_k, lnx_g, lnx_b, gn_g, gn_b, w_out_ab, w_in_conv, conv_w, w_out_conv,
           ln1_g, ln1_b, ln2_g, ln2_b, w_up, w_down):
    bp = x_prompt.shape[0]
    half = HEAD_DIM // 2
    inv = ROPE_BASE ** (-jnp.arange(half, dtype=F32) / half)
    zpad = jnp.zeros((LANES - DECAY_LORA, WIDTH), F32)
    w = {
        "w_in_ab": _bf(w_in_ab[0]),
        "mu_a": mu_a[0], "w0": w0[0], "a0": a0[0],
        "w2p": _bf(jnp.concatenate([w2[0], zpad], axis=0)),
        "a2p": _bf(jnp.concatenate([zpad, a2[0]], axis=0)),
        "g2": _bf(g2[0]), "k_k": k_k[0], "k_a": k_a[0], "r_k": r_k[0],
        "lnx_g": lnx_g[0], "lnx_b": lnx_b[0], "gn_g": gn_g[0], "gn_b": gn_b[0],
        "invf": jnp.tile(inv, LANES // half).reshape(1, LANES),
        "w_out_ab": w_out_ab, "w_in_conv": w_in_conv, "conv_w": conv_w[0], "w_out_conv": w_out_conv,
        "ln1_g": ln1_g, "ln1_b": ln1_b, "ln2_g": ln2_g, "ln2_b": ln2_b,
        "w_up": w_up, "w_down": w_down,
    }
    dt = state_wkv.dtype
    z_shift = jnp.zeros((bp, A_PROJ), dt)
    z_state = jnp.zeros((bp, HEADS, HEAD_DIM, HEAD_DIM), dt)
    z_conv = jnp.zeros((bp, CONV_W - 1, D_MODEL), dt)
    blk_p = {"tm_in": 1024, "tm_post": 1024, "split": 4, "nb_mix": 2, "tt_mix": 256,
             "nb_conv": 1, "tt_conv": 1024, "split_conv": 2}
    (y_p, p_shift, p_wkv, p_ret, p_conv), dense = _trunk(x_prompt, z_shift, z_state, z_state, z_conv, 0,
                                                         blk_p, w, None)
    ts = x_sample.shape[1]
    blk_s = {"tm_in": 512, "tm_post": 1024, "split": 4, "nb_mix": 2 * ROWS // ts, "tt_mix": ts,
             "nb_conv": 256 // ts, "tt_conv": ts, "split_conv": 1}
    (y_s, s_shift, s_wkv, s_ret, s_conv), _ = _trunk(x_sample, state_shift[0], state_wkv[0], state_ret[0],
                                                     state_conv[0], PAST_LEN, blk_s, w, dense)
    return (y_p, y_s, p_shift, p_wkv, p_ret, p_conv, s_shift, s_wkv, s_ret, s_conv)
```

```python
import functools

import jax
import jax.numpy as jnp
from jax import lax
from jax.experimental import pallas as pl
from jax.experimental.pallas import tpu as pltpu

F32 = jnp.float32
BF16 = jnp.bfloat16

D_MODEL = 1024
DEPTH = 2
PAST_LEN = 16384
HEADS = 8
HEAD_DIM = 64
WIDTH = HEADS * HEAD_DIM
DECAY_LORA = 64
AAA_LORA = 64
GATE_LORA = 128
A_PROJ = 3 * WIDTH + DECAY_LORA + AAA_LORA + GATE_LORA
B_PROJ = 4 * WIDTH
A_GN_EPS = 64e-5
B_GN_EPS = 1e-5
ROPE_BASE = 10000.0
CONV_W = 3
D_FF = 4 * D_MODEL
LN_EPS = 1e-5
ALPHA = (2.0 * DEPTH) ** 0.25

LANES = 128
SUBLANES = 8
PAIRS = WIDTH // LANES
ROWS = 64
N2 = 2 * ROWS
VMEM_LIMIT = 56 * 1024 * 1024


def _bf(x):
    return x.astype(BF16)


def _dot(a, b):
    return jnp.dot(a, b, preferred_element_type=F32)


def _dot_tb(a, b):
    return lax.dot_general(a, b, (((1,), (1,)), ((), ())), preferred_element_type=F32)


def _dot_ta(a, b):
    return lax.dot_general(a, b, (((0,), (0,)), ((), ())), preferred_element_type=F32)


def _dot_split_lhs(m, x, parts):
    acc = None
    rem = x
    for i in range(parts):
        hi = _bf(rem)
        term = _dot(m, hi)
        acc = term if acc is None else acc + term
        if i + 1 < parts:
            rem = rem - hi.astype(F32)
    return acc


def _iota2(shape, dim):
    return lax.broadcasted_iota(jnp.int32, shape, dim)


def _div2(x, n):
    assert n & (n - 1) == 0
    return lax.shift_right_logical(x, n.bit_length() - 1)


def _mod2(x, n):
    assert n & (n - 1) == 0
    return lax.bitwise_and(x, n - 1)


def _head_ones():
    ri = _iota2((LANES, LANES), 0)
    ci = _iota2((LANES, LANES), 1)
    return jnp.where(_div2(ri, HEAD_DIM) == _div2(ci, HEAD_DIM), 1.0, 0.0).astype(BF16)


def _headsum(x, ones_bd):
    xb = _bf(x)
    return jnp.concatenate([_dot(xb[:, LANES * j:LANES * (j + 1)], ones_bd) for j in range(PAIRS)], axis=1)


def _head_norm(o, g, b, eps, ones_bd):
    mu = _headsum(o, ones_bd) * (1.0 / HEAD_DIM)
    d = o - mu
    var = _headsum(d * d, ones_bd) * (1.0 / HEAD_DIM)
    return d * lax.rsqrt(var + eps) * g + b


def _layer_norm(z, g, b):
    mu = jnp.mean(z, axis=-1, keepdims=True)
    d = z - mu
    var = jnp.mean(d * d, axis=-1, keepdims=True)
    return d * lax.rsqrt(var + LN_EPS) * g + b


def _bdiag(x, lane_lo):
    zero = jnp.zeros_like(x)
    return jnp.concatenate([jnp.where(lane_lo, x, zero), jnp.where(lane_lo, zero, x)], axis=0)


def _unstack_heads(x, lane_lo):
    return jnp.where(lane_lo, x[0:ROWS], x[ROWS:])


def _stream_masks(ct, width):
    ri = _iota2((N2, width), 0)
    ci = _mod2(_iota2((N2, width), 1), N2)
    same = _div2(ri, ct) == _div2(ci, ct)
    return same & (ci < ri), same & (ci <= ri), ri == ci, _div2(ri, ROWS) == _div2(ci, ROWS)


def _load_state(s_scr, s0_ref, nb):
    z = jnp.zeros((HEAD_DIM, HEAD_DIM), F32)
    for s in range(nb):
        for j in range(PAIRS):
            top = jnp.concatenate([s0_ref[s, 2 * j], z], axis=1)
            bot = jnp.concatenate([z, s0_ref[s, 2 * j + 1]], axis=1)
            s_scr[s, j] = jnp.concatenate([top, bot], axis=0)


def _store_state(sout_ref, s_scr, nb):
    for s in range(nb):
        for j in range(PAIRS):
            s2 = s_scr[s, j]
            sout_ref[s, 2 * j] = s2[0:HEAD_DIM, 0:HEAD_DIM]
            sout_ref[s, 2 * j + 1] = s2[HEAD_DIM:, HEAD_DIM:]


def _seq_rows(x, q, ct):
    if ct == ROWS:
        return x
    return jnp.concatenate([x[q * ct:(q + 1) * ct], x[ROWS + q * ct:ROWS + (q + 1) * ct]], axis=0)


def _from_seq_rows(pieces, ct):
    if ct == ROWS:
        return pieces[0]
    return jnp.concatenate([p[0:ct] for p in pieces] + [p[ct:2 * ct] for p in pieces], axis=0)


def _pad_rows(x, rows):
    if x.shape[0] == rows:
        return x
    return jnp.concatenate([x, jnp.zeros((rows - x.shape[0], x.shape[1]), x.dtype)], axis=0)


def _emit_by_level(steps, envs):
    level = {}
    for name, deps, _ in steps:
        level[name] = 1 + max([level.get(d, 0) for d in deps], default=0)
    for lv in sorted(set(level.values())):
        for name, deps, fn in steps:
            if level[name] == lv:
                for env in envs:
                    env[name] = fn(*[env[d] for d in deps])


def _inverse_steps(nfac, eye):
    def advance(last):
        def fn(q, p):
            pb = _bf(p)
            if last:
                return p + _dot(q, pb), None
            x = _dot(q, jnp.concatenate([pb, q], axis=1))
            return p + x[:, 0:N2], _bf(x[:, N2:])
        return fn

    steps = [
        ("q0", ("l",), _bf),
        ("p1", ("l",), lambda l: l + eye()),
        ("q1", ("q0",), lambda q: _bf(_dot(q, q))),
    ]
    for i in range(1, nfac):
        steps.append((f"s{i}", (f"q{i}", f"p{i}"), advance(i == nfac - 1)))
        steps.append((f"p{i + 1}", (f"s{i}",), lambda s: s[0]))
        steps.append((f"q{i + 1}", (f"s{i}",), lambda s: s[1]))
    steps.append(("tb", (f"p{nfac}",), _bf))
    return steps


def _rwkv_score_steps(sc_mask):
    return [
        ("scm", ("a2", "r2", "bk"),
         lambda a2, r2, bk: _dot_tb(jnp.concatenate([a2, r2], axis=0), bk) * sc_mask()),
        ("l", ("scm",), lambda scm: scm[0:N2, 0:N2]),
        ("lkb", ("scm",), lambda scm: _bf(scm[:, N2:])),
        ("arb", ("scm",), lambda scm: _bf(scm[N2:, 0:N2])),
        ("lvark", ("lkb", "vst"), _dot),
    ]


def _rwkv_carried_steps(same_head):
    def f_ro(arb, tal, r2, lvark):
        x = _dot(arb, _bf(tal))
        return _bf(r2.astype(F32) + x[:, 0:N2]), x[:, N2:] + lvark[N2:]

    def f_mn(tal, b2, k2, vst):
        mm = _dot_ta(_bf(tal), b2)
        n2 = _dot_ta(vst, k2)
        return _bf(mm[0:N2]), (mm[N2:] + n2) * same_head()

    return [
        ("tal", ("tb", "a2", "lvark"),
         lambda tb, a2, lvark: _dot(tb, jnp.concatenate([a2, _bf(lvark[0:N2])], axis=1))),
        ("ro", ("arb", "tal", "r2", "lvark"), f_ro),
        ("mn", ("tal", "b2", "k2", "vst"), f_mn),
    ]


def _side_casts(side, nsteps, step_of):
    in_specs, out_specs, out_shapes = [], [], []
    for a, layer in side:
        rows, cols = a.shape[-2:]
        slab = rows // nsteps
        assert rows % nsteps == 0 and slab % (2 * SUBLANES) == 0
        if layer is None:
            in_specs.append(pl.BlockSpec((slab, cols), lambda *g: (step_of(*g), 0)))
        else:
            in_specs.append(pl.BlockSpec((None, slab, cols), lambda *g, layer=layer: (layer, step_of(*g), 0)))
        out_specs.append(pl.BlockSpec((slab, cols), lambda *g: (step_of(*g), 0)))
        out_shapes.append(jax.ShapeDtypeStruct((rows, cols), BF16))
    return in_specs, out_specs, out_shapes, [a for a, _ in side]


def _with_side_casts(body, n_in, n_out, nside):
    def kernel(*refs):
        ins, rest = refs[:n_in], refs[n_in:]
        side_in, rest = rest[:nside], rest[nside:]
        outs, rest = rest[:n_out], rest[n_out:]
        side_out, scratch = rest[:nside], rest[nside:]
        for src, dst in zip(side_in, side_out):
            dst[...] = _bf(src[...])
        body(*ins, *outs, *scratch)
    return kernel


def _inproj_kernel(x_ref, w_ref, pa_ref, pb_ref):
    xb = _bf(x_ref[...])
    pa_ref[...] = _dot(xb, w_ref[:, 0:A_PROJ])
    pb_ref[...] = _dot(xb, w_ref[:, A_PROJ:])


def _inproj(x2d, w, tm):
    m = x2d.shape[0]
    return pl.pallas_call(
        _inproj_kernel,
        grid=(m // tm,),
        in_specs=[
            pl.BlockSpec((tm, D_MODEL), lambda i: (i, 0)),
            pl.BlockSpec(w.shape, lambda i: (0, 0), pipeline_mode=pl.Buffered(1)),
        ],
        out_specs=[
            pl.BlockSpec((tm, A_PROJ), lambda i: (i, 0)),
            pl.BlockSpec((tm, B_PROJ), lambda i: (i, 0)),
        ],
        out_shape=[
            jax.ShapeDtypeStruct((m, A_PROJ), F32),
            jax.ShapeDtypeStruct((m, B_PROJ), F32),
        ],
        compiler_params=pltpu.CompilerParams(
            dimension_semantics=("arbitrary",), vmem_limit_bytes=VMEM_LIMIT),
        name="inproj_ab",
    )(x2d, w)


def _rwkv_kernel(nb, tt, ct, pa_ref, shift_ref, s0_ref, mu_ref, w0_ref, w2p_ref, a0_ref, a2p_ref,
                 g2_ref, kk_ref, ka_ref, rk_ref, lng_ref, lnb_ref,
                 ya_ref, sout_ref,
                 s_scr, prev_scr, at_scr, rt_scr, bt_scr, kt_scr, v_scr, p_scr, o_scr,
                 scmask_scr, eye_scr, head_scr, lcum_scr):
    rows = nb * tt
    nchunk = rows // ROWS
    ns = ROWS // ct
    ti = pl.program_id(1)

    @pl.when((pl.program_id(0) == 0) & (ti == 0))
    def _():
        strict, incl, eye, same_head = _stream_masks(ct, 2 * N2)
        one = lambda m: jnp.where(m, 1.0, 0.0)
        scmask_scr[...] = jnp.concatenate([one(strict), one(incl)], axis=0)
        eye_scr[...] = one(eye)[:, 0:N2]
        head_scr[...] = one(same_head)[:, 0:N2]
        ri = _iota2(lcum_scr.shape, 0)
        ci = _iota2(lcum_scr.shape, 1)
        lcum_scr[...] = jnp.where((_div2(ri, ct) == _div2(ci, ct)) & (ci <= ri), 1.0, 0.0).astype(BF16)

    @pl.when(ti == 0)
    def _():
        _load_state(s_scr, s0_ref, nb)
        prev_scr[...] = shift_ref[...]

    ones_bd = _head_ones()

    p3 = pa_ref[...]
    p2 = p3.reshape(rows, A_PROJ)
    t3 = _iota2((nb, tt, A_PROJ), 1)
    rolled = pltpu.roll(p2, 1, axis=0).reshape(nb, tt, A_PROJ)
    pprev = jnp.where(t3 == 0, prev_scr[...], rolled).reshape(rows, A_PROJ)
    prev_scr[...] = p3[:, tt - 1:tt, :]
    m = p2 + (pprev - p2) * mu_ref[...]
    r = m[:, 0:WIDTH]
    k = m[:, WIDTH:2 * WIDTH]
    v = m[:, 2 * WIDTH:3 * WIDTH]
    wa = m[:, 3 * WIDTH:3 * WIDTH + LANES]
    gd = m[:, 3 * WIDTH + LANES:A_PROJ]
    z = -(w0_ref[...] + _dot(_bf(jnp.tanh(wa)), w2p_ref[...]))
    softplus = jnp.maximum(z, 0.0) + jnp.log(1.0 + jnp.exp(-jnp.abs(z)))
    lw = -jnp.exp(-softplus - 0.5)
    a = jax.nn.sigmoid(a0_ref[...] + _dot(_bf(wa), a2p_ref[...]))
    g = _dot(_bf(jax.nn.sigmoid(gd)), g2_ref[...])
    kk = k * kk_ref[...]
    k = k * (1.0 + (a - 1.0) * ka_ref[...])
    kk = kk * lax.rsqrt(jnp.maximum(_headsum(kk * kk, ones_bd), 1e-24))
    bonus = _headsum(r * k * rk_ref[...], ones_bd) * v

    span = lcum_scr.shape[0]
    cs = jnp.concatenate([_dot_split_lhs(lcum_scr[...], lw[i * span:(i + 1) * span], 2)
                          for i in range(rows // span)], axis=0)
    pinv = jnp.exp(-cs)
    p_scr[...] = jnp.exp(cs)
    at_scr[...] = _bf(-kk * jnp.exp(cs - lw))
    rt_scr[...] = _bf(r * p_scr[...])
    bt_scr[...] = _bf(kk * a * pinv)
    kt_scr[...] = _bf(k * pinv)
    v_scr[...] = _bf(v)

    lane_lo = _iota2((ROWS, LANES), 1) < HEAD_DIM
    sc_mask = lambda: scmask_scr[...]
    eye = lambda: eye_scr[...]
    same_head = lambda: head_scr[...]
    nfac = ct.bit_length() - 1

    envs = []
    for c in range(nchunk):
        for j in range(PAIRS):
            rsl = slice(c * ROWS, (c + 1) * ROWS)
            lsl = slice(LANES * j, LANES * (j + 1))
            aj, rj, bj, kj, vj = (s[rsl, lsl] for s in (at_scr, rt_scr, bt_scr, kt_scr, v_scr))
            envs.append({
                "c": c, "j": j, "rsl": rsl, "lsl": lsl,
                "a2": _bdiag(aj, lane_lo), "r2": _bdiag(rj, lane_lo),
                "b2": _bdiag(bj, lane_lo), "k2": _bdiag(kj, lane_lo),
                "bk": jnp.concatenate([bj, bj, kj, kj], axis=0),
                "vst": jnp.concatenate([vj, vj], axis=0),
            })

    steps = _rwkv_score_steps(sc_mask) + _inverse_steps(nfac, eye)
    if ns == 1:
        _emit_by_level(steps + _rwkv_carried_steps(same_head), envs)
        per_seq = tt // ROWS
        state = [[s_scr[s, j] for j in range(PAIRS)] for s in range(nb)]
        for cpos in range(per_seq):
            for s in range(nb):
                c = s * per_seq + cpos
                cenv = envs[c * PAIRS:(c + 1) * PAIRS]
                sb = [_bf(x) for x in state[s]]
                for j, env in enumerate(cenv):
                    rp, oc = env["ro"]
                    o_scr[env["rsl"], env["lsl"]] = _unstack_heads(_dot_tb(rp, sb[j]) + oc, lane_lo)
                for j, env in enumerate(cenv):
                    mk, nn = env["mn"]
                    pc = p_scr[(c + 1) * ROWS - 1:(c + 1) * ROWS, env["lsl"]]
                    state[s][j] = (state[s][j] + _dot(sb[j], mk) + nn) * pc
        for s in range(nb):
            for j in range(PAIRS):
                s_scr[s, j] = state[s][j]
    else:
        for env in envs:
            env["sb"] = [_bf(s_scr[env["c"] * ns + q, env["j"]]) for q in range(ns)]

        def f_h(a2, r2, sb):
            pieces = [_dot_tb(jnp.concatenate([_seq_rows(a2, q, ct), _seq_rows(r2, q, ct)], axis=0), sb[q])
                      for q in range(ns)]
            return (_from_seq_rows([x[0:2 * ct] for x in pieces], ct),
                    _from_seq_rows([x[2 * ct:] for x in pieces], ct))

        def f_ds(u2, vst, b2, k2):
            vst32 = vst.astype(F32)
            out = []
            for q in range(ns):
                uv = jnp.concatenate([_seq_rows(u2, q, ct), _seq_rows(vst32, q, ct)], axis=0)
                bk = jnp.concatenate([_seq_rows(b2, q, ct), _seq_rows(k2, q, ct)], axis=0)
                out.append(_dot_ta(_bf(_pad_rows(uv, LANES)), _pad_rows(bk, LANES)) * same_head())
            return out

        steps += [
            ("h", ("a2", "r2", "sb"), f_h),
            ("u2", ("tb", "h", "lvark"), lambda tb, h, lvark: _dot(tb, _bf(h[0] + lvark[0:N2]))),
            ("o2", ("arb", "u2", "h", "lvark"),
             lambda arb, u2, h, lvark: h[1] + _dot(arb, _bf(u2)) + lvark[N2:]),
            ("ds", ("u2", "vst", "b2", "k2"), f_ds),
        ]
        _emit_by_level(steps, envs)
        for env in envs:
            o_scr[env["rsl"], env["lsl"]] = _unstack_heads(env["o2"], lane_lo)
            for q in range(ns):
                seq = env["c"] * ns + q
                pc = p_scr[(seq + 1) * ct - 1:(seq + 1) * ct, env["lsl"]]
                s_scr[seq, env["j"]] = (s_scr[seq, env["j"]] + env["ds"][q]) * pc

    o = o_scr[...]
    y = (_head_norm(o, lng_ref[...], lnb_ref[...], A_GN_EPS, ones_bd) + bonus) * g
    ya_ref[...] = _bf(y).reshape(nb, tt, WIDTH)

    @pl.when(ti == pl.num_programs(1) - 1)
    def _():
        _store_state(sout_ref, s_scr, nb)


def _row_spec(shape):
    return pl.BlockSpec(shape, lambda b, t: (0,) * len(shape))


def _check_mixer_blocking(bsz, tlen, nb, tt):
    ct = min(ROWS, tt)
    rows = nb * tt
    assert rows % ROWS == 0 and ROWS % ct == 0 and bsz % nb == 0 and tlen % tt == 0
    assert tt % ROWS == 0 or tt == tlen
    return ct, rows


def _rwkv(pa3, shift0, s0, prm, nb, tt, side=()):
    bsz, tlen, _ = pa3.shape
    ct, rows = _check_mixer_blocking(bsz, tlen, nb, tt)
    span = tt if tt % ROWS == 0 else rows
    grid = (bsz // nb, tlen // tt)
    s_in, s_out, s_shapes, s_args = _side_casts(side, grid[0] * grid[1], lambda b, t: b * grid[1] + t)
    state_spec = pl.BlockSpec((nb, HEADS, HEAD_DIM, HEAD_DIM), lambda b, t: (b, 0, 0, 0))
    vec = lambda n: _row_spec((1, n))
    in_specs = [
        pl.BlockSpec((nb, tt, A_PROJ), lambda b, t: (b, t, 0)),
        pl.BlockSpec((nb, 1, A_PROJ), lambda b, t: (b, 0, 0)),
        state_spec,
        vec(A_PROJ), vec(WIDTH), _row_spec((LANES, WIDTH)), vec(WIDTH), _row_spec((LANES, WIDTH)),
        _row_spec((GATE_LORA, WIDTH)), vec(WIDTH), vec(WIDTH), vec(WIDTH), vec(WIDTH), vec(WIDTH),
    ]
    out_specs = [
        pl.BlockSpec((nb, tt, WIDTH), lambda b, t: (b, t, 0)),
        state_spec,
    ]
    scratch = [
        pltpu.VMEM((nb, PAIRS, LANES, LANES), F32),
        pltpu.VMEM((nb, 1, A_PROJ), F32),
        pltpu.VMEM((rows, WIDTH), BF16),
        pltpu.VMEM((rows, WIDTH), BF16),
        pltpu.VMEM((rows, WIDTH), BF16),
        pltpu.VMEM((rows, WIDTH), BF16),
        pltpu.VMEM((rows, WIDTH), BF16),
        pltpu.VMEM((rows, WIDTH), F32),
        pltpu.VMEM((rows, WIDTH), F32),
        pltpu.VMEM((2 * N2, 2 * N2), F32),
        pltpu.VMEM((N2, N2), F32),
        pltpu.VMEM((N2, N2), F32),
        pltpu.VMEM((span, span), BF16),
    ]
    return pl.pallas_call(
        _with_side_casts(functools.partial(_rwkv_kernel, nb, tt, ct), len(in_specs), len(out_specs), len(side)),
        grid=grid,
        in_specs=in_specs + s_in,
        out_specs=out_specs + s_out,
        out_shape=[
            jax.ShapeDtypeStruct((bsz, tlen, WIDTH), BF16),
            jax.ShapeDtypeStruct(s0.shape, F32),
        ] + s_shapes,
        scratch_shapes=scratch,
        compiler_params=pltpu.CompilerParams(
            dimension_semantics=("arbitrary", "arbitrary"), vmem_limit_bytes=VMEM_LIMIT),
        name="rwkv7_mixer",
    )(pa3, shift0.reshape(bsz, 1, A_PROJ), s0, *prm, *s_args)


def _ret_kernel(nb, tt, ct, pos0, pb_ref, s0_ref, invf_ref, gng_ref, gnb_ref,
                yb_ref, sout_ref,
                s_scr, q_scr, k_scr, v_scr, o_scr, dec_scr, head_scr, trig_scr):
    rows = nb * tt
    nchunk = rows // ROWS
    ns = ROWS // ct
    ti = pl.program_id(1)

    @pl.when((pl.program_id(0) == 0) & (ti == 0))
    def _():
        _, incl, _, same_head = _stream_masks(ct, N2)
        head_scr[...] = jnp.where(same_head, 1.0, 0.0)
        ri = _iota2((N2, N2), 0)
        ci = _iota2((N2, N2), 1)
        tpos = _mod2(ri, ct).astype(F32)
        tdiff = (_mod2(ri, ct) - _mod2(ci, ct)).astype(F32)
        for j in range(PAIRS):
            head = (2 * j + _div2(ri, ROWS)).astype(F32)
            lg = jnp.log1p(-jnp.exp2(-5.0 - head))
            dec_scr[j, 0] = jnp.where(incl, jnp.exp(lg * jnp.maximum(tdiff, 0.0)), 0.0)
            dec_scr[j, 1] = jnp.exp(lg * (tpos + 1.0))
            dec_scr[j, 2] = jnp.exp(lg * (ct - 1.0 - tpos))
            dec_scr[j, 3] = jnp.exp(lg * ct)
        off = _mod2(_iota2((rows, LANES), 0), tt).astype(F32) * invf_ref[...]
        trig_scr[0] = jnp.cos(off)
        trig_scr[1] = jnp.sin(off)

    @pl.when(ti == 0)
    def _():
        _load_state(s_scr, s0_ref, nb)

    ones_bd = _head_ones()
    pb = pb_ref[...].reshape(rows, B_PROJ)
    q = pb[:, 0:WIDTH]
    k = pb[:, WIDTH:2 * WIDTH]
    gate = pb[:, 3 * WIDTH:]

    base = jnp.zeros((SUBLANES, LANES), F32) + (pos0 + ti * tt).astype(F32)
    ang = base * invf_ref[...]
    cos_a, sin_a = jnp.cos(ang)[0:1], jnp.sin(ang)[0:1]
    cos_b, sin_b = trig_scr[0], trig_scr[1]
    cos = jnp.concatenate([cos_a * cos_b - sin_a * sin_b] * PAIRS, axis=1)
    sin = jnp.concatenate([sin_a * cos_b + cos_a * sin_b] * PAIRS, axis=1)
    first_half = _mod2(_iota2((rows, WIDTH), 1), HEAD_DIM) < (HEAD_DIM // 2)
    sin = jnp.where(first_half, -sin, sin)

    pr = _iota2((LANES, LANES), 0)
    pc = _iota2((LANES, LANES), 1)
    swap = jnp.where(pr == lax.bitwise_xor(pc, HEAD_DIM // 2), 1.0, 0.0).astype(BF16)

    def rope(x):
        xb = _bf(x)
        partner = jnp.concatenate([_dot(xb[:, LANES * j:LANES * (j + 1)], swap) for j in range(PAIRS)], axis=1)
        return x * cos + partner * sin

    q_scr[...] = _bf(rope(q))
    k_scr[...] = rope(k) * (HEAD_DIM ** -0.5)
    v_scr[...] = _bf(pb[:, 2 * WIDTH:3 * WIDTH])

    lane_lo = _iota2((ROWS, LANES), 1) < HEAD_DIM
    same_head = lambda: head_scr[...]

    envs = []
    for c in range(nchunk):
        for j in range(PAIRS):
            rsl = slice(c * ROWS, (c + 1) * ROWS)
            lsl = slice(LANES * j, LANES * (j + 1))
            qj, kj, vj = q_scr[rsl, lsl], k_scr[rsl, lsl], v_scr[rsl, lsl]
            kjb = _bf(kj)
            envs.append({
                "c": c, "j": j, "rsl": rsl, "lsl": lsl,
                "q2": _bdiag(qj, lane_lo),
                "kst": jnp.concatenate([kjb, kjb], axis=0),
                "k2d": _bdiag(kj, lane_lo) * dec_scr[j, 2],
                "vst": jnp.concatenate([vj, vj], axis=0),
                "dmask": lambda j=j: dec_scr[j, 0],
            })

    def f_ds(k2d, vst):
        return [_dot_ta(_bf(_pad_rows(_seq_rows(k2d, s, ct), LANES)), _pad_rows(_seq_rows(vst, s, ct), LANES))
                * same_head() for s in range(ns)]

    steps = [
        ("sc", ("q2", "kst", "dmask"), lambda q2, kst, dmask: _bf(_dot_tb(q2, kst) * dmask())),
        ("inner", ("sc", "vst"), _dot),
        ("ds", ("k2d", "vst"), f_ds),
    ]
    _emit_by_level(steps, envs)

    if ns == 1:
        per_seq = tt // ROWS
        for s in range(nb):
            state = [s_scr[s, j] for j in range(PAIRS)]
            for c in range(s * per_seq, (s + 1) * per_seq):
                for j in range(PAIRS):
                    env = envs[c * PAIRS + j]
                    env["sb"] = [_bf(state[j])]
                    state[j] = state[j] * dec_scr[j, 3] + env["ds"][0]
            for j in range(PAIRS):
                s_scr[s, j] = state[j]
    else:
        for env in envs:
            j = env["j"]
            first = env["c"] * ns
            env["sb"] = [_bf(s_scr[first + s, j]) for s in range(ns)]
            for s in range(ns):
                s_scr[first + s, j] = s_scr[first + s, j] * dec_scr[j, 3] + env["ds"][s]

    def f_cross(q2, sb):
        return _from_seq_rows([_dot(_seq_rows(q2, s, ct), sb[s]) for s in range(ns)], ct)

    _emit_by_level([("cross", ("q2", "sb"), f_cross)], envs)
    for env in envs:
        o2 = env["inner"] + env["cross"] * dec_scr[env["j"], 1]
        o_scr[env["rsl"], env["lsl"]] = _unstack_heads(o2, lane_lo)

    o = o_scr[...]
    y = jax.nn.silu(gate) * _head_norm(o, gng_ref[...], gnb_ref[...], B_GN_EPS, ones_bd)
    yb_ref[...] = _bf(y).reshape(nb, tt, WIDTH)

    @pl.when(ti == pl.num_programs(1) - 1)
    def _():
        _store_state(sout_ref, s_scr, nb)


def _retention(pb3, s0, invf, gn_g, gn_b, nb, tt, pos0, side=()):
    bsz, tlen, _ = pb3.shape
    ct, rows = _check_mixer_blocking(bsz, tlen, nb, tt)
    grid = (bsz // nb, tlen // tt)
    s_in, s_out, s_shapes, s_args = _side_casts(side, grid[0] * grid[1], lambda b, t: b * grid[1] + t)
    state_spec = pl.BlockSpec((nb, HEADS, HEAD_DIM, HEAD_DIM), lambda b, t: (b, 0, 0, 0))
    in_specs = [
        pl.BlockSpec((nb, tt, B_PROJ), lambda b, t: (b, t, 0)),
        state_spec,
        _row_spec((1, LANES)), _row_spec((1, WIDTH)), _row_spec((1, WIDTH)),
    ]
    out_specs = [
        pl.BlockSpec((nb, tt, WIDTH), lambda b, t: (b, t, 0)),
        state_spec,
    ]
    scratch = [
        pltpu.VMEM((nb, PAIRS, LANES, LANES), F32),
        pltpu.VMEM((rows, WIDTH), BF16),
        pltpu.VMEM((rows, WIDTH), F32),
        pltpu.VMEM((rows, WIDTH), BF16),
        pltpu.VMEM((rows, WIDTH), F32),
        pltpu.VMEM((PAIRS, 4, N2, N2), F32),
        pltpu.VMEM((N2, N2), F32),
        pltpu.VMEM((2, rows, LANES), F32),
    ]
    return pl.pallas_call(
        _with_side_casts(functools.partial(_ret_kernel, nb, tt, ct, pos0), len(in_specs), len(out_specs),
                         len(side)),
        grid=grid,
        in_specs=in_specs + s_in,
        out_specs=out_specs + s_out,
        out_shape=[
            jax.ShapeDtypeStruct((bsz, tlen, WIDTH), BF16),
            jax.ShapeDtypeStruct(s0.shape, F32),
        ] + s_shapes,
        scratch_shapes=scratch,
        compiler_params=pltpu.CompilerParams(
            dimension_semantics=("arbitrary", "arbitrary"), vmem_limit_bytes=VMEM_LIMIT),
        name="retention_mixer",
    )(pb3, s0, invf, gn_g, gn_b, *s_args)


CONV_COLS = 256


def _conv_kernel(nb, tt, nsplit, x_ref, buf_ref, win_ref, cw_ref, z_ref, bout_ref, halo_scr):
    rows = nb * tt
    sub = rows // nsplit
    st = tt // nsplit if nb == 1 else tt
    ti = pl.program_id(1)

    @pl.when(ti == 0)
    def _():
        halo_scr[...] = buf_ref[...]

    t3 = _iota2((nb, st, CONV_COLS), 1)

    def project(xb, lo):
        col = lambda k: win_ref[:, k * D_MODEL + lo:k * D_MODEL + lo + CONV_COLS]
        return _dot(xb, col(0)), _dot(xb, col(1)) * _dot(xb, col(2))

    def mix(i, lo, h0, h1, bg, u):
        cw = cw_ref[:, lo:lo + CONV_COLS]
        u3 = u.reshape(nb, st, CONV_COLS)
        prev1 = jnp.where(t3 == 0, h1, pltpu.roll(u, 1, axis=0).reshape(nb, st, CONV_COLS))
        prev2 = jnp.where(t3 == 0, h0,
                          jnp.where(t3 == 1, h1, pltpu.roll(u, 2, axis=0).reshape(nb, st, CONV_COLS)))
        conv = prev2 * cw[0:1, :] + prev1 * cw[1:2, :] + u3 * cw[2:3, :]
        z_ref[i * sub:(i + 1) * sub, lo:lo + CONV_COLS] = _bf(bg * conv.reshape(sub, CONV_COLS))

    halo = {lo: (halo_scr[:, 0:1, lo:lo + CONV_COLS], halo_scr[:, 1:2, lo:lo + CONV_COLS])
            for lo in range(0, D_MODEL, CONV_COLS)}
    pending = None
    for i in range(nsplit):
        xi = x_ref[0, i * sub:(i + 1) * sub, :] if nb == 1 else x_ref[...].reshape(rows, D_MODEL)
        xb = _bf(xi)
        for lo in range(0, D_MODEL, CONV_COLS):
            bg, u = project(xb, lo)
            if pending is not None:
                mix(*pending)
            pending = (i, lo, *halo[lo], bg, u)
            u3 = u.reshape(nb, st, CONV_COLS)
            halo[lo] = (u3[:, st - 2:st - 1, :], u3[:, st - 1:st, :])
    mix(*pending)
    for lo, (h0, h1) in halo.items():
        halo_scr[:, 0:1, lo:lo + CONV_COLS] = h0
        halo_scr[:, 1:2, lo:lo + CONV_COLS] = h1

    @pl.when(ti == pl.num_programs(1) - 1)
    def _():
        bout_ref[...] = halo_scr[...]


def _conv_mixer(x3, buf0, win, cw, nb, tt, nsplit):
    bsz, tlen, _ = x3.shape
    rows = nb * tt
    assert bsz % nb == 0 and tlen % tt == 0
    assert nsplit == 1 or nb == 1
    assert tt % nsplit == 0 and tt // nsplit >= CONV_W - 1
    grid = (bsz // nb, tlen // tt)
    buf_spec = pl.BlockSpec((nb, CONV_W - 1, D_MODEL), lambda b, t: (b, 0, 0))
    return pl.pallas_call(
        functools.partial(_conv_kernel, nb, tt, nsplit),
        grid=grid,
        in_specs=[
            pl.BlockSpec((nb, tt, D_MODEL), lambda b, t: (b, t, 0)),
            buf_spec,
            pl.BlockSpec(win.shape, lambda b, t: (0, 0), pipeline_mode=pl.Buffered(1)),
            _row_spec((CONV_W, D_MODEL)),
        ],
        out_specs=[
            pl.BlockSpec((rows, D_MODEL), lambda b, t: (b * (tlen // tt) + t, 0)),
            buf_spec,
        ],
        out_shape=[
            jax.ShapeDtypeStruct((bsz * tlen, D_MODEL), BF16),
            jax.ShapeDtypeStruct(buf0.shape, F32),
        ],
        scratch_shapes=[pltpu.VMEM((nb, CONV_W - 1, D_MODEL), F32)],
        compiler_params=pltpu.CompilerParams(
            dimension_semantics=("arbitrary", "arbitrary"), vmem_limit_bytes=VMEM_LIMIT),
        name="conv_mixer",
    )(x3, buf0, win, cw)


FF_CHUNK = 1024


def _post_kernel(npieces, nsplit, *refs):
    x_ref = refs[0]
    y_refs = refs[1:1 + npieces]
    wout_ref, g1_ref, b1_ref, wup_ref, wdown_ref, g2_ref, b2_ref, out_ref = refs[1 + npieces:]
    sub = x_ref.shape[0] // nsplit

    def f_y(r):
        y = None
        off = 0
        for y_ref in y_refs:
            n = y_ref.shape[1]
            term = _dot(y_ref[r, :], wout_ref[off:off + n, :])
            y = term if y is None else y + term
            off += n
        return y

    def f_mlp(f):
        def step(x1b, acc):
            h = jnp.maximum(_dot(x1b, wup_ref[:, f * FF_CHUNK:(f + 1) * FF_CHUNK]), 0.0)
            term = _dot(_bf(h * h), wdown_ref[f * FF_CHUNK:(f + 1) * FF_CHUNK, :])
            return term if acc is None else acc + term
        return step

    def f_out(r, x1, acc):
        out_ref[r, :] = _layer_norm(ALPHA * x1 + acc, g2_ref[...], b2_ref[...])
        return None

    steps = [
        ("y", ("r",), f_y),
        ("x1", ("r", "y"), lambda r, y: _layer_norm(ALPHA * x_ref[r, :] + y, g1_ref[...], b1_ref[...])),
        ("x1b", ("x1",), _bf),
        ("acc-1", ("x1b",), lambda x1b: None),
    ]
    nff = D_FF // FF_CHUNK
    for f in range(nff):
        steps.append((f"acc{f}", ("x1b", f"acc{f - 1}"), f_mlp(f)))
    steps.append(("out", ("r", "x1", f"acc{nff - 1}"), f_out))
    envs = [{"r": slice(i * sub, (i + 1) * sub)} for i in range(nsplit)]
    _emit_by_level(steps, envs)


def _post(x2d, ys, wout, g1, b1, wup, wdown, g2, b2, tm, nsplit):
    m = x2d.shape[0]
    assert m % tm == 0 and tm % nsplit == 0
    const = lambda i: (0, 0)
    resident = lambda a: pl.BlockSpec(a.shape, const, pipeline_mode=pl.Buffered(1))
    vec = pl.BlockSpec((1, D_MODEL), const)
    in_specs = [pl.BlockSpec((tm, D_MODEL), lambda i: (i, 0))]
    in_specs += [pl.BlockSpec((tm, y.shape[1]), lambda i: (i, 0)) for y in ys]
    in_specs += [resident(wout), vec, vec, resident(wup), resident(wdown), vec, vec]
    return pl.pallas_call(
        functools.partial(_post_kernel, len(ys), nsplit),
        grid=(m // tm,),
        in_specs=in_specs,
        out_specs=pl.BlockSpec((tm, D_MODEL), lambda i: (i, 0)),
        out_shape=jax.ShapeDtypeStruct((m, D_MODEL), F32),
        compiler_params=pltpu.CompilerParams(
            dimension_semantics=("arbitrary",), vmem_limit_bytes=VMEM_LIMIT),
        name="post_block",
    )(x2d, *ys, wout, g1, b1, wup, wdown, g2, b2)


def _trunk(x, st_shift, st_wkv, st_ret, st_conv, pos0, blk, w, dense):
    bsz, tlen, _ = x.shape
    m = bsz * tlen
    x2d = x.reshape(m, D_MODEL)
    row = lambda a: a.reshape(1, -1)
    cast = dense is None
    dense = {} if cast else dense
    side = lambda *items: items if cast else ()

    pa, pb = _inproj(x2d, w["w_in_ab"], blk["tm_in"])
    pa3 = pa.reshape(bsz, tlen, A_PROJ)
    rw_prm = (row(w["mu_a"]), row(w["w0"]), w["w2p"], row(w["a0"]), w["a2p"], w["g2"],
              row(w["k_k"]), row(w["k_a"]), row(w["r_k"]), row(w["lnx_g"]), row(w["lnx_b"]))
    ya, wkv1, *made = _rwkv(pa3, st_shift, st_wkv, rw_prm, blk["nb_mix"], blk["tt_mix"],
                            side((w["w_out_ab"], 0), (w["w_up"], 0), (w["w_down"], 0)))
    if cast:
        dense["w_out_ab"], dense["w_up0"], dense["w_down0"] = made
    yb, ret1, *made = _retention(pb.reshape(bsz, tlen, B_PROJ), st_ret, w["invf"], row(w["gn_g"]),
                                 row(w["gn_b"]), blk["nb_mix"], blk["tt_mix"], pos0,
                                 side((w["w_in_conv"], 0), (w["w_out_conv"], 0), (w["w_up"], 1), (w["w_down"], 1)))
    if cast:
        dense["w_in_conv"], dense["w_out_conv"], dense["w_up1"], dense["w_down1"] = made
    shift1 = pa3[:, tlen - 1, :]
    x2d = _post(x2d, [ya.reshape(m, WIDTH), yb.reshape(m, WIDTH)], dense["w_out_ab"],
                row(w["ln1_g"][0]), row(w["ln1_b"][0]), dense["w_up0"], dense["w_down0"],
                row(w["ln2_g"][0]), row(w["ln2_b"][0]), blk["tm_post"], blk["split"])

    z, conv1 = _conv_mixer(x2d.reshape(bsz, tlen, D_MODEL), st_conv, dense["w_in_conv"], w["conv_w"],
                           blk["nb_conv"], blk["tt_conv"], blk["split_conv"])
    x2d = _post(x2d, [z], dense["w_out_conv"], row(w["ln1_g"][1]), row(w["ln1_b"][1]),
                dense["w_up1"], dense["w_down1"], row(w["ln2_g"][1]), row(w["ln2_b"][1]),
                blk["tm_post"], blk["split"])
    outs = (x2d.reshape(bsz, tlen, D_MODEL), shift1[None], wkv1[None], ret1[None], conv1[None])
    return outs, dense


def kernel(x_prompt, x_sample, state_shift, state_wkv, state_ret, state_conv, w_in_ab, mu_a, w0, w2, a0, a2,
           g2, k_k, k_a, r_k, lnx_g, lnx_b, gn_g, gn_b, w_out_ab, w_in_conv, conv_w, w_out_conv,
           ln1_g, ln1_b, ln2_g, ln2_b, w_up, w_down):
    bp = x_prompt.shape[0]
    half = HEAD_DIM // 2
    inv = ROPE_BASE ** (-jnp.arange(half, dtype=F32) / half)
    zpad = jnp.zeros((LANES - DECAY_LORA, WIDTH), F32)
    w = {
        "w_in_ab": _bf(w_in_ab[0]),
        "mu_a": mu_a[0], "w0": w0[0], "a0": a0[0],
        "w2p": _bf(jnp.concatenate([w2[0], zpad], axis=0)),
        "a2p": _bf(jnp.concatenate([zpad, a2[0]], axis=0)),
        "g2": _bf(g2[0]), "k_k": k_k[0], "k_a": k_a[0], "r_k": r_k[0],
        "lnx_g": lnx_g[0], "lnx_b": lnx_b[0], "gn_g": gn_g[0], "gn_b": gn_b[0],
        "invf": jnp.tile(inv, LANES // half).reshape(1, LANES),
        "w_out_ab": w_out_ab, "w_in_conv": w_in_conv, "conv_w": conv_w[0], "w_out_conv": w_out_conv,
        "ln1_g": ln1_g, "ln1_b": ln1_b, "ln2_g": ln2_g, "ln2_b": ln2_b,
        "w_up": w_up, "w_down": w_down,
    }
    dt = state_wkv.dtype
    z_shift = jnp.zeros((bp, A_PROJ), dt)
    z_state = jnp.zeros((bp, HEADS, HEAD_DIM, HEAD_DIM), dt)
    z_conv = jnp.zeros((bp, CONV_W - 1, D_MODEL), dt)
    blk_p = {"tm_in": 1024, "tm_post": 1024, "split": 4, "nb_mix": 2, "tt_mix": 256,
             "nb_conv": 1, "tt_conv": 1024, "split_conv": 1}
    (y_p, p_shift, p_wkv, p_ret, p_conv), dense = _trunk(x_prompt, z_shift, z_state, z_state, z_conv, 0,
                                                         blk_p, w, None)
    ts = x_sample.shape[1]
    blk_s = {"tm_in": 512, "tm_post": 1024, "split": 4, "nb_mix": 2 * ROWS // ts, "tt_mix": ts,
             "nb_conv": 256 // ts, "tt_conv": ts, "split_conv": 1}
    (y_s, s_shift, s_wkv, s_ret, s_conv), _ = _trunk(x_sample, state_shift[0], state_wkv[0], state_ret[0],
                                                     state_conv[0], PAST_LEN, blk_s, w, dense)
    return (y_p, y_s, p_shift, p_wkv, p_ret, p_conv, s_shift, s_wkv, s_ret, s_conv)
```

```python
import functools

import jax
import jax.numpy as jnp
from jax import lax
from jax.experimental import pallas as pl
from jax.experimental.pallas import tpu as pltpu

F32 = jnp.float32
BF16 = jnp.bfloat16

D_MODEL = 1024
DEPTH = 2
PAST_LEN = 16384
HEADS = 8
HEAD_DIM = 64
WIDTH = HEADS * HEAD_DIM
DECAY_LORA = 64
AAA_LORA = 64
GATE_LORA = 128
A_PROJ = 3 * WIDTH + DECAY_LORA + AAA_LORA + GATE_LORA
B_PROJ = 4 * WIDTH
A_GN_EPS = 64e-5
B_GN_EPS = 1e-5
ROPE_BASE = 10000.0
CONV_W = 3
D_FF = 4 * D_MODEL
LN_EPS = 1e-5
ALPHA = (2.0 * DEPTH) ** 0.25

LANES = 128
SUBLANES = 8
PAIRS = WIDTH // LANES
ROWS = 64
N2 = 2 * ROWS
VMEM_LIMIT = 56 * 1024 * 1024


def _bf(x):
    return x.astype(BF16)


def _dot(a, b):
    return jnp.dot(a, b, preferred_element_type=F32)


def _dot_tb(a, b):
    return lax.dot_general(a, b, (((1,), (1,)), ((), ())), preferred_element_type=F32)


def _dot_ta(a, b):
    return lax.dot_general(a, b, (((0,), (0,)), ((), ())), preferred_element_type=F32)


def _dot_split_lhs(m, x, parts):
    acc = None
    rem = x
    for i in range(parts):
        hi = _bf(rem)
        term = _dot(m, hi)
        acc = term if acc is None else acc + term
        if i + 1 < parts:
            rem = rem - hi.astype(F32)
    return acc


def _iota2(shape, dim):
    return lax.broadcasted_iota(jnp.int32, shape, dim)


def _div2(x, n):
    assert n & (n - 1) == 0
    return lax.shift_right_logical(x, n.bit_length() - 1)


def _mod2(x, n):
    assert n & (n - 1) == 0
    return lax.bitwise_and(x, n - 1)


def _head_ones():
    ri = _iota2((LANES, LANES), 0)
    ci = _iota2((LANES, LANES), 1)
    return jnp.where(_div2(ri, HEAD_DIM) == _div2(ci, HEAD_DIM), 1.0, 0.0).astype(BF16)


def _headsum(x, ones_bd):
    xb = _bf(x)
    return jnp.concatenate([_dot(xb[:, LANES * j:LANES * (j + 1)], ones_bd) for j in range(PAIRS)], axis=1)


def _head_norm(o, g, b, eps, ones_bd):
    mu = _headsum(o, ones_bd) * (1.0 / HEAD_DIM)
    d = o - mu
    var = _headsum(d * d, ones_bd) * (1.0 / HEAD_DIM)
    return d * lax.rsqrt(var + eps) * g + b


def _layer_norm(z, g, b):
    mu = jnp.mean(z, axis=-1, keepdims=True)
    d = z - mu
    var = jnp.mean(d * d, axis=-1, keepdims=True)
    return d * lax.rsqrt(var + LN_EPS) * g + b


def _bdiag(x, lane_lo):
    zero = jnp.zeros_like(x)
    return jnp.concatenate([jnp.where(lane_lo, x, zero), jnp.where(lane_lo, zero, x)], axis=0)


def _unstack_heads(x, lane_lo):
    return jnp.where(lane_lo, x[0:ROWS], x[ROWS:])


def _stream_masks(ct, width):
    ri = _iota2((N2, width), 0)
    ci = _mod2(_iota2((N2, width), 1), N2)
    same = _div2(ri, ct) == _div2(ci, ct)
    return same & (ci < ri), same & (ci <= ri), ri == ci, _div2(ri, ROWS) == _div2(ci, ROWS)


def _load_state(s_scr, s0_ref, nb):
    z = jnp.zeros((HEAD_DIM, HEAD_DIM), F32)
    for s in range(nb):
        for j in range(PAIRS):
            top = jnp.concatenate([s0_ref[s, 2 * j], z], axis=1)
            bot = jnp.concatenate([z, s0_ref[s, 2 * j + 1]], axis=1)
            s_scr[s, j] = jnp.concatenate([top, bot], axis=0)


def _store_state(sout_ref, s_scr, nb):
    for s in range(nb):
        for j in range(PAIRS):
            s2 = s_scr[s, j]
            sout_ref[s, 2 * j] = s2[0:HEAD_DIM, 0:HEAD_DIM]
            sout_ref[s, 2 * j + 1] = s2[HEAD_DIM:, HEAD_DIM:]


def _seq_rows(x, q, ct):
    if ct == ROWS:
        return x
    return jnp.concatenate([x[q * ct:(q + 1) * ct], x[ROWS + q * ct:ROWS + (q + 1) * ct]], axis=0)


def _from_seq_rows(pieces, ct):
    if ct == ROWS:
        return pieces[0]
    return jnp.concatenate([p[0:ct] for p in pieces] + [p[ct:2 * ct] for p in pieces], axis=0)


def _pad_rows(x, rows):
    if x.shape[0] == rows:
        return x
    return jnp.concatenate([x, jnp.zeros((rows - x.shape[0], x.shape[1]), x.dtype)], axis=0)


def _emit_by_level(steps, envs):
    level = {}
    for name, deps, _ in steps:
        level[name] = 1 + max([level.get(d, 0) for d in deps], default=0)
    for lv in sorted(set(level.values())):
        for name, deps, fn in steps:
            if level[name] == lv:
                for env in envs:
                    env[name] = fn(*[env[d] for d in deps])


def _inverse_steps(nfac, eye):
    def advance(last):
        def fn(q, p):
            pb = _bf(p)
            if last:
                return p + _dot(q, pb), None
            x = _dot(q, jnp.concatenate([pb, q], axis=1))
            return p + x[:, 0:N2], _bf(x[:, N2:])
        return fn

    steps = [
        ("q0", ("l",), _bf),
        ("p1", ("l",), lambda l: l + eye()),
        ("q1", ("q0",), lambda q: _bf(_dot(q, q))),
    ]
    for i in range(1, nfac):
        steps.append((f"s{i}", (f"q{i}", f"p{i}"), advance(i == nfac - 1)))
        steps.append((f"p{i + 1}", (f"s{i}",), lambda s: s[0]))
        steps.append((f"q{i + 1}", (f"s{i}",), lambda s: s[1]))
    steps.append(("tb", (f"p{nfac}",), _bf))
    return steps


def _rwkv_score_steps(sc_mask):
    return [
        ("scm", ("a2", "r2", "bk"),
         lambda a2, r2, bk: _dot_tb(jnp.concatenate([a2, r2], axis=0), bk) * sc_mask()),
        ("l", ("scm",), lambda scm: scm[0:N2, 0:N2]),
        ("lkb", ("scm",), lambda scm: _bf(scm[:, N2:])),
        ("arb", ("scm",), lambda scm: _bf(scm[N2:, 0:N2])),
        ("lvark", ("lkb", "vst"), _dot),
    ]


def _rwkv_carried_steps(same_head):
    def f_ro(arb, tal, r2, lvark):
        x = _dot(arb, _bf(tal))
        return _bf(r2.astype(F32) + x[:, 0:N2]), x[:, N2:] + lvark[N2:]

    def f_mn(tal, b2, k2, vst):
        mm = _dot_ta(_bf(tal), b2)
        n2 = _dot_ta(vst, k2)
        return _bf(mm[0:N2]), (mm[N2:] + n2) * same_head()

    return [
        ("tal", ("tb", "a2", "lvark"),
         lambda tb, a2, lvark: _dot(tb, jnp.concatenate([a2, _bf(lvark[0:N2])], axis=1))),
        ("ro", ("arb", "tal", "r2", "lvark"), f_ro),
        ("mn", ("tal", "b2", "k2", "vst"), f_mn),
    ]


def _side_casts(side, nsteps, step_of):
    in_specs, out_specs, out_shapes = [], [], []
    for a, layer in side:
        rows, cols = a.shape[-2:]
        slab = rows // nsteps
        assert rows % nsteps == 0 and slab % (2 * SUBLANES) == 0
        if layer is None:
            in_specs.append(pl.BlockSpec((slab, cols), lambda *g: (step_of(*g), 0)))
        else:
            in_specs.append(pl.BlockSpec((None, slab, cols), lambda *g, layer=layer: (layer, step_of(*g), 0)))
        out_specs.append(pl.BlockSpec((slab, cols), lambda *g: (step_of(*g), 0)))
        out_shapes.append(jax.ShapeDtypeStruct((rows, cols), BF16))
    return in_specs, out_specs, out_shapes, [a for a, _ in side]


def _with_side_casts(body, n_in, n_out, nside):
    def kernel(*refs):
        ins, rest = refs[:n_in], refs[n_in:]
        side_in, rest = rest[:nside], rest[nside:]
        outs, rest = rest[:n_out], rest[n_out:]
        side_out, scratch = rest[:nside], rest[nside:]
        for src, dst in zip(side_in, side_out):
            dst[...] = _bf(src[...])
        body(*ins, *outs, *scratch)
    return kernel


def _inproj_kernel(x_ref, w_ref, pa_ref, pb_ref):
    xb = _bf(x_ref[...])
    pa_ref[...] = _dot(xb, w_ref[:, 0:A_PROJ])
    pb_ref[...] = _dot(xb, w_ref[:, A_PROJ:])


def _inproj(x2d, w, tm):
    m = x2d.shape[0]
    return pl.pallas_call(
        _inproj_kernel,
        grid=(m // tm,),
        in_specs=[
            pl.BlockSpec((tm, D_MODEL), lambda i: (i, 0)),
            pl.BlockSpec(w.shape, lambda i: (0, 0), pipeline_mode=pl.Buffered(1)),
        ],
        out_specs=[
            pl.BlockSpec((tm, A_PROJ), lambda i: (i, 0)),
            pl.BlockSpec((tm, B_PROJ), lambda i: (i, 0)),
        ],
        out_shape=[
            jax.ShapeDtypeStruct((m, A_PROJ), F32),
            jax.ShapeDtypeStruct((m, B_PROJ), F32),
        ],
        compiler_params=pltpu.CompilerParams(
            dimension_semantics=("arbitrary",), vmem_limit_bytes=VMEM_LIMIT),
        name="inproj_ab",
    )(x2d, w)


def _rwkv_kernel(nb, tt, ct, pa_ref, shift_ref, s0_ref, mu_ref, w0_ref, w2p_ref, a0_ref, a2p_ref,
                 g2_ref, kk_ref, ka_ref, rk_ref, lng_ref, lnb_ref,
                 ya_ref, sout_ref,
                 s_scr, prev_scr, at_scr, rt_scr, bt_scr, kt_scr, v_scr, p_scr, o_scr,
                 scmask_scr, eye_scr, head_scr, lcum_scr):
    rows = nb * tt
    nchunk = rows // ROWS
    ns = ROWS // ct
    ti = pl.program_id(1)

    @pl.when((pl.program_id(0) == 0) & (ti == 0))
    def _():
        strict, incl, eye, same_head = _stream_masks(ct, 2 * N2)
        one = lambda m: jnp.where(m, 1.0, 0.0)
        scmask_scr[...] = jnp.concatenate([one(strict), one(incl)], axis=0)
        eye_scr[...] = one(eye)[:, 0:N2]
        head_scr[...] = one(same_head)[:, 0:N2]
        ri = _iota2(lcum_scr.shape, 0)
        ci = _iota2(lcum_scr.shape, 1)
        lcum_scr[...] = jnp.where((_div2(ri, ct) == _div2(ci, ct)) & (ci <= ri), 1.0, 0.0).astype(BF16)

    @pl.when(ti == 0)
    def _():
        _load_state(s_scr, s0_ref, nb)
        prev_scr[...] = shift_ref[...]

    ones_bd = _head_ones()

    p3 = pa_ref[...]
    p2 = p3.reshape(rows, A_PROJ)
    t3 = _iota2((nb, tt, A_PROJ), 1)
    rolled = pltpu.roll(p2, 1, axis=0).reshape(nb, tt, A_PROJ)
    pprev = jnp.where(t3 == 0, prev_scr[...], rolled).reshape(rows, A_PROJ)
    prev_scr[...] = p3[:, tt - 1:tt, :]
    m = p2 + (pprev - p2) * mu_ref[...]
    r = m[:, 0:WIDTH]
    k = m[:, WIDTH:2 * WIDTH]
    v = m[:, 2 * WIDTH:3 * WIDTH]
    wa = m[:, 3 * WIDTH:3 * WIDTH + LANES]
    gd = m[:, 3 * WIDTH + LANES:A_PROJ]
    z = -(w0_ref[...] + _dot(_bf(jnp.tanh(wa)), w2p_ref[...]))
    softplus = jnp.maximum(z, 0.0) + jnp.log(1.0 + jnp.exp(-jnp.abs(z)))
    lw = -jnp.exp(-softplus - 0.5)
    a = jax.nn.sigmoid(a0_ref[...] + _dot(_bf(wa), a2p_ref[...]))
    g = _dot(_bf(jax.nn.sigmoid(gd)), g2_ref[...])
    kk = k * kk_ref[...]
    k = k * (1.0 + (a - 1.0) * ka_ref[...])
    kk = kk * lax.rsqrt(jnp.maximum(_headsum(kk * kk, ones_bd), 1e-24))
    bonus = _headsum(r * k * rk_ref[...], ones_bd) * v

    span = lcum_scr.shape[0]
    cs = jnp.concatenate([_dot_split_lhs(lcum_scr[...], lw[i * span:(i + 1) * span], 2)
                          for i in range(rows // span)], axis=0)
    pinv = jnp.exp(-cs)
    p_scr[...] = jnp.exp(cs)
    at_scr[...] = _bf(-kk * jnp.exp(cs - lw))
    rt_scr[...] = _bf(r * p_scr[...])
    bt_scr[...] = _bf(kk * a * pinv)
    kt_scr[...] = _bf(k * pinv)
    v_scr[...] = _bf(v)

    lane_lo = _iota2((ROWS, LANES), 1) < HEAD_DIM
    sc_mask = lambda: scmask_scr[...]
    eye = lambda: eye_scr[...]
    same_head = lambda: head_scr[...]
    nfac = ct.bit_length() - 1

    envs = []
    for c in range(nchunk):
        for j in range(PAIRS):
            rsl = slice(c * ROWS, (c + 1) * ROWS)
            lsl = slice(LANES * j, LANES * (j + 1))
            aj, rj, bj, kj, vj = (s[rsl, lsl] for s in (at_scr, rt_scr, bt_scr, kt_scr, v_scr))
            envs.append({
                "c": c, "j": j, "rsl": rsl, "lsl": lsl,
                "a2": _bdiag(aj, lane_lo), "r2": _bdiag(rj, lane_lo),
                "b2": _bdiag(bj, lane_lo), "k2": _bdiag(kj, lane_lo),
                "bk": jnp.concatenate([bj, bj, kj, kj], axis=0),
                "vst": jnp.concatenate([vj, vj], axis=0),
            })

    steps = _rwkv_score_steps(sc_mask) + _inverse_steps(nfac, eye)
    if ns == 1:
        _emit_by_level(steps + _rwkv_carried_steps(same_head), envs)
        per_seq = tt // ROWS
        state = [[s_scr[s, j] for j in range(PAIRS)] for s in range(nb)]
        for cpos in range(per_seq):
            for s in range(nb):
                c = s * per_seq + cpos
                cenv = envs[c * PAIRS:(c + 1) * PAIRS]
                sb = [_bf(x) for x in state[s]]
                for j, env in enumerate(cenv):
                    rp, oc = env["ro"]
                    o_scr[env["rsl"], env["lsl"]] = _unstack_heads(_dot_tb(rp, sb[j]) + oc, lane_lo)
                for j, env in enumerate(cenv):
                    mk, nn = env["mn"]
                    pc = p_scr[(c + 1) * ROWS - 1:(c + 1) * ROWS, env["lsl"]]
                    state[s][j] = (state[s][j] + _dot(sb[j], mk) + nn) * pc
        for s in range(nb):
            for j in range(PAIRS):
                s_scr[s, j] = state[s][j]
    else:
        for env in envs:
            env["sb"] = [_bf(s_scr[env["c"] * ns + q, env["j"]]) for q in range(ns)]

        def f_h(a2, r2, sb):
            pieces = [_dot_tb(jnp.concatenate([_seq_rows(a2, q, ct), _seq_rows(r2, q, ct)], axis=0), sb[q])
                      for q in range(ns)]
            return (_from_seq_rows([x[0:2 * ct] for x in pieces], ct),
                    _from_seq_rows([x[2 * ct:] for x in pieces], ct))

        def f_ds(u2, vst, b2, k2):
            vst32 = vst.astype(F32)
            out = []
            for q in range(ns):
                uv = jnp.concatenate([_seq_rows(u2, q, ct), _seq_rows(vst32, q, ct)], axis=0)
                bk = jnp.concatenate([_seq_rows(b2, q, ct), _seq_rows(k2, q, ct)], axis=0)
                out.append(_dot_ta(_bf(_pad_rows(uv, LANES)), _pad_rows(bk, LANES)) * same_head())
            return out

        steps += [
            ("h", ("a2", "r2", "sb"), f_h),
            ("u2", ("tb", "h", "lvark"), lambda tb, h, lvark: _dot(tb, _bf(h[0] + lvark[0:N2]))),
            ("o2", ("arb", "u2", "h", "lvark"),
             lambda arb, u2, h, lvark: h[1] + _dot(arb, _bf(u2)) + lvark[N2:]),
            ("ds", ("u2", "vst", "b2", "k2"), f_ds),
        ]
        _emit_by_level(steps, envs)
        for env in envs:
            o_scr[env["rsl"], env["lsl"]] = _unstack_heads(env["o2"], lane_lo)
            for q in range(ns):
                seq = env["c"] * ns + q
                pc = p_scr[(seq + 1) * ct - 1:(seq + 1) * ct, env["lsl"]]
                s_scr[seq, env["j"]] = (s_scr[seq, env["j"]] + env["ds"][q]) * pc

    o = o_scr[...]
    y = (_head_norm(o, lng_ref[...], lnb_ref[...], A_GN_EPS, ones_bd) + bonus) * g
    ya_ref[...] = _bf(y).reshape(nb, tt, WIDTH)

    @pl.when(ti == pl.num_programs(1) - 1)
    def _():
        _store_state(sout_ref, s_scr, nb)


def _row_spec(shape):
    return pl.BlockSpec(shape, lambda b, t: (0,) * len(shape))


def _check_mixer_blocking(bsz, tlen, nb, tt):
    ct = min(ROWS, tt)
    rows = nb * tt
    assert rows % ROWS == 0 and ROWS % ct == 0 and bsz % nb == 0 and tlen % tt == 0
    assert tt % ROWS == 0 or tt == tlen
    return ct, rows


def _rwkv(pa3, shift0, s0, prm, nb, tt, side=()):
    bsz, tlen, _ = pa3.shape
    ct, rows = _check_mixer_blocking(bsz, tlen, nb, tt)
    span = tt if tt % ROWS == 0 else rows
    grid = (bsz // nb, tlen // tt)
    s_in, s_out, s_shapes, s_args = _side_casts(side, grid[0] * grid[1], lambda b, t: b * grid[1] + t)
    state_spec = pl.BlockSpec((nb, HEADS, HEAD_DIM, HEAD_DIM), lambda b, t: (b, 0, 0, 0))
    vec = lambda n: _row_spec((1, n))
    in_specs = [
        pl.BlockSpec((nb, tt, A_PROJ), lambda b, t: (b, t, 0)),
        pl.BlockSpec((nb, 1, A_PROJ), lambda b, t: (b, 0, 0)),
        state_spec,
        vec(A_PROJ), vec(WIDTH), _row_spec((LANES, WIDTH)), vec(WIDTH), _row_spec((LANES, WIDTH)),
        _row_spec((GATE_LORA, WIDTH)), vec(WIDTH), vec(WIDTH), vec(WIDTH), vec(WIDTH), vec(WIDTH),
    ]
    out_specs = [
        pl.BlockSpec((nb, tt, WIDTH), lambda b, t: (b, t, 0)),
        state_spec,
    ]
    scratch = [
        pltpu.VMEM((nb, PAIRS, LANES, LANES), F32),
        pltpu.VMEM((nb, 1, A_PROJ), F32),
        pltpu.VMEM((rows, WIDTH), BF16),
        pltpu.VMEM((rows, WIDTH), BF16),
        pltpu.VMEM((rows, WIDTH), BF16),
        pltpu.VMEM((rows, WIDTH), BF16),
        pltpu.VMEM((rows, WIDTH), BF16),
        pltpu.VMEM((rows, WIDTH), F32),
        pltpu.VMEM((rows, WIDTH), F32),
        pltpu.VMEM((2 * N2, 2 * N2), F32),
        pltpu.VMEM((N2, N2), F32),
        pltpu.VMEM((N2, N2), F32),
        pltpu.VMEM((span, span), BF16),
    ]
    return pl.pallas_call(
        _with_side_casts(functools.partial(_rwkv_kernel, nb, tt, ct), len(in_specs), len(out_specs), len(side)),
        grid=grid,
        in_specs=in_specs + s_in,
        out_specs=out_specs + s_out,
        out_shape=[
            jax.ShapeDtypeStruct((bsz, tlen, WIDTH), BF16),
            jax.ShapeDtypeStruct(s0.shape, F32),
        ] + s_shapes,
        scratch_shapes=scratch,
        compiler_params=pltpu.CompilerParams(
            dimension_semantics=("arbitrary", "arbitrary"), vmem_limit_bytes=VMEM_LIMIT),
        name="rwkv7_mixer",
    )(pa3, shift0.reshape(bsz, 1, A_PROJ), s0, *prm, *s_args)


def _ret_kernel(nb, tt, ct, pos0, pb_ref, s0_ref, invf_ref, gng_ref, gnb_ref,
                yb_ref, sout_ref,
                s_scr, q_scr, k_scr, v_scr, o_scr, dec_scr, head_scr, trig_scr):
    rows = nb * tt
    nchunk = rows // ROWS
    ns = ROWS // ct
    ti = pl.program_id(1)

    @pl.when((pl.program_id(0) == 0) & (ti == 0))
    def _():
        _, incl, _, same_head = _stream_masks(ct, N2)
        head_scr[...] = jnp.where(same_head, 1.0, 0.0)
        ri = _iota2((N2, N2), 0)
        ci = _iota2((N2, N2), 1)
        tpos = _mod2(ri, ct).astype(F32)
        tdiff = (_mod2(ri, ct) - _mod2(ci, ct)).astype(F32)
        for j in range(PAIRS):
            head = (2 * j + _div2(ri, ROWS)).astype(F32)
            lg = jnp.log1p(-jnp.exp2(-5.0 - head))
            dec_scr[j, 0] = jnp.where(incl, jnp.exp(lg * jnp.maximum(tdiff, 0.0)), 0.0)
            dec_scr[j, 1] = jnp.exp(lg * (tpos + 1.0))
            dec_scr[j, 2] = jnp.exp(lg * (ct - 1.0 - tpos))
            dec_scr[j, 3] = jnp.exp(lg * ct)
        off = _mod2(_iota2((rows, LANES), 0), tt).astype(F32) * invf_ref[...]
        trig_scr[0] = jnp.cos(off)
        trig_scr[1] = jnp.sin(off)

    @pl.when(ti == 0)
    def _():
        _load_state(s_scr, s0_ref, nb)

    ones_bd = _head_ones()
    pb = pb_ref[...].reshape(rows, B_PROJ)
    q = pb[:, 0:WIDTH]
    k = pb[:, WIDTH:2 * WIDTH]
    gate = pb[:, 3 * WIDTH:]

    base = jnp.zeros((SUBLANES, LANES), F32) + (pos0 + ti * tt).astype(F32)
    ang = base * invf_ref[...]
    cos_a, sin_a = jnp.cos(ang)[0:1], jnp.sin(ang)[0:1]
    cos_b, sin_b = trig_scr[0], trig_scr[1]
    cos = jnp.concatenate([cos_a * cos_b - sin_a * sin_b] * PAIRS, axis=1)
    sin = jnp.concatenate([sin_a * cos_b + cos_a * sin_b] * PAIRS, axis=1)
    first_half = _mod2(_iota2((rows, WIDTH), 1), HEAD_DIM) < (HEAD_DIM // 2)
    sin = jnp.where(first_half, -sin, sin)

    pr = _iota2((LANES, LANES), 0)
    pc = _iota2((LANES, LANES), 1)
    swap = jnp.where(pr == lax.bitwise_xor(pc, HEAD_DIM // 2), 1.0, 0.0).astype(BF16)

    def rope(x):
        xb = _bf(x)
        partner = jnp.concatenate([_dot(xb[:, LANES * j:LANES * (j + 1)], swap) for j in range(PAIRS)], axis=1)
        return x * cos + partner * sin

    q_scr[...] = _bf(rope(q))
    k_scr[...] = rope(k) * (HEAD_DIM ** -0.5)
    v_scr[...] = _bf(pb[:, 2 * WIDTH:3 * WIDTH])

    lane_lo = _iota2((ROWS, LANES), 1) < HEAD_DIM
    same_head = lambda: head_scr[...]

    envs = []
    for c in range(nchunk):
        for j in range(PAIRS):
            rsl = slice(c * ROWS, (c + 1) * ROWS)
            lsl = slice(LANES * j, LANES * (j + 1))
            qj, kj, vj = q_scr[rsl, lsl], k_scr[rsl, lsl], v_scr[rsl, lsl]
            kjb = _bf(kj)
            envs.append({
                "c": c, "j": j, "rsl": rsl, "lsl": lsl,
                "q2": _bdiag(qj, lane_lo),
                "kst": jnp.concatenate([kjb, kjb], axis=0),
                "k2d": _bdiag(kj, lane_lo) * dec_scr[j, 2],
                "vst": jnp.concatenate([vj, vj], axis=0),
                "dmask": lambda j=j: dec_scr[j, 0],
            })

    def f_ds(k2d, vst):
        return [_dot_ta(_bf(_pad_rows(_seq_rows(k2d, s, ct), LANES)), _pad_rows(_seq_rows(vst, s, ct), LANES))
                * same_head() for s in range(ns)]

    steps = [
        ("sc", ("q2", "kst", "dmask"), lambda q2, kst, dmask: _bf(_dot_tb(q2, kst) * dmask())),
        ("inner", ("sc", "vst"), _dot),
        ("ds", ("k2d", "vst"), f_ds),
    ]
    _emit_by_level(steps, envs)

    if ns == 1:
        per_seq = tt // ROWS
        for s in range(nb):
            state = [s_scr[s, j] for j in range(PAIRS)]
            for c in range(s * per_seq, (s + 1) * per_seq):
                for j in range(PAIRS):
                    env = envs[c * PAIRS + j]
                    env["sb"] = [_bf(state[j])]
                    state[j] = state[j] * dec_scr[j, 3] + env["ds"][0]
            for j in range(PAIRS):
                s_scr[s, j] = state[j]
    else:
        for env in envs:
            j = env["j"]
            first = env["c"] * ns
            env["sb"] = [_bf(s_scr[first + s, j]) for s in range(ns)]
            for s in range(ns):
                s_scr[first + s, j] = s_scr[first + s, j] * dec_scr[j, 3] + env["ds"][s]

    def f_cross(q2, sb):
        return _from_seq_rows([_dot(_seq_rows(q2, s, ct), sb[s]) for s in range(ns)], ct)

    _emit_by_level([("cross", ("q2", "sb"), f_cross)], envs)
    for env in envs:
        o2 = env["inner"] + env["cross"] * dec_scr[env["j"], 1]
        o_scr[env["rsl"], env["lsl"]] = _unstack_heads(o2, lane_lo)

    o = o_scr[...]
    y = jax.nn.silu(gate) * _head_norm(o, gng_ref[...], gnb_ref[...], B_GN_EPS, ones_bd)
    yb_ref[...] = _bf(y).reshape(nb, tt, WIDTH)

    @pl.when(ti == pl.num_programs(1) - 1)
    def _():
        _store_state(sout_ref, s_scr, nb)


def _retention(pb3, s0, invf, gn_g, gn_b, nb, tt, pos0, side=()):
    bsz, tlen, _ = pb3.shape
    ct, rows = _check_mixer_blocking(bsz, tlen, nb, tt)
    grid = (bsz // nb, tlen // tt)
    s_in, s_out, s_shapes, s_args = _side_casts(side, grid[0] * grid[1], lambda b, t: b * grid[1] + t)
    state_spec = pl.BlockSpec((nb, HEADS, HEAD_DIM, HEAD_DIM), lambda b, t: (b, 0, 0, 0))
    in_specs = [
        pl.BlockSpec((nb, tt, B_PROJ), lambda b, t: (b, t, 0)),
        state_spec,
        _row_spec((1, LANES)), _row_spec((1, WIDTH)), _row_spec((1, WIDTH)),
    ]
    out_specs = [
        pl.BlockSpec((nb, tt, WIDTH), lambda b, t: (b, t, 0)),
        state_spec,
    ]
    scratch = [
        pltpu.VMEM((nb, PAIRS, LANES, LANES), F32),
        pltpu.VMEM((rows, WIDTH), BF16),
        pltpu.VMEM((rows, WIDTH), F32),
        pltpu.VMEM((rows, WIDTH), BF16),
        pltpu.VMEM((rows, WIDTH), F32),
        pltpu.VMEM((PAIRS, 4, N2, N2), F32),
        pltpu.VMEM((N2, N2), F32),
        pltpu.VMEM((2, rows, LANES), F32),
    ]
    return pl.pallas_call(
        _with_side_casts(functools.partial(_ret_kernel, nb, tt, ct, pos0), len(in_specs), len(out_specs),
                         len(side)),
        grid=grid,
        in_specs=in_specs + s_in,
        out_specs=out_specs + s_out,
        out_shape=[
            jax.ShapeDtypeStruct((bsz, tlen, WIDTH), BF16),
            jax.ShapeDtypeStruct(s0.shape, F32),
        ] + s_shapes,
        scratch_shapes=scratch,
        compiler_params=pltpu.CompilerParams(
            dimension_semantics=("arbitrary", "arbitrary"), vmem_limit_bytes=VMEM_LIMIT),
        name="retention_mixer",
    )(pb3, s0, invf, gn_g, gn_b, *s_args)


CONV_COLS = 256


def _conv_kernel(nb, tt, nsplit, x_ref, buf_ref, win_ref, cw_ref, z_ref, bout_ref, halo_scr):
    rows = nb * tt
    sub = rows // nsplit
    st = tt // nsplit if nb == 1 else tt
    ti = pl.program_id(1)

    @pl.when(ti == 0)
    def _():
        halo_scr[...] = buf_ref[...]

    t3 = _iota2((nb, st, CONV_COLS), 1)

    def project(xb, lo):
        col = lambda k: win_ref[:, k * D_MODEL + lo:k * D_MODEL + lo + CONV_COLS]
        return _dot(xb, col(0)), _dot(xb, col(1)) * _dot(xb, col(2))

    def mix(i, lo, h0, h1, bg, u):
        cw = cw_ref[:, lo:lo + CONV_COLS]
        u3 = u.reshape(nb, st, CONV_COLS)
        prev1 = jnp.where(t3 == 0, h1, pltpu.roll(u, 1, axis=0).reshape(nb, st, CONV_COLS))
        prev2 = jnp.where(t3 == 0, h0,
                          jnp.where(t3 == 1, h1, pltpu.roll(u, 2, axis=0).reshape(nb, st, CONV_COLS)))
        conv = prev2 * cw[0:1, :] + prev1 * cw[1:2, :] + u3 * cw[2:3, :]
        z_ref[i * sub:(i + 1) * sub, lo:lo + CONV_COLS] = _bf(bg * conv.reshape(sub, CONV_COLS))

    halo = {lo: (halo_scr[:, 0:1, lo:lo + CONV_COLS], halo_scr[:, 1:2, lo:lo + CONV_COLS])
            for lo in range(0, D_MODEL, CONV_COLS)}
    pending = None
    for i in range(nsplit):
        xi = x_ref[0, i * sub:(i + 1) * sub, :] if nb == 1 else x_ref[...].reshape(rows, D_MODEL)
        xb = _bf(xi)
        for lo in range(0, D_MODEL, CONV_COLS):
            bg, u = project(xb, lo)
            if pending is not None:
                mix(*pending)
            pending = (i, lo, *halo[lo], bg, u)
            u3 = u.reshape(nb, st, CONV_COLS)
            halo[lo] = (u3[:, st - 2:st - 1, :], u3[:, st - 1:st, :])
    mix(*pending)
    for lo, (h0, h1) in halo.items():
        halo_scr[:, 0:1, lo:lo + CONV_COLS] = h0
        halo_scr[:, 1:2, lo:lo + CONV_COLS] = h1

    @pl.when(ti == pl.num_programs(1) - 1)
    def _():
        bout_ref[...] = halo_scr[...]


def _conv_mixer(x3, buf0, win, cw, nb, tt, nsplit):
    bsz, tlen, _ = x3.shape
    rows = nb * tt
    assert bsz % nb == 0 and tlen % tt == 0
    assert nsplit == 1 or nb == 1
    assert tt % nsplit == 0 and tt // nsplit >= CONV_W - 1
    grid = (bsz // nb, tlen // tt)
    buf_spec = pl.BlockSpec((nb, CONV_W - 1, D_MODEL), lambda b, t: (b, 0, 0))
    return pl.pallas_call(
        functools.partial(_conv_kernel, nb, tt, nsplit),
        grid=grid,
        in_specs=[
            pl.BlockSpec((nb, tt, D_MODEL), lambda b, t: (b, t, 0)),
            buf_spec,
            pl.BlockSpec(win.shape, lambda b, t: (0, 0), pipeline_mode=pl.Buffered(1)),
            _row_spec((CONV_W, D_MODEL)),
        ],
        out_specs=[
            pl.BlockSpec((rows, D_MODEL), lambda b, t: (b * (tlen // tt) + t, 0)),
            buf_spec,
        ],
        out_shape=[
            jax.ShapeDtypeStruct((bsz * tlen, D_MODEL), BF16),
            jax.ShapeDtypeStruct(buf0.shape, F32),
        ],
        scratch_shapes=[pltpu.VMEM((nb, CONV_W - 1, D_MODEL), F32)],
        compiler_params=pltpu.CompilerParams(
            dimension_semantics=("arbitrary", "arbitrary"), vmem_limit_bytes=VMEM_LIMIT),
        name="conv_mixer",
    )(x3, buf0, win, cw)


FF_CHUNK = 1024


def _post_kernel(npieces, nsplit, *refs):
    x_ref = refs[0]
    y_refs = refs[1:1 + npieces]
    wout_ref, g1_ref, b1_ref, wup_ref, wdown_ref, g2_ref, b2_ref, out_ref = refs[1 + npieces:]
    sub = x_ref.shape[0] // nsplit

    def f_y(r):
        y = None
        off = 0
        for y_ref in y_refs:
            n = y_ref.shape[1]
            term = _dot(y_ref[r, :], wout_ref[off:off + n, :])
            y = term if y is None else y + term
            off += n
        return y

    def f_mlp(f):
        def step(x1b, acc):
            h = jnp.maximum(_dot(x1b, wup_ref[:, f * FF_CHUNK:(f + 1) * FF_CHUNK]), 0.0)
            term = _dot(_bf(h * h), wdown_ref[f * FF_CHUNK:(f + 1) * FF_CHUNK, :])
            return term if acc is None else acc + term
        return step

    def f_out(r, x1, acc):
        out_ref[r, :] = _layer_norm(ALPHA * x1 + acc, g2_ref[...], b2_ref[...])
        return None

    steps = [
        ("y", ("r",), f_y),
        ("x1", ("r", "y"), lambda r, y: _layer_norm(ALPHA * x_ref[r, :] + y, g1_ref[...], b1_ref[...])),
        ("x1b", ("x1",), _bf),
        ("acc-1", ("x1b",), lambda x1b: None),
    ]
    nff = D_FF // FF_CHUNK
    for f in range(nff):
        steps.append((f"acc{f}", ("x1b", f"acc{f - 1}"), f_mlp(f)))
    steps.append(("out", ("r", "x1", f"acc{nff - 1}"), f_out))
    envs = [{"r": slice(i * sub, (i + 1) * sub)} for i in range(nsplit)]
    _emit_by_level(steps, envs)


def _post(x2d, ys, wout, g1, b1, wup, wdown, g2, b2, tm, nsplit):
    m = x2d.shape[0]
    assert m % tm == 0 and tm % nsplit == 0
    const = lambda i: (0, 0)
    resident = lambda a: pl.BlockSpec(a.shape, const, pipeline_mode=pl.Buffered(1))
    vec = pl.BlockSpec((1, D_MODEL), const)
    in_specs = [pl.BlockSpec((tm, D_MODEL), lambda i: (i, 0))]
    in_specs += [pl.BlockSpec((tm, y.shape[1]), lambda i: (i, 0)) for y in ys]
    in_specs += [resident(wout), vec, vec, resident(wup), resident(wdown), vec, vec]
    return pl.pallas_call(
        functools.partial(_post_kernel, len(ys), nsplit),
        grid=(m // tm,),
        in_specs=in_specs,
        out_specs=pl.BlockSpec((tm, D_MODEL), lambda i: (i, 0)),
        out_shape=jax.ShapeDtypeStruct((m, D_MODEL), F32),
        compiler_params=pltpu.CompilerParams(
            dimension_semantics=("arbitrary",), vmem_limit_bytes=VMEM_LIMIT),
        name="post_block",
    )(x2d, *ys, wout, g1, b1, wup, wdown, g2, b2)


def _trunk(x, st_shift, st_wkv, st_ret, st_conv, pos0, blk, w, dense):
    bsz, tlen, _ = x.shape
    m = bsz * tlen
    x2d = x.reshape(m, D_MODEL)
    row = lambda a: a.reshape(1, -1)
    cast = dense is None
    dense = {} if cast else dense
    side = lambda *items: items if cast else ()

    pa, pb = _inproj(x2d, w["w_in_ab"], blk["tm_in"])
    pa3 = pa.reshape(bsz, tlen, A_PROJ)
    rw_prm = (row(w["mu_a"]), row(w["w0"]), w["w2p"], row(w["a0"]), w["a2p"], w["g2"],
              row(w["k_k"]), row(w["k_a"]), row(w["r_k"]), row(w["lnx_g"]), row(w["lnx_b"]))
    ya, wkv1, *made = _rwkv(pa3, st_shift, st_wkv, rw_prm, blk["nb_mix"], blk["tt_mix"],
                            side((w["w_out_ab"], 0), (w["w_up"], 0), (w["w_down"], 0)))
    if cast:
        dense["w_out_ab"], dense["w_up0"], dense["w_down0"] = made
    yb, ret1, *made = _retention(pb.reshape(bsz, tlen, B_PROJ), st_ret, w["invf"], row(w["gn_g"]),
                                 row(w["gn_b"]), blk["nb_mix"], blk["tt_mix"], pos0,
                                 side((w["w_in_conv"], 0), (w["w_out_conv"], 0), (w["w_up"], 1), (w["w_down"], 1)))
    if cast:
        dense["w_in_conv"], dense["w_out_conv"], dense["w_up1"], dense["w_down1"] = made
    shift1 = pa3[:, tlen - 1, :]
    x2d = _post(x2d, [ya.reshape(m, WIDTH), yb.reshape(m, WIDTH)], dense["w_out_ab"],
                row(w["ln1_g"][0]), row(w["ln1_b"][0]), dense["w_up0"], dense["w_down0"],
                row(w["ln2_g"][0]), row(w["ln2_b"][0]), blk["tm_post"], blk["split"])

    z, conv1 = _conv_mixer(x2d.reshape(bsz, tlen, D_MODEL), st_conv, dense["w_in_conv"], w["conv_w"],
                           blk["nb_conv"], blk["tt_conv"], blk["split_conv"])
    x2d = _post(x2d, [z], dense["w_out_conv"], row(w["ln1_g"][1]), row(w["ln1_b"][1]),
                dense["w_up1"], dense["w_down1"], row(w["ln2_g"][1]), row(w["ln2_b"][1]),
                blk["tm_post"], blk["split"])
    outs = (x2d.reshape(bsz, tlen, D_MODEL), shift1[None], wkv1[None], ret1[None], conv1[None])
    return outs, dense


def kernel(x_prompt, x_sample, state_shift, state_wkv, state_ret, state_conv, w_in_ab, mu_a, w0, w2, a0, a2,
           g2, k_k, k_a, r_k, lnx_g, lnx_b, gn_g, gn_b, w_out_ab, w_in_conv, conv_w, w_out_conv,
           ln1_g, ln1_b, ln2_g, ln2_b, w_up, w_down):
    bp = x_prompt.shape[0]
    half = HEAD_DIM // 2
    inv = ROPE_BASE ** (-jnp.arange(half, dtype=F32) / half)
    zpad = jnp.zeros((LANES - DECAY_LORA, WIDTH), F32)
    w = {
        "w_in_ab": _bf(w_in_ab[0]),
        "mu_a": mu_a[0], "w0": w0[0], "a0": a0[0],
        "w2p": _bf(jnp.concatenate([w2[0], zpad], axis=0)),
        "a2p": _bf(jnp.concatenate([zpad, a2[0]], axis=0)),
        "g2": _bf(g2[0]), "k_k": k_k[0], "k_a": k_a[0], "r_k": r_k[0],
        "lnx_g": lnx_g[0], "lnx_b": lnx_b[0], "gn_g": gn_g[0], "gn_b": gn_b[0],
        "invf": jnp.tile(inv, LANES // half).reshape(1, LANES),
        "w_out_ab": w_out_ab, "w_in_conv": w_in_conv, "conv_w": conv_w[0], "w_out_conv": w_out_conv,
        "ln1_g": ln1_g, "ln1_b": ln1_b, "ln2_g": ln2_g, "ln2_b": ln2_b,
        "w_up": w_up, "w_down": w_down,
    }
    dt = state_wkv.dtype
    z_shift = jnp.zeros((bp, A_PROJ), dt)
    z_state = jnp.zeros((bp, HEADS, HEAD_DIM, HEAD_DIM), dt)
    z_conv = jnp.zeros((bp, CONV_W - 1, D_MODEL), dt)
    blk_p = {"tm_in": 1024, "tm_post": 1024, "split": 4, "nb_mix": 2, "tt_mix": 256,
             "nb_conv": 1, "tt_conv": 2048, "split_conv": 2}
    (y_p, p_shift, p_wkv, p_ret, p_conv), dense = _trunk(x_prompt, z_shift, z_state, z_state, z_conv, 0,
                                                         blk_p, w, None)
    ts = x_sample.shape[1]
    blk_s = {"tm_in": 512, "tm_post": 1024, "split": 4, "nb_mix": 2 * ROWS // ts, "tt_mix": ts,
             "nb_conv": 512 // ts, "tt_conv": ts, "split_conv": 1}
    (y_s, s_shift, s_wkv, s_ret, s_conv), _ = _trunk(x_sample, state_shift[0], state_wkv[0], state_ret[0],
                                                     state_conv[0], PAST_LEN, blk_s, w, dense)
    return (y_p, y_s, p_shift, p_wkv, p_ret, p_conv, s_shift, s_wkv, s_ret, s_conv)
```

```python
import functools

import jax
import jax.numpy as jnp
from jax import lax
from jax.experimental import pallas as pl
from jax.experimental.pallas import tpu as pltpu

F32 = jnp.float32
BF16 = jnp.bfloat16

D_MODEL = 1024
DEPTH = 2
PAST_LEN = 16384
HEADS = 8
HEAD_DIM = 64
WIDTH = HEADS * HEAD_DIM
DECAY_LORA = 64
AAA_LORA = 64
GATE_LORA = 128
A_PROJ = 3 * WIDTH + DECAY_LORA + AAA_LORA + GATE_LORA
B_PROJ = 4 * WIDTH
A_GN_EPS = 64e-5
B_GN_EPS = 1e-5
ROPE_BASE = 10000.0
CONV_W = 3
D_FF = 4 * D_MODEL
LN_EPS = 1e-5
ALPHA = (2.0 * DEPTH) ** 0.25

LANES = 128
SUBLANES = 8
PAIRS = WIDTH // LANES
ROWS = 64
N2 = 2 * ROWS
VMEM_LIMIT = 56 * 1024 * 1024


def _bf(x):
    return x.astype(BF16)


def _dot(a, b):
    return jnp.dot(a, b, preferred_element_type=F32)


def _dot_tb(a, b):
    return lax.dot_general(a, b, (((1,), (1,)), ((), ())), preferred_element_type=F32)


def _dot_ta(a, b):
    return lax.dot_general(a, b, (((0,), (0,)), ((), ())), preferred_element_type=F32)


def _dot_split_lhs(m, x, parts):
    acc = None
    rem = x
    for i in range(parts):
        hi = _bf(rem)
        term = _dot(m, hi)
        acc = term if acc is None else acc + term
        if i + 1 < parts:
            rem = rem - hi.astype(F32)
    return acc


def _iota2(shape, dim):
    return lax.broadcasted_iota(jnp.int32, shape, dim)


def _div2(x, n):
    assert n & (n - 1) == 0
    return lax.shift_right_logical(x, n.bit_length() - 1)


def _mod2(x, n):
    assert n & (n - 1) == 0
    return lax.bitwise_and(x, n - 1)


def _head_ones():
    ri = _iota2((LANES, LANES), 0)
    ci = _iota2((LANES, LANES), 1)
    return jnp.where(_div2(ri, HEAD_DIM) == _div2(ci, HEAD_DIM), 1.0, 0.0).astype(BF16)


def _headsum(x, ones_bd):
    xb = _bf(x)
    return jnp.concatenate([_dot(xb[:, LANES * j:LANES * (j + 1)], ones_bd) for j in range(PAIRS)], axis=1)


def _head_norm(o, g, b, eps, ones_bd):
    mu = _headsum(o, ones_bd) * (1.0 / HEAD_DIM)
    d = o - mu
    var = _headsum(d * d, ones_bd) * (1.0 / HEAD_DIM)
    return d * lax.rsqrt(var + eps) * g + b


def _layer_norm(z, g, b):
    mu = jnp.mean(z, axis=-1, keepdims=True)
    d = z - mu
    var = jnp.mean(d * d, axis=-1, keepdims=True)
    return d * lax.rsqrt(var + LN_EPS) * g + b


def _bdiag(x, lane_lo):
    zero = jnp.zeros_like(x)
    return jnp.concatenate([jnp.where(lane_lo, x, zero), jnp.where(lane_lo, zero, x)], axis=0)


def _unstack_heads(x, lane_lo):
    return jnp.where(lane_lo, x[0:ROWS], x[ROWS:])


def _stream_masks(ct, width):
    ri = _iota2((N2, width), 0)
    ci = _mod2(_iota2((N2, width), 1), N2)
    same = _div2(ri, ct) == _div2(ci, ct)
    return same & (ci < ri), same & (ci <= ri), ri == ci, _div2(ri, ROWS) == _div2(ci, ROWS)


def _load_state(s_scr, s0_ref, nb):
    z = jnp.zeros((HEAD_DIM, HEAD_DIM), F32)
    for s in range(nb):
        for j in range(PAIRS):
            top = jnp.concatenate([s0_ref[s, 2 * j], z], axis=1)
            bot = jnp.concatenate([z, s0_ref[s, 2 * j + 1]], axis=1)
            s_scr[s, j] = jnp.concatenate([top, bot], axis=0)


def _store_state(sout_ref, s_scr, nb):
    for s in range(nb):
        for j in range(PAIRS):
            s2 = s_scr[s, j]
            sout_ref[s, 2 * j] = s2[0:HEAD_DIM, 0:HEAD_DIM]
            sout_ref[s, 2 * j + 1] = s2[HEAD_DIM:, HEAD_DIM:]


def _seq_rows(x, q, ct):
    if ct == ROWS:
        return x
    return jnp.concatenate([x[q * ct:(q + 1) * ct], x[ROWS + q * ct:ROWS + (q + 1) * ct]], axis=0)


def _from_seq_rows(pieces, ct):
    if ct == ROWS:
        return pieces[0]
    return jnp.concatenate([p[0:ct] for p in pieces] + [p[ct:2 * ct] for p in pieces], axis=0)


def _pad_rows(x, rows):
    if x.shape[0] == rows:
        return x
    return jnp.concatenate([x, jnp.zeros((rows - x.shape[0], x.shape[1]), x.dtype)], axis=0)


def _emit_by_level(steps, envs):
    level = {}
    for name, deps, _ in steps:
        level[name] = 1 + max([level.get(d, 0) for d in deps], default=0)
    for lv in sorted(set(level.values())):
        for name, deps, fn in steps:
            if level[name] == lv:
                for env in envs:
                    env[name] = fn(*[env[d] for d in deps])


def _inverse_steps(nfac, eye):
    def advance(last):
        def fn(q, p):
            pb = _bf(p)
            if last:
                return p + _dot(q, pb), None
            x = _dot(q, jnp.concatenate([pb, q], axis=1))
            return p + x[:, 0:N2], _bf(x[:, N2:])
        return fn

    steps = [
        ("q0", ("l",), _bf),
        ("p1", ("l",), lambda l: l + eye()),
        ("q1", ("q0",), lambda q: _bf(_dot(q, q))),
    ]
    for i in range(1, nfac):
        steps.append((f"s{i}", (f"q{i}", f"p{i}"), advance(i == nfac - 1)))
        steps.append((f"p{i + 1}", (f"s{i}",), lambda s: s[0]))
        steps.append((f"q{i + 1}", (f"s{i}",), lambda s: s[1]))
    steps.append(("tb", (f"p{nfac}",), _bf))
    return steps


def _rwkv_score_steps(sc_mask):
    return [
        ("scm", ("a2", "r2", "bk"),
         lambda a2, r2, bk: _dot_tb(jnp.concatenate([a2, r2], axis=0), bk) * sc_mask()),
        ("l", ("scm",), lambda scm: scm[0:N2, 0:N2]),
        ("lkb", ("scm",), lambda scm: _bf(scm[:, N2:])),
        ("arb", ("scm",), lambda scm: _bf(scm[N2:, 0:N2])),
        ("lvark", ("lkb", "vst"), _dot),
    ]


def _rwkv_carried_steps(same_head):
    def f_ro(arb, tal, r2, lvark):
        x = _dot(arb, _bf(tal))
        return _bf(r2.astype(F32) + x[:, 0:N2]), x[:, N2:] + lvark[N2:]

    def f_mn(tal, b2, k2, vst):
        mm = _dot_ta(_bf(tal), b2)
        n2 = _dot_ta(vst, k2)
        return _bf(mm[0:N2]), (mm[N2:] + n2) * same_head()

    return [
        ("tal", ("tb", "a2", "lvark"),
         lambda tb, a2, lvark: _dot(tb, jnp.concatenate([a2, _bf(lvark[0:N2])], axis=1))),
        ("ro", ("arb", "tal", "r2", "lvark"), f_ro),
        ("mn", ("tal", "b2", "k2", "vst"), f_mn),
    ]


def _side_casts(side, nsteps, step_of):
    in_specs, out_specs, out_shapes = [], [], []
    for a, layer in side:
        rows, cols = a.shape[-2:]
        slab = rows // nsteps
        assert rows % nsteps == 0 and slab % (2 * SUBLANES) == 0
        if layer is None:
            in_specs.append(pl.BlockSpec((slab, cols), lambda *g: (step_of(*g), 0)))
        else:
            in_specs.append(pl.BlockSpec((None, slab, cols), lambda *g, layer=layer: (layer, step_of(*g), 0)))
        out_specs.append(pl.BlockSpec((slab, cols), lambda *g: (step_of(*g), 0)))
        out_shapes.append(jax.ShapeDtypeStruct((rows, cols), BF16))
    return in_specs, out_specs, out_shapes, [a for a, _ in side]


def _with_side_casts(body, n_in, n_out, nside):
    def kernel(*refs):
        ins, rest = refs[:n_in], refs[n_in:]
        side_in, rest = rest[:nside], rest[nside:]
        outs, rest = rest[:n_out], rest[n_out:]
        side_out, scratch = rest[:nside], rest[nside:]
        for src, dst in zip(side_in, side_out):
            dst[...] = _bf(src[...])
        body(*ins, *outs, *scratch)
    return kernel


def _inproj_kernel(x_ref, w_ref, pa_ref, pb_ref):
    xb = _bf(x_ref[...])
    pa_ref[...] = _dot(xb, w_ref[:, 0:A_PROJ])
    pb_ref[...] = _dot(xb, w_ref[:, A_PROJ:])


def _inproj(x2d, w, tm):
    m = x2d.shape[0]
    return pl.pallas_call(
        _inproj_kernel,
        grid=(m // tm,),
        in_specs=[
            pl.BlockSpec((tm, D_MODEL), lambda i: (i, 0)),
            pl.BlockSpec(w.shape, lambda i: (0, 0), pipeline_mode=pl.Buffered(1)),
        ],
        out_specs=[
            pl.BlockSpec((tm, A_PROJ), lambda i: (i, 0)),
            pl.BlockSpec((tm, B_PROJ), lambda i: (i, 0)),
        ],
        out_shape=[
            jax.ShapeDtypeStruct((m, A_PROJ), F32),
            jax.ShapeDtypeStruct((m, B_PROJ), F32),
        ],
        compiler_params=pltpu.CompilerParams(
            dimension_semantics=("arbitrary",), vmem_limit_bytes=VMEM_LIMIT),
        name="inproj_ab",
    )(x2d, w)


def _rwkv_kernel(nb, tt, ct, pa_ref, shift_ref, s0_ref, mu_ref, w0_ref, w2p_ref, a0_ref, a2p_ref,
                 g2_ref, kk_ref, ka_ref, rk_ref, lng_ref, lnb_ref,
                 ya_ref, sout_ref,
                 s_scr, prev_scr, at_scr, rt_scr, bt_scr, kt_scr, v_scr, p_scr, o_scr,
                 scmask_scr, eye_scr, head_scr, lcum_scr):
    rows = nb * tt
    nchunk = rows // ROWS
    ns = ROWS // ct
    ti = pl.program_id(1)

    @pl.when((pl.program_id(0) == 0) & (ti == 0))
    def _():
        strict, incl, eye, same_head = _stream_masks(ct, 2 * N2)
        one = lambda m: jnp.where(m, 1.0, 0.0)
        scmask_scr[...] = jnp.concatenate([one(strict), one(incl)], axis=0)
        eye_scr[...] = one(eye)[:, 0:N2]
        head_scr[...] = one(same_head)[:, 0:N2]
        ri = _iota2(lcum_scr.shape, 0)
        ci = _iota2(lcum_scr.shape, 1)
        lcum_scr[...] = jnp.where((_div2(ri, ct) == _div2(ci, ct)) & (ci <= ri), 1.0, 0.0).astype(BF16)

    @pl.when(ti == 0)
    def _():
        _load_state(s_scr, s0_ref, nb)
        prev_scr[...] = shift_ref[...]

    ones_bd = _head_ones()

    p3 = pa_ref[...]
    p2 = p3.reshape(rows, A_PROJ)
    t3 = _iota2((nb, tt, A_PROJ), 1)
    rolled = pltpu.roll(p2, 1, axis=0).reshape(nb, tt, A_PROJ)
    pprev = jnp.where(t3 == 0, prev_scr[...], rolled).reshape(rows, A_PROJ)
    prev_scr[...] = p3[:, tt - 1:tt, :]
    m = p2 + (pprev - p2) * mu_ref[...]
    r = m[:, 0:WIDTH]
    k = m[:, WIDTH:2 * WIDTH]
    v = m[:, 2 * WIDTH:3 * WIDTH]
    wa = m[:, 3 * WIDTH:3 * WIDTH + LANES]
    gd = m[:, 3 * WIDTH + LANES:A_PROJ]
    z = -(w0_ref[...] + _dot(_bf(jnp.tanh(wa)), w2p_ref[...]))
    softplus = jnp.maximum(z, 0.0) + jnp.log(1.0 + jnp.exp(-jnp.abs(z)))
    lw = -jnp.exp(-softplus - 0.5)
    a = jax.nn.sigmoid(a0_ref[...] + _dot(_bf(wa), a2p_ref[...]))
    g = _dot(_bf(jax.nn.sigmoid(gd)), g2_ref[...])
    kk = k * kk_ref[...]
    k = k * (1.0 + (a - 1.0) * ka_ref[...])
    kk = kk * lax.rsqrt(jnp.maximum(_headsum(kk * kk, ones_bd), 1e-24))
    bonus = _headsum(r * k * rk_ref[...], ones_bd) * v

    span = lcum_scr.shape[0]
    cs = jnp.concatenate([_dot_split_lhs(lcum_scr[...], lw[i * span:(i + 1) * span], 2)
                          for i in range(rows // span)], axis=0)
    pinv = jnp.exp(-cs)
    p_scr[...] = jnp.exp(cs)
    at_scr[...] = _bf(-kk * jnp.exp(cs - lw))
    rt_scr[...] = _bf(r * p_scr[...])
    bt_scr[...] = _bf(kk * a * pinv)
    kt_scr[...] = _bf(k * pinv)
    v_scr[...] = _bf(v)

    lane_lo = _iota2((ROWS, LANES), 1) < HEAD_DIM
    sc_mask = lambda: scmask_scr[...]
    eye = lambda: eye_scr[...]
    same_head = lambda: head_scr[...]
    nfac = ct.bit_length() - 1

    envs = []
    for c in range(nchunk):
        for j in range(PAIRS):
            rsl = slice(c * ROWS, (c + 1) * ROWS)
            lsl = slice(LANES * j, LANES * (j + 1))
            aj, rj, bj, kj, vj = (s[rsl, lsl] for s in (at_scr, rt_scr, bt_scr, kt_scr, v_scr))
            envs.append({
                "c": c, "j": j, "rsl": rsl, "lsl": lsl,
                "a2": _bdiag(aj, lane_lo), "r2": _bdiag(rj, lane_lo),
                "b2": _bdiag(bj, lane_lo), "k2": _bdiag(kj, lane_lo),
                "bk": jnp.concatenate([bj, bj, kj, kj], axis=0),
                "vst": jnp.concatenate([vj, vj], axis=0),
            })

    steps = _rwkv_score_steps(sc_mask) + _inverse_steps(nfac, eye)
    if ns == 1:
        _emit_by_level(steps + _rwkv_carried_steps(same_head), envs)
        per_seq = tt // ROWS
        state = [[s_scr[s, j] for j in range(PAIRS)] for s in range(nb)]
        for cpos in range(per_seq):
            for s in range(nb):
                c = s * per_seq + cpos
                cenv = envs[c * PAIRS:(c + 1) * PAIRS]
                sb = [_bf(x) for x in state[s]]
                for j, env in enumerate(cenv):
                    rp, oc = env["ro"]
                    o_scr[env["rsl"], env["lsl"]] = _unstack_heads(_dot_tb(rp, sb[j]) + oc, lane_lo)
                for j, env in enumerate(cenv):
                    mk, nn = env["mn"]
                    pc = p_scr[(c + 1) * ROWS - 1:(c + 1) * ROWS, env["lsl"]]
                    state[s][j] = (state[s][j] + _dot(sb[j], mk) + nn) * pc
        for s in range(nb):
            for j in range(PAIRS):
                s_scr[s, j] = state[s][j]
    else:
        for env in envs:
            env["sb"] = [_bf(s_scr[env["c"] * ns + q, env["j"]]) for q in range(ns)]

        def f_h(a2, r2, sb):
            pieces = [_dot_tb(jnp.concatenate([_seq_rows(a2, q, ct), _seq_rows(r2, q, ct)], axis=0), sb[q])
                      for q in range(ns)]
            return (_from_seq_rows([x[0:2 * ct] for x in pieces], ct),
                    _from_seq_rows([x[2 * ct:] for x in pieces], ct))

        def f_ds(u2, vst, b2, k2):
            vst32 = vst.astype(F32)
            out = []
            for q in range(ns):
                uv = jnp.concatenate([_seq_rows(u2, q, ct), _seq_rows(vst32, q, ct)], axis=0)
                bk = jnp.concatenate([_seq_rows(b2, q, ct), _seq_rows(k2, q, ct)], axis=0)
                out.append(_dot_ta(_bf(_pad_rows(uv, LANES)), _pad_rows(bk, LANES)) * same_head())
            return out

        steps += [
            ("h", ("a2", "r2", "sb"), f_h),
            ("u2", ("tb", "h", "lvark"), lambda tb, h, lvark: _dot(tb, _bf(h[0] + lvark[0:N2]))),
            ("o2", ("arb", "u2", "h", "lvark"),
             lambda arb, u2, h, lvark: h[1] + _dot(arb, _bf(u2)) + lvark[N2:]),
            ("ds", ("u2", "vst", "b2", "k2"), f_ds),
        ]
        _emit_by_level(steps, envs)
        for env in envs:
            o_scr[env["rsl"], env["lsl"]] = _unstack_heads(env["o2"], lane_lo)
            for q in range(ns):
                seq = env["c"] * ns + q
                pc = p_scr[(seq + 1) * ct - 1:(seq + 1) * ct, env["lsl"]]
                s_scr[seq, env["j"]] = (s_scr[seq, env["j"]] + env["ds"][q]) * pc

    o = o_scr[...]
    y = (_head_norm(o, lng_ref[...], lnb_ref[...], A_GN_EPS, ones_bd) + bonus) * g
    ya_ref[...] = _bf(y).reshape(nb, tt, WIDTH)

    @pl.when(ti == pl.num_programs(1) - 1)
    def _():
        _store_state(sout_ref, s_scr, nb)


def _row_spec(shape):
    return pl.BlockSpec(shape, lambda b, t: (0,) * len(shape))


def _check_mixer_blocking(bsz, tlen, nb, tt):
    ct = min(ROWS, tt)
    rows = nb * tt
    assert rows % ROWS == 0 and ROWS % ct == 0 and bsz % nb == 0 and tlen % tt == 0
    assert tt % ROWS == 0 or tt == tlen
    return ct, rows


def _rwkv(pa3, shift0, s0, prm, nb, tt, side=()):
    bsz, tlen, _ = pa3.shape
    ct, rows = _check_mixer_blocking(bsz, tlen, nb, tt)
    span = tt if tt % ROWS == 0 else rows
    grid = (bsz // nb, tlen // tt)
    s_in, s_out, s_shapes, s_args = _side_casts(side, grid[0] * grid[1], lambda b, t: b * grid[1] + t)
    state_spec = pl.BlockSpec((nb, HEADS, HEAD_DIM, HEAD_DIM), lambda b, t: (b, 0, 0, 0))
    vec = lambda n: _row_spec((1, n))
    in_specs = [
        pl.BlockSpec((nb, tt, A_PROJ), lambda b, t: (b, t, 0)),
        pl.BlockSpec((nb, 1, A_PROJ), lambda b, t: (b, 0, 0)),
        state_spec,
        vec(A_PROJ), vec(WIDTH), _row_spec((LANES, WIDTH)), vec(WIDTH), _row_spec((LANES, WIDTH)),
        _row_spec((GATE_LORA, WIDTH)), vec(WIDTH), vec(WIDTH), vec(WIDTH), vec(WIDTH), vec(WIDTH),
    ]
    out_specs = [
        pl.BlockSpec((nb, tt, WIDTH), lambda b, t: (b, t, 0)),
        state_spec,
    ]
    scratch = [
        pltpu.VMEM((nb, PAIRS, LANES, LANES), F32),
        pltpu.VMEM((nb, 1, A_PROJ), F32),
        pltpu.VMEM((rows, WIDTH), BF16),
        pltpu.VMEM((rows, WIDTH), BF16),
        pltpu.VMEM((rows, WIDTH), BF16),
        pltpu.VMEM((rows, WIDTH), BF16),
        pltpu.VMEM((rows, WIDTH), BF16),
        pltpu.VMEM((rows, WIDTH), F32),
        pltpu.VMEM((rows, WIDTH), F32),
        pltpu.VMEM((2 * N2, 2 * N2), F32),
        pltpu.VMEM((N2, N2), F32),
        pltpu.VMEM((N2, N2), F32),
        pltpu.VMEM((span, span), BF16),
    ]
    return pl.pallas_call(
        _with_side_casts(functools.partial(_rwkv_kernel, nb, tt, ct), len(in_specs), len(out_specs), len(side)),
        grid=grid,
        in_specs=in_specs + s_in,
        out_specs=out_specs + s_out,
        out_shape=[
            jax.ShapeDtypeStruct((bsz, tlen, WIDTH), BF16),
            jax.ShapeDtypeStruct(s0.shape, F32),
        ] + s_shapes,
        scratch_shapes=scratch,
        compiler_params=pltpu.CompilerParams(
            dimension_semantics=("arbitrary", "arbitrary"), vmem_limit_bytes=VMEM_LIMIT),
        name="rwkv7_mixer",
    )(pa3, shift0.reshape(bsz, 1, A_PROJ), s0, *prm, *s_args)


def _ret_kernel(nb, tt, ct, pos0, pb_ref, s0_ref, invf_ref, gng_ref, gnb_ref,
                yb_ref, sout_ref,
                s_scr, q_scr, k_scr, v_scr, o_scr, dec_scr, head_scr, trig_scr):
    rows = nb * tt
    nchunk = rows // ROWS
    ns = ROWS // ct
    ti = pl.program_id(1)

    @pl.when((pl.program_id(0) == 0) & (ti == 0))
    def _():
        _, incl, _, same_head = _stream_masks(ct, N2)
        head_scr[...] = jnp.where(same_head, 1.0, 0.0)
        ri = _iota2((N2, N2), 0)
        ci = _iota2((N2, N2), 1)
        tpos = _mod2(ri, ct).astype(F32)
        tdiff = (_mod2(ri, ct) - _mod2(ci, ct)).astype(F32)
        for j in range(PAIRS):
            head = (2 * j + _div2(ri, ROWS)).astype(F32)
            lg = jnp.log1p(-jnp.exp2(-5.0 - head))
            dec_scr[j, 0] = jnp.where(incl, jnp.exp(lg * jnp.maximum(tdiff, 0.0)), 0.0)
            dec_scr[j, 1] = jnp.exp(lg * (tpos + 1.0))
            dec_scr[j, 2] = jnp.exp(lg * (ct - 1.0 - tpos))
            dec_scr[j, 3] = jnp.exp(lg * ct)
        off = _mod2(_iota2((rows, LANES), 0), tt).astype(F32) * invf_ref[...]
        trig_scr[0] = jnp.cos(off)
        trig_scr[1] = jnp.sin(off)

    @pl.when(ti == 0)
    def _():
        _load_state(s_scr, s0_ref, nb)

    ones_bd = _head_ones()
    pb = pb_ref[...].reshape(rows, B_PROJ)
    q = pb[:, 0:WIDTH]
    k = pb[:, WIDTH:2 * WIDTH]
    gate = pb[:, 3 * WIDTH:]

    base = jnp.zeros((SUBLANES, LANES), F32) + (pos0 + ti * tt).astype(F32)
    ang = base * invf_ref[...]
    cos_a, sin_a = jnp.cos(ang)[0:1], jnp.sin(ang)[0:1]
    cos_b, sin_b = trig_scr[0], trig_scr[1]
    cos = jnp.concatenate([cos_a * cos_b - sin_a * sin_b] * PAIRS, axis=1)
    sin = jnp.concatenate([sin_a * cos_b + cos_a * sin_b] * PAIRS, axis=1)
    first_half = _mod2(_iota2((rows, WIDTH), 1), HEAD_DIM) < (HEAD_DIM // 2)
    sin = jnp.where(first_half, -sin, sin)

    pr = _iota2((LANES, LANES), 0)
    pc = _iota2((LANES, LANES), 1)
    swap = jnp.where(pr == lax.bitwise_xor(pc, HEAD_DIM // 2), 1.0, 0.0).astype(BF16)

    def rope(x):
        xb = _bf(x)
        partner = jnp.concatenate([_dot(xb[:, LANES * j:LANES * (j + 1)], swap) for j in range(PAIRS)], axis=1)
        return x * cos + partner * sin

    q_scr[...] = _bf(rope(q))
    k_scr[...] = rope(k) * (HEAD_DIM ** -0.5)
    v_scr[...] = _bf(pb[:, 2 * WIDTH:3 * WIDTH])

    lane_lo = _iota2((ROWS, LANES), 1) < HEAD_DIM
    same_head = lambda: head_scr[...]

    envs = []
    for c in range(nchunk):
        for j in range(PAIRS):
            rsl = slice(c * ROWS, (c + 1) * ROWS)
            lsl = slice(LANES * j, LANES * (j + 1))
            qj, kj, vj = q_scr[rsl, lsl], k_scr[rsl, lsl], v_scr[rsl, lsl]
            kjb = _bf(kj)
            envs.append({
                "c": c, "j": j, "rsl": rsl, "lsl": lsl,
                "q2": _bdiag(qj, lane_lo),
                "kst": jnp.concatenate([kjb, kjb], axis=0),
                "k2d": _bdiag(kj, lane_lo) * dec_scr[j, 2],
                "vst": jnp.concatenate([vj, vj], axis=0),
                "dmask": lambda j=j: dec_scr[j, 0],
            })

    def f_ds(k2d, vst):
        return [_dot_ta(_bf(_pad_rows(_seq_rows(k2d, s, ct), LANES)), _pad_rows(_seq_rows(vst, s, ct), LANES))
                * same_head() for s in range(ns)]

    steps = [
        ("sc", ("q2", "kst", "dmask"), lambda q2, kst, dmask: _bf(_dot_tb(q2, kst) * dmask())),
        ("inner", ("sc", "vst"), _dot),
        ("ds", ("k2d", "vst"), f_ds),
    ]
    _emit_by_level(steps, envs)

    if ns == 1:
        per_seq = tt // ROWS
        for s in range(nb):
            state = [s_scr[s, j] for j in range(PAIRS)]
            for c in range(s * per_seq, (s + 1) * per_seq):
                for j in range(PAIRS):
                    env = envs[c * PAIRS + j]
                    env["sb"] = [_bf(state[j])]
                    state[j] = state[j] * dec_scr[j, 3] + env["ds"][0]
            for j in range(PAIRS):
                s_scr[s, j] = state[j]
    else:
        for env in envs:
            j = env["j"]
            first = env["c"] * ns
            env["sb"] = [_bf(s_scr[first + s, j]) for s in range(ns)]
            for s in range(ns):
                s_scr[first + s, j] = s_scr[first + s, j] * dec_scr[j, 3] + env["ds"][s]

    def f_cross(q2, sb):
        return _from_seq_rows([_dot(_seq_rows(q2, s, ct), sb[s]) for s in range(ns)], ct)

    _emit_by_level([("cross", ("q2", "sb"), f_cross)], envs)
    for env in envs:
        o2 = env["inner"] + env["cross"] * dec_scr[env["j"], 1]
        o_scr[env["rsl"], env["lsl"]] = _unstack_heads(o2, lane_lo)

    o = o_scr[...]
    y = jax.nn.silu(gate) * _head_norm(o, gng_ref[...], gnb_ref[...], B_GN_EPS, ones_bd)
    yb_ref[...] = _bf(y).reshape(nb, tt, WIDTH)

    @pl.when(ti == pl.num_programs(1) - 1)
    def _():
        _store_state(sout_ref, s_scr, nb)


def _retention(pb3, s0, invf, gn_g, gn_b, nb, tt, pos0, side=()):
    bsz, tlen, _ = pb3.shape
    ct, rows = _check_mixer_blocking(bsz, tlen, nb, tt)
    grid = (bsz // nb, tlen // tt)
    s_in, s_out, s_shapes, s_args = _side_casts(side, grid[0] * grid[1], lambda b, t: b * grid[1] + t)
    state_spec = pl.BlockSpec((nb, HEADS, HEAD_DIM, HEAD_DIM), lambda b, t: (b, 0, 0, 0))
    in_specs = [
        pl.BlockSpec((nb, tt, B_PROJ), lambda b, t: (b, t, 0)),
        state_spec,
        _row_spec((1, LANES)), _row_spec((1, WIDTH)), _row_spec((1, WIDTH)),
    ]
    out_specs = [
        pl.BlockSpec((nb, tt, WIDTH), lambda b, t: (b, t, 0)),
        state_spec,
    ]
    scratch = [
        pltpu.VMEM((nb, PAIRS, LANES, LANES), F32),
        pltpu.VMEM((rows, WIDTH), BF16),
        pltpu.VMEM((rows, WIDTH), F32),
        pltpu.VMEM((rows, WIDTH), BF16),
        pltpu.VMEM((rows, WIDTH), F32),
        pltpu.VMEM((PAIRS, 4, N2, N2), F32),
        pltpu.VMEM((N2, N2), F32),
        pltpu.VMEM((2, rows, LANES), F32),
    ]
    return pl.pallas_call(
        _with_side_casts(functools.partial(_ret_kernel, nb, tt, ct, pos0), len(in_specs), len(out_specs),
                         len(side)),
        grid=grid,
        in_specs=in_specs + s_in,
        out_specs=out_specs + s_out,
        out_shape=[
            jax.ShapeDtypeStruct((bsz, tlen, WIDTH), BF16),
            jax.ShapeDtypeStruct(s0.shape, F32),
        ] + s_shapes,
        scratch_shapes=scratch,
        compiler_params=pltpu.CompilerParams(
            dimension_semantics=("arbitrary", "arbitrary"), vmem_limit_bytes=VMEM_LIMIT),
        name="retention_mixer",
    )(pb3, s0, invf, gn_g, gn_b, *s_args)


CONV_COLS = 256


def _conv_kernel(nb, tt, nsplit, x_ref, buf_ref, win_ref, cw_ref, z_ref, bout_ref, halo_scr):
    rows = nb * tt
    sub = rows // nsplit
    st = tt // nsplit if nb == 1 else tt
    ti = pl.program_id(1)

    @pl.when(ti == 0)
    def _():
        halo_scr[...] = buf_ref[...]

    t3 = _iota2((nb, st, CONV_COLS), 1)

    def project(xb, lo):
        col = lambda k: win_ref[:, k * D_MODEL + lo:k * D_MODEL + lo + CONV_COLS]
        return _dot(xb, col(0)), _dot(xb, col(1)) * _dot(xb, col(2))

    def mix(i, lo, h0, h1, bg, u):
        cw = cw_ref[:, lo:lo + CONV_COLS]
        u3 = u.reshape(nb, st, CONV_COLS)
        prev1 = jnp.where(t3 == 0, h1, pltpu.roll(u, 1, axis=0).reshape(nb, st, CONV_COLS))
        prev2 = jnp.where(t3 == 0, h0,
                          jnp.where(t3 == 1, h1, pltpu.roll(u, 2, axis=0).reshape(nb, st, CONV_COLS)))
        conv = prev2 * cw[0:1, :] + prev1 * cw[1:2, :] + u3 * cw[2:3, :]
        z_ref[i * sub:(i + 1) * sub, lo:lo + CONV_COLS] = _bf(bg * conv.reshape(sub, CONV_COLS))

    halo = {lo: (halo_scr[:, 0:1, lo:lo + CONV_COLS], halo_scr[:, 1:2, lo:lo + CONV_COLS])
            for lo in range(0, D_MODEL, CONV_COLS)}
    pending = None
    for i in range(nsplit):
        xi = x_ref[0, i * sub:(i + 1) * sub, :] if nb == 1 else x_ref[...].reshape(rows, D_MODEL)
        xb = _bf(xi)
        for lo in range(0, D_MODEL, CONV_COLS):
            bg, u = project(xb, lo)
            if pending is not None:
                mix(*pending)
            pending = (i, lo, *halo[lo], bg, u)
            u3 = u.reshape(nb, st, CONV_COLS)
            halo[lo] = (u3[:, st - 2:st - 1, :], u3[:, st - 1:st, :])
    mix(*pending)
    for lo, (h0, h1) in halo.items():
        halo_scr[:, 0:1, lo:lo + CONV_COLS] = h0
        halo_scr[:, 1:2, lo:lo + CONV_COLS] = h1

    @pl.when(ti == pl.num_programs(1) - 1)
    def _():
        bout_ref[...] = halo_scr[...]


def _conv_mixer(x3, buf0, win, cw, nb, tt, nsplit):
    bsz, tlen, _ = x3.shape
    rows = nb * tt
    assert bsz % nb == 0 and tlen % tt == 0
    assert nsplit == 1 or nb == 1
    assert tt % nsplit == 0 and tt // nsplit >= CONV_W - 1
    grid = (bsz // nb, tlen // tt)
    buf_spec = pl.BlockSpec((nb, CONV_W - 1, D_MODEL), lambda b, t: (b, 0, 0))
    return pl.pallas_call(
        functools.partial(_conv_kernel, nb, tt, nsplit),
        grid=grid,
        in_specs=[
            pl.BlockSpec((nb, tt, D_MODEL), lambda b, t: (b, t, 0)),
            buf_spec,
            pl.BlockSpec(win.shape, lambda b, t: (0, 0), pipeline_mode=pl.Buffered(1)),
            _row_spec((CONV_W, D_MODEL)),
        ],
        out_specs=[
            pl.BlockSpec((rows, D_MODEL), lambda b, t: (b * (tlen // tt) + t, 0)),
            buf_spec,
        ],
        out_shape=[
            jax.ShapeDtypeStruct((bsz * tlen, D_MODEL), BF16),
            jax.ShapeDtypeStruct(buf0.shape, F32),
        ],
        scratch_shapes=[pltpu.VMEM((nb, CONV_W - 1, D_MODEL), F32)],
        compiler_params=pltpu.CompilerParams(
            dimension_semantics=("arbitrary", "arbitrary"), vmem_limit_bytes=VMEM_LIMIT),
        name="conv_mixer",
    )(x3, buf0, win, cw)


FF_CHUNK = 1024


def _post_kernel(npieces, nsplit, stream, *refs):
    if stream:
        *refs, wout_vm, wup_vm, wdown_vm, sem = refs
    x_ref = refs[0]
    y_refs = refs[1:1 + npieces]
    wout_ref, g1_ref, b1_ref, wup_ref, wdown_ref, g2_ref, b2_ref, out_ref = refs[1 + npieces:]
    sub = x_ref.shape[0] // nsplit
    nff = D_FF // FF_CHUNK
    ff = lambda f: slice(f * FF_CHUNK, (f + 1) * FF_CHUNK)

    copies = {}
    if stream:
        copies["out"] = pltpu.make_async_copy(wout_ref, wout_vm, sem.at[0])
        for f in range(nff):
            copies["up", f] = pltpu.make_async_copy(wup_ref.at[:, ff(f)], wup_vm.at[:, ff(f)], sem.at[1 + 2 * f])
            copies["down", f] = pltpu.make_async_copy(wdown_ref.at[ff(f), :], wdown_vm.at[ff(f), :],
                                                      sem.at[2 + 2 * f])
        for copy in copies.values():
            copy.start()
        wout_ref, wup_ref, wdown_ref = wout_vm, wup_vm, wdown_vm

    def arrived(piece):
        if piece in copies:
            copies.pop(piece).wait()

    def f_y(r):
        arrived("out")
        y = None
        off = 0
        for y_ref in y_refs:
            n = y_ref.shape[1]
            term = _dot(y_ref[r, :], wout_ref[off:off + n, :])
            y = term if y is None else y + term
            off += n
        return y

    def f_mlp(f):
        def step(x1b, acc):
            arrived(("up", f))
            h = jnp.maximum(_dot(x1b, wup_ref[:, ff(f)]), 0.0)
            arrived(("down", f))
            term = _dot(_bf(h * h), wdown_ref[ff(f), :])
            return term if acc is None else acc + term
        return step

    def f_out(r, x1, acc):
        out_ref[r, :] = _layer_norm(ALPHA * x1 + acc, g2_ref[...], b2_ref[...])
        return None

    steps = [
        ("y", ("r",), f_y),
        ("x1", ("r", "y"), lambda r, y: _layer_norm(ALPHA * x_ref[r, :] + y, g1_ref[...], b1_ref[...])),
        ("x1b", ("x1",), _bf),
        ("acc-1", ("x1b",), lambda x1b: None),
    ]
    for f in range(nff):
        steps.append((f"acc{f}", ("x1b", f"acc{f - 1}"), f_mlp(f)))
    steps.append(("out", ("r", "x1", f"acc{nff - 1}"), f_out))
    envs = [{"r": slice(i * sub, (i + 1) * sub)} for i in range(nsplit)]
    _emit_by_level(steps, envs)
    assert not copies


def _post(x2d, ys, wout, g1, b1, wup, wdown, g2, b2, tm, nsplit):
    m = x2d.shape[0]
    assert m % tm == 0 and tm % nsplit == 0
    stream = m == tm
    const = lambda i: (0, 0)
    if stream:
        resident = lambda a: pl.BlockSpec(memory_space=pl.ANY)
        scratch = [pltpu.VMEM(a.shape, a.dtype) for a in (wout, wup, wdown)]
        scratch.append(pltpu.SemaphoreType.DMA((1 + 2 * (D_FF // FF_CHUNK),)))
    else:
        resident = lambda a: pl.BlockSpec(a.shape, const, pipeline_mode=pl.Buffered(1))
        scratch = []
    vec = pl.BlockSpec((1, D_MODEL), const)
    in_specs = [pl.BlockSpec((tm, D_MODEL), lambda i: (i, 0))]
    in_specs += [pl.BlockSpec((tm, y.shape[1]), lambda i: (i, 0)) for y in ys]
    in_specs += [resident(wout), vec, vec, resident(wup), resident(wdown), vec, vec]
    return pl.pallas_call(
        functools.partial(_post_kernel, len(ys), nsplit, stream),
        grid=(m // tm,),
        in_specs=in_specs,
        out_specs=pl.BlockSpec((tm, D_MODEL), lambda i: (i, 0)),
        out_shape=jax.ShapeDtypeStruct((m, D_MODEL), F32),
        scratch_shapes=scratch,
        compiler_params=pltpu.CompilerParams(
            dimension_semantics=("arbitrary",), vmem_limit_bytes=VMEM_LIMIT),
        name="post_block",
    )(x2d, *ys, wout, g1, b1, wup, wdown, g2, b2)


def _trunk(x, st_shift, st_wkv, st_ret, st_conv, pos0, blk, w, dense):
    bsz, tlen, _ = x.shape
    m = bsz * tlen
    x2d = x.reshape(m, D_MODEL)
    row = lambda a: a.reshape(1, -1)
    cast = dense is None
    dense = {} if cast else dense
    side = lambda *items: items if cast else ()

    pa, pb = _inproj(x2d, w["w_in_ab"], blk["tm_in"])
    pa3 = pa.reshape(bsz, tlen, A_PROJ)
    rw_prm = (row(w["mu_a"]), row(w["w0"]), w["w2p"], row(w["a0"]), w["a2p"], w["g2"],
              row(w["k_k"]), row(w["k_a"]), row(w["r_k"]), row(w["lnx_g"]), row(w["lnx_b"]))
    ya, wkv1, *made = _rwkv(pa3, st_shift, st_wkv, rw_prm, blk["nb_mix"], blk["tt_mix"],
                            side((w["w_out_ab"], 0), (w["w_up"], 0), (w["w_down"], 0)))
    if cast:
        dense["w_out_ab"], dense["w_up0"], dense["w_down0"] = made
    yb, ret1, *made = _retention(pb.reshape(bsz, tlen, B_PROJ), st_ret, w["invf"], row(w["gn_g"]),
                                 row(w["gn_b"]), blk["nb_mix"], blk["tt_mix"], pos0,
                                 side((w["w_in_conv"], 0), (w["w_out_conv"], 0), (w["w_up"], 1), (w["w_down"], 1)))
    if cast:
        dense["w_in_conv"], dense["w_out_conv"], dense["w_up1"], dense["w_down1"] = made
    shift1 = pa3[:, tlen - 1, :]
    x2d = _post(x2d, [ya.reshape(m, WIDTH), yb.reshape(m, WIDTH)], dense["w_out_ab"],
                row(w["ln1_g"][0]), row(w["ln1_b"][0]), dense["w_up0"], dense["w_down0"],
                row(w["ln2_g"][0]), row(w["ln2_b"][0]), blk["tm_post"], blk["split"])

    z, conv1 = _conv_mixer(x2d.reshape(bsz, tlen, D_MODEL), st_conv, dense["w_in_conv"], w["conv_w"],
                           blk["nb_conv"], blk["tt_conv"], blk["split_conv"])
    x2d = _post(x2d, [z], dense["w_out_conv"], row(w["ln1_g"][1]), row(w["ln1_b"][1]),
                dense["w_up1"], dense["w_down1"], row(w["ln2_g"][1]), row(w["ln2_b"][1]),
                blk["tm_post"], blk["split"])
    outs = (x2d.reshape(bsz, tlen, D_MODEL), shift1[None], wkv1[None], ret1[None], conv1[None])
    return outs, dense


def kernel(x_prompt, x_sample, state_shift, state_wkv, state_ret, state_conv, w_in_ab, mu_a, w0, w2, a0, a2,
           g2, k_k, k_a, r_k, lnx_g, lnx_b, gn_g, gn_b, w_out_ab, w_in_conv, conv_w, w_out_conv,
           ln1_g, ln1_b, ln2_g, ln2_b, w_up, w_down):
    bp = x_prompt.shape[0]
    half = HEAD_DIM // 2
    inv = ROPE_BASE ** (-jnp.arange(half, dtype=F32) / half)
    zpad = jnp.zeros((LANES - DECAY_LORA, WIDTH), F32)
    w = {
        "w_in_ab": _bf(w_in_ab[0]),
        "mu_a": mu_a[0], "w0": w0[0], "a0": a0[0],
        "w2p": _bf(jnp.concatenate([w2[0], zpad], axis=0)),
        "a2p": _bf(jnp.concatenate([zpad, a2[0]], axis=0)),
        "g2": _bf(g2[0]), "k_k": k_k[0], "k_a": k_a[0], "r_k": r_k[0],
        "lnx_g": lnx_g[0], "lnx_b": lnx_b[0], "gn_g": gn_g[0], "gn_b": gn_b[0],
        "invf": jnp.tile(inv, LANES // half).reshape(1, LANES),
        "w_out_ab": w_out_ab, "w_in_conv": w_in_conv, "conv_w": conv_w[0], "w_out_conv": w_out_conv,
        "ln1_g": ln1_g, "ln1_b": ln1_b, "ln2_g": ln2_g, "ln2_b": ln2_b,
        "w_up": w_up, "w_down": w_down,
    }
    dt = state_wkv.dtype
    z_shift = jnp.zeros((bp, A_PROJ), dt)
    z_state = jnp.zeros((bp, HEADS, HEAD_DIM, HEAD_DIM), dt)
    z_conv = jnp.zeros((bp, CONV_W - 1, D_MODEL), dt)
    blk_p = {"tm_in": 1024, "tm_post": 1024, "split": 4, "nb_mix": 2, "tt_mix": 256,
             "nb_conv": 1, "tt_conv": 1024, "split_conv": 1}
    (y_p, p_shift, p_wkv, p_ret, p_conv), dense = _trunk(x_prompt, z_shift, z_state, z_state, z_conv, 0,
                                                         blk_p, w, None)
    ts = x_sample.shape[1]
    blk_s = {"tm_in": 512, "tm_post": 1024, "split": 4, "nb_mix": 2 * ROWS // ts, "tt_mix": ts,
             "nb_conv": 256 // ts, "tt_conv": ts, "split_conv": 1}
    (y_s, s_shift, s_wkv, s_ret, s_conv), _ = _trunk(x_sample, state_shift[0], state_wkv[0], state_ret[0],
                                                     state_conv[0], PAST_LEN, blk_s, w, dense)
    return (y_p, y_s, p_shift, p_wkv, p_ret, p_conv, s_shift, s_wkv, s_ret, s_conv)
```
